```python
import jax, jax.numpy as jnp
from jax import lax
import numpy as np

D_MODEL = 1024
BATCH = 8
SEQ = 2048
DEPTH = 1
DEC_BATCH = 128
DEC_SEQ = 4
PAST_LEN = 16384
PAGE_SIZE = 128

N_META = 16
EPS = 1e-6
H_A = 8
DK_A = 128
DV_A = 128
D_QK_A = H_A * DK_A
D_A = H_A * DV_A
CONV_W = 4
GDN_CHUNK = 64
GDN_QKV = 2 * D_QK_A + D_A
H_B = 16
N_B = 64
D_B = H_B * N_B
LORA_W = 64
LORA_A = 64
GN_EPS = N_B * 1e-5
RWKV_SHIFT = 3 * D_B + LORA_W + LORA_A + D_B
OFF_GDN_QKV = 0
OFF_GDN_A = OFF_GDN_QKV + GDN_QKV
OFF_GDN_B = OFF_GDN_A + H_A
OFF_GDN_Z = OFF_GDN_B + H_A
OFF_RWKV = OFF_GDN_Z + D_A
OFF_GATE = OFF_RWKV + RWKV_SHIFT
D_IN = OFF_GATE + 2 * D_MODEL

kernel_name = "hybrid_gdn_rwkv7_gated_merge_step"


def rmsnorm(x, w):
    xf = x.astype(jnp.float32)
    xf = xf * lax.rsqrt(jnp.mean(xf * xf, axis=-1, keepdims=True) + EPS)
    return (xf * w.astype(jnp.float32)).astype(x.dtype)


def l2norm(x):
    return x * lax.rsqrt(jnp.sum(x * x, axis=-1, keepdims=True) + 1e-6)


def gdn_chunked(q, k, v, g, beta, s0, chunk):
    B, T, H, DK = q.shape
    DV = v.shape[-1]
    n = T // chunk

    def to_chunks(t):
        return jnp.moveaxis(t.reshape((B, n, chunk) + t.shape[2:]), 3, 2)

    q, k, v, g, beta = (to_chunks(t) for t in (q, k, v, g, beta))
    gc = jnp.cumsum(g, axis=-1)
    tril = jnp.tril(jnp.ones((chunk, chunk), dtype=bool))
    strict = jnp.tril(jnp.ones((chunk, chunk), dtype=bool), k=-1)
    diff = gc[..., :, None] - gc[..., None, :]
    decay = jnp.where(tril, jnp.exp(jnp.where(tril, diff, 0.0)), 0.0)
    k_beta = k * beta[..., None]
    v_beta = v * beta[..., None]
    lower = jnp.where(strict, jnp.einsum('bnhik,bnhjk->bnhij', k_beta, k) * decay, 0.0)
    eye = jnp.eye(chunk, dtype=q.dtype)
    t_inv = lax.linalg.triangular_solve(eye + lower, jnp.broadcast_to(eye, lower.shape),
                                        left_side=True, lower=True)
    u = jnp.einsum('bnhij,bnhjv->bnhiv', t_inv, v_beta)
    w = jnp.einsum('bnhij,bnhjk->bnhik', t_inv, k_beta * jnp.exp(gc)[..., None])
    qk = jnp.where(tril, jnp.einsum('bnhik,bnhjk->bnhij', q, k) * decay, 0.0)
    g_last = gc[..., -1]
    k_tail = k * jnp.exp(g_last[..., None] - gc)[..., None]

    def step(s, inp):
        q_c, gc_c, u_c, w_c, qk_c, gl_c, kt_c = inp
        v_new = u_c - jnp.einsum('bhck,bhkv->bhcv', w_c, s)
        o = (jnp.einsum('bhck,bhkv->bhcv', q_c * jnp.exp(gc_c)[..., None], s)
             + jnp.einsum('bhij,bhjv->bhiv', qk_c, v_new))
        s = s * jnp.exp(gl_c)[..., None, None] + jnp.einsum('bhck,bhcv->bhkv', kt_c, v_new)
        return s, o

    xs = tuple(jnp.moveaxis(t, 1, 0) for t in (q, gc, u, w, qk, g_last, k_tail))
    s, o = lax.scan(step, s0, xs)
    o = jnp.transpose(o, (1, 0, 3, 2, 4)).reshape(B, T, H, DV)
    return o, s


def rwkv7_scan(r, w_log, k, v, kk, a, s0):
    def step(s, inp):
        r_t, w_t, k_t, v_t, kk_t, a_t = inp
        sa = jnp.einsum('bhvk,bhk->bhv', s, kk_t)
        s = (s * jnp.exp(w_t)[:, :, None, :] - sa[..., None] * (kk_t * a_t)[:, :, None, :]
             + v_t[..., None] * k_t[:, :, None, :])
        y = jnp.einsum('bhvk,bhk->bhv', s, r_t)
        return s, y

    xs = tuple(jnp.moveaxis(t, 1, 0) for t in (r, w_log, k, v, kk, a))
    s, ys = lax.scan(step, s0, xs)
    return jnp.moveaxis(ys, 0, 1), s


def hybrid_layer(x, x_prev, conv_buf, s_gdn, s_rwkv, front_pad, chunk,
                 ln1_w, w_in, gdn_conv_w, gdn_a_log, gdn_dt_bias, gdn_norm_w, w_out_a,
                 rwkv_mu, rwkv_w0, rwkv_w2, rwkv_a0, rwkv_a2, rwkv_k_k, rwkv_k_a, rwkv_r_k,
                 rwkv_gn_w, rwkv_gn_b, w_out_b, w_out):
    f32 = jnp.float32
    B, T, _ = x.shape
    h = rmsnorm(x, ln1_w)
    p = jnp.einsum('btd,de->bte', h, w_in)

    qkv_raw = p[..., OFF_GDN_QKV:OFF_GDN_A]
    conv_in = jnp.concatenate([conv_buf.astype(p.dtype), qkv_raw], axis=1)
    qkv = conv_in[:, 0:T] * gdn_conv_w[0]
    for i in range(1, CONV_W):
        qkv = qkv + conv_in[:, i:i + T] * gdn_conv_w[i]
    qkv = jax.nn.silu(qkv.astype(f32))
    q = l2norm(qkv[..., :D_QK_A].reshape(B, T, H_A, DK_A)) * (DK_A ** -0.5)
    k = l2norm(qkv[..., D_QK_A:2 * D_QK_A].reshape(B, T, H_A, DK_A))
    v = qkv[..., 2 * D_QK_A:].reshape(B, T, H_A, DV_A)
    g = -jnp.exp(gdn_a_log.astype(f32)) * jax.nn.softplus(
        p[..., OFF_GDN_A:OFF_GDN_B].astype(f32) + gdn_dt_bias.astype(f32))
    beta = jax.nn.sigmoid(p[..., OFF_GDN_B:OFF_GDN_Z].astype(f32))
    pad4 = ((0, 0), (front_pad, 0), (0, 0), (0, 0))
    pad3 = ((0, 0), (front_pad, 0), (0, 0))
    o_a, s_gdn_new = gdn_chunked(jnp.pad(q, pad4), jnp.pad(k, pad4), jnp.pad(v, pad4),
                                 jnp.pad(g, pad3), jnp.pad(beta, pad3), s_gdn.astype(f32), chunk)
    o_a = o_a[:, front_pad:]
    z_a = p[..., OFF_GDN_Z:OFF_RWKV].astype(f32).reshape(B, T, H_A, DV_A)
    o_a = rmsnorm(o_a, gdn_norm_w) * jax.nn.silu(z_a)
    branch_a = jnp.einsum('bte,ed->btd', o_a.reshape(B, T, D_A).astype(x.dtype), w_out_a)

    w_b = w_in[:, OFF_RWKV:OFF_GATE]
    pb = p[..., OFF_RWKV:OFF_GATE]
    pb_first = jnp.einsum('bd,de->be', x_prev.astype(h.dtype), w_b)
    pb_prev = jnp.concatenate([pb_first[:, None].astype(pb.dtype), pb[:, :-1]], axis=1)
    mix = (pb + rwkv_mu * (pb_prev - pb)).astype(f32)
    r = mix[..., 0:D_B]
    kb = mix[..., D_B:2 * D_B]
    vb = mix[..., 2 * D_B:3 * D_B]
    wd = mix[..., 3 * D_B:3 * D_B + LORA_W]
    ad = mix[..., 3 * D_B + LORA_W:3 * D_B + LORA_W + LORA_A]
    z_b = mix[..., 3 * D_B + LORA_W + LORA_A:]
    w_raw = rwkv_w0.astype(f32) + jnp.tanh(wd) @ rwkv_w2.astype(f32)
    w_log = -jnp.exp(-jax.nn.softplus(-w_raw) - 0.5)
    a = jax.nn.sigmoid(rwkv_a0.astype(f32) + ad @ rwkv_a2.astype(f32))
    kk = l2norm((kb * rwkv_k_k.astype(f32)).reshape(B, T, H_B, N_B))
    kb = kb * (1.0 + (a - 1.0) * rwkv_k_a.astype(f32))
    r4 = r.reshape(B, T, H_B, N_B)
    k4 = kb.reshape(B, T, H_B, N_B)
    v4 = vb.reshape(B, T, H_B, N_B)
    y, s_rwkv_new = rwkv7_scan(r4, w_log.reshape(B, T, H_B, N_B), k4, v4, kk,
                               a.reshape(B, T, H_B, N_B), s_rwkv.astype(f32))
    mu = jnp.mean(y, axis=-1, keepdims=True)
    var = jnp.mean(jnp.square(y - mu), axis=-1, keepdims=True)
    yn = (y - mu) * lax.rsqrt(var + GN_EPS)
    yn = yn.reshape(B, T, D_B) * rwkv_gn_w.astype(f32) + rwkv_gn_b.astype(f32)
    bonus = jnp.sum(r4 * k4 * rwkv_r_k.astype(f32), axis=-1, keepdims=True) * v4
    o_b = (yn + bonus.reshape(B, T, D_B)) * jax.nn.silu(z_b)
    branch_b = jnp.einsum('bte,ed->btd', o_b.astype(x.dtype), w_out_b)

    gates = jax.nn.sigmoid(p[..., OFF_GATE:].astype(f32))
    merged = gates[..., :D_MODEL] * branch_a.astype(f32) + gates[..., D_MODEL:] * branch_b.astype(f32)
    x_new = x + jnp.einsum('btd,de->bte', merged.astype(x.dtype), w_out)
    new_state = (s_gdn_new.astype(x.dtype), conv_in[:, T:].astype(x.dtype),
                 s_rwkv_new.astype(x.dtype), h[:, -1])
    return x_new, new_state


def setup_inputs(seed: int = 0) -> dict:
    key = jax.random.key(seed)
    ks = jax.random.split(key, 32)
    f32 = jnp.float32
    nrm = lambda i, shape, s: (jax.random.normal(ks[i], shape, f32) * s)
    dt = jnp.exp(jax.random.uniform(ks[10], (DEPTH, H_A), f32, np.log(1e-3), np.log(1e-1)))
    return {
        "x_prompt": nrm(0, (BATCH, SEQ, D_MODEL), 1.0),
        "x_sample": nrm(1, (DEC_BATCH, DEC_SEQ, D_MODEL), 1.0),
        "state_gdn": nrm(2, (DEPTH, DEC_BATCH, H_A, DK_A, DV_A), 0.1),
        "state_gdn_conv": nrm(3, (DEPTH, DEC_BATCH, CONV_W - 1, GDN_QKV), 1.0),
        "state_rwkv": nrm(4, (DEPTH, DEC_BATCH, H_B, N_B, N_B), 0.1),
        "state_shift": nrm(5, (DEPTH, DEC_BATCH, D_MODEL), 1.0),
        "meta_tokens": nrm(6, (N_META, D_MODEL), 1.0),
        "ln1_w": 1.0 + nrm(7, (DEPTH, D_MODEL), 0.02),
        "w_in": nrm(8, (DEPTH, D_MODEL, D_IN), D_MODEL ** -0.5),
        "gdn_conv_w": nrm(9, (DEPTH, CONV_W, GDN_QKV), CONV_W ** -0.5),
        "gdn_a_log": jnp.log(jax.random.uniform(ks[11], (DEPTH, H_A), f32, 1.0, 16.0)),
        "gdn_dt_bias": dt + jnp.log(-jnp.expm1(-dt)),
        "gdn_norm_w": 1.0 + nrm(12, (DEPTH, DV_A), 0.02),
        "w_out_a": nrm(13, (DEPTH, D_A, D_MODEL), D_A ** -0.5),
        "rwkv_mu": jax.random.uniform(ks[14], (DEPTH, RWKV_SHIFT), f32, 0.0, 1.0),
        "rwkv_w0": jax.random.uniform(ks[15], (DEPTH, D_B), f32, -5.0, 0.5),
        "rwkv_w2": nrm(16, (DEPTH, LORA_W, D_B), 0.1),
        "rwkv_a0": nrm(17, (DEPTH, D_B), 0.1),
        "rwkv_a2": nrm(18, (DEPTH, LORA_A, D_B), 0.1),
        "rwkv_k_k": 0.85 + nrm(19, (DEPTH, D_B), 0.02),
        "rwkv_k_a": 1.0 + nrm(20, (DEPTH, D_B), 0.02),
        "rwkv_r_k": nrm(21, (DEPTH, H_B, N_B), 0.1),
        "rwkv_gn_w": 1.0 + nrm(22, (DEPTH, D_B), 0.02),
        "rwkv_gn_b": nrm(23, (DEPTH, D_B), 0.01),
        "w_out_b": nrm(24, (DEPTH, D_B, D_MODEL), D_B ** -0.5),
        "w_out": nrm(25, (DEPTH, D_MODEL, D_MODEL), D_MODEL ** -0.5),
        "lnf_w": 1.0 + nrm(26, (D_MODEL,), 0.02),
    }


def reference(x_prompt, x_sample, state_gdn, state_gdn_conv, state_rwkv, state_shift,
              meta_tokens, ln1_w, w_in, gdn_conv_w, gdn_a_log, gdn_dt_bias, gdn_norm_w, w_out_a,
              rwkv_mu, rwkv_w0, rwkv_w2, rwkv_a0, rwkv_a2, rwkv_k_k, rwkv_k_a, rwkv_r_k,
              rwkv_gn_w, rwkv_gn_b, w_out_b, w_out, lnf_w):
    dt = x_prompt.dtype
    bp = x_prompt.shape[0]
    xp = jnp.concatenate([jnp.broadcast_to(meta_tokens.astype(dt)[None], (bp, N_META, D_MODEL)),
                          x_prompt], axis=1)
    xs = x_sample
    front_pad = (-N_META) % GDN_CHUNK
    prompt_states = []
    sample_states = []
    for l in range(DEPTH):
        lw = (ln1_w[l], w_in[l], gdn_conv_w[l], gdn_a_log[l], gdn_dt_bias[l], gdn_norm_w[l], w_out_a[l],
              rwkv_mu[l], rwkv_w0[l], rwkv_w2[l], rwkv_a0[l], rwkv_a2[l], rwkv_k_k[l], rwkv_k_a[l],
              rwkv_r_k[l], rwkv_gn_w[l], rwkv_gn_b[l], w_out_b[l], w_out[l])
        xp, st_p = hybrid_layer(xp, jnp.zeros((bp, D_MODEL), dt),
                                jnp.zeros((bp, CONV_W - 1, GDN_QKV), dt),
                                jnp.zeros((bp, H_A, DK_A, DV_A), dt),
                                jnp.zeros((bp, H_B, N_B, N_B), dt),
                                front_pad, GDN_CHUNK, *lw)
        xs, st_s = hybrid_layer(xs, state_shift[l], state_gdn_conv[l], state_gdn[l], state_rwkv[l],
                                0, xs.shape[1], *lw)
        prompt_states.append(st_p)
        sample_states.append(st_s)
    y_prompt = rmsnorm(xp[:, N_META:], lnf_w)
    y_sample = rmsnorm(xs, lnf_w)
    new_gdn_prompt = jnp.stack([s[0] for s in prompt_states])
    new_conv_prompt = jnp.stack([s[1] for s in prompt_states])
    new_rwkv_prompt = jnp.stack([s[2] for s in prompt_states])
    new_shift_prompt = jnp.stack([s[3] for s in prompt_states])
    new_gdn_sample = jnp.stack([s[0] for s in sample_states])
    new_conv_sample = jnp.stack([s[1] for s in sample_states])
    new_rwkv_sample = jnp.stack([s[2] for s in sample_states])
    new_shift_sample = jnp.stack([s[3] for s in sample_states])
    return (y_prompt, y_sample, new_gdn_prompt, new_conv_prompt, new_rwkv_prompt, new_shift_prompt,
            new_gdn_sample, new_conv_sample, new_rwkv_sample, new_shift_sample)
```

```python
import functools

import jax
import jax.numpy as jnp
from jax import lax
from jax.experimental import pallas as pl
from jax.experimental.pallas import tpu as pltpu

F32 = jnp.float32
BF16 = jnp.bfloat16

EPS = 1e-6
L2_EPS = 1e-6
GN_EPS_PER_CH = 1e-5

LANES = 128
SUBLANES = 8
VMEM_LIMIT = 48 * 1024 * 1024

NN = (((1,), (0,)), ((), ()))
NT = (((1,), (1,)), ((), ()))
TN = (((0,), (0,)), ((), ()))


def _mm(a, b, dims=NN):
    return lax.dot_general(a.astype(BF16), b.astype(BF16), dims, preferred_element_type=F32)


def _mm_f32(a, b, dims=NN):
    return lax.dot_general(a, b, dims, precision=lax.Precision.HIGHEST,
                           preferred_element_type=F32)


def _sigmoid(x):
    return 1.0 / (1.0 + jnp.exp(-x))


def _silu(x):
    return x * _sigmoid(x)


def _softplus(x):
    return jnp.maximum(x, 0.0) + jnp.log(1.0 + jnp.exp(-jnp.abs(x)))


def _shift_rows(cur, prev, k, rid):
    rolled = pltpu.roll(cur, k, 0)
    fix = pltpu.roll(prev, k, 0)
    reps = cur.shape[0] // SUBLANES
    fixfull = jnp.concatenate([fix] * reps, axis=0) if reps > 1 else fix
    return jnp.where(rid < k, fixfull, rolled)


def _tri_masks(c):
    row = lax.broadcasted_iota(jnp.int32, (c, c), 0)
    col = lax.broadcasted_iota(jnp.int32, (c, c), 1)
    diff = row ^ col
    levels = [(diff >> 1) == 0]
    sh = 2
    while (1 << sh) <= c:
        levels.append((diff >> (sh - 1)) == 1)
        sh += 1
    return row, col, levels


def _tri_inv(low, eye, levels):
    x = eye - jnp.where(levels[0], low, 0.0)
    for m in levels[1:]:
        off = jnp.where(m, low, 0.0)
        x = x - _mm_f32(_mm_f32(x, off), x)
    return x


def _proj_kernel(x_ref, lnw_ref, w_ref, o_ref, h_ref, *, apply_norm):
    @pl.when(pl.program_id(1) == 0)
    def _():
        x = x_ref[...]
        if apply_norm:
            x = x * lax.rsqrt(jnp.mean(x * x, axis=-1, keepdims=True) + EPS)
            x = x * lnw_ref[...]
        h_ref[...] = x.astype(BF16)

    o_ref[...] = jnp.dot(h_ref[...], w_ref[...], preferred_element_type=F32)


def _proj(x, lnw, w, *, apply_norm):
    m, d = x.shape
    n = w.shape[1]
    tm = min(m, 1024)
    tn = 768
    assert m % tm == 0 and n % tn == 0
    return pl.pallas_call(
        functools.partial(_proj_kernel, apply_norm=apply_norm),
        out_shape=jax.ShapeDtypeStruct((m, n), F32),
        grid=(m // tm, n // tn),
        in_specs=[pl.BlockSpec((tm, d), lambda i, j: (i, 0)),
                  pl.BlockSpec((1, d), lambda i, j: (0, 0)),
                  pl.BlockSpec((d, tn), lambda i, j: (0, j))],
        out_specs=pl.BlockSpec((tm, tn), lambda i, j: (i, j)),
        scratch_shapes=[pltpu.VMEM((tm, d), BF16)],
        compiler_params=pltpu.CompilerParams(
            dimension_semantics=("parallel", "arbitrary"), vmem_limit_bytes=VMEM_LIMIT),
        name="proj",
    )(x, lnw, w)


def _rmsnorm_kernel(x_ref, w_ref, o_ref):
    x = x_ref[...]
    o_ref[...] = x * lax.rsqrt(jnp.mean(x * x, axis=-1, keepdims=True) + EPS) * w_ref[...]


def _rmsnorm_rows(x, w):
    return pl.pallas_call(
        _rmsnorm_kernel,
        out_shape=jax.ShapeDtypeStruct(x.shape, F32),
        name="rmsnorm_rows",
    )(x, w)


def _gdn_kernel(qkv_ref, z_ref, ab_ref, abt_ref, s0_ref, prev0_ref, convw_ref,
                alr_ref, dtr_ref, alc_ref, dtc_ref, nw_ref,
                o_ref, s_ref, prev_ref, *, c, nh, dk, dv, t_valid, t_total):
    ci = pl.program_id(1)

    @pl.when(ci == 0)
    def _():
        s_ref[...] = s0_ref[...]
        prev_ref[...] = prev0_ref[0]

    row, col, levels = _tri_masks(c)
    tril = row >= col
    strict = row > col
    eye = (row == col).astype(F32)
    kw = convw_ref.shape[0]
    rid = lax.broadcasted_iota(jnp.int32, (c, dk), 0)

    ab = ab_ref[...]
    g_c = -jnp.exp(alr_ref[...]) * _softplus(ab + dtr_ref[...])
    beta_c = _sigmoid(ab)
    abt = abt_ref[0, 0]
    g_r = -jnp.exp(alc_ref[...]) * _softplus(abt + dtc_ref[...])
    if t_valid < t_total:
        tok_c = ci * c + lax.broadcasted_iota(jnp.int32, ab.shape, 0)
        tok_r = ci * c + lax.broadcasted_iota(jnp.int32, abt.shape, 1)
        g_c = jnp.where(tok_c < t_valid, g_c, 0.0)
        beta_c = jnp.where(tok_c < t_valid, beta_c, 0.0)
        g_r = jnp.where(tok_r < t_valid, g_r, 0.0)
    gc_c = _mm_f32(tril.astype(F32), g_c)
    gc_r = _mm_f32(g_r, (row <= col).astype(F32))

    def conv(col0, width):
        cur = qkv_ref[:, col0:col0 + width]
        prev = prev_ref[:, col0:col0 + width]
        acc = None
        for k in range(kw - 1, -1, -1):
            src = cur if k == 0 else _shift_rows(cur, prev, k, rid)
            term = src * convw_ref[kw - 1 - k:kw - k, col0:col0 + width]
            acc = term if acc is None else acc + term
        return _silu(acc)

    for h in range(nh):
        gcc = gc_c[:, h:h + 1]
        gcr = gc_r[h:h + 1, :]
        beta = beta_c[:, nh + h:nh + h + 1]
        glast = gc_c[c - 1:c, h:h + 1]
        decay = jnp.where(tril, jnp.exp(jnp.where(tril, gcc - gcr, 0.0)), 0.0)
        q = conv(h * dk, dk)
        k = conv(nh * dk + h * dk, dk)
        v = conv(2 * nh * dk + h * dv, dv)
        q = q * lax.rsqrt(jnp.sum(q * q, axis=-1, keepdims=True) + L2_EPS) * (dk ** -0.5)
        k = k * lax.rsqrt(jnp.sum(k * k, axis=-1, keepdims=True) + L2_EPS)
        kb = k * beta
        vb = v * beta
        lower = jnp.where(strict, _mm(kb, k, NT) * decay, 0.0)
        tinv = _tri_inv(lower, eye, levels)
        u = _mm(tinv, vb)
        w = _mm(tinv, kb * jnp.exp(gcc))
        qk = jnp.where(tril, _mm(q, k, NT) * decay, 0.0)
        s = s_ref[0, h]
        v_new = u - _mm(w, s)
        o = _mm(q * jnp.exp(gcc), s) + _mm(qk, v_new)
        s_ref[0, h] = s * jnp.exp(glast) + _mm(k * jnp.exp(glast - gcc), v_new, TN)
        o = o * lax.rsqrt(jnp.mean(o * o, axis=-1, keepdims=True) + EPS) * nw_ref[...]
        o = o * _silu(z_ref[:, h * dv:(h + 1) * dv])
        o_ref[:, h * dv:(h + 1) * dv] = o.astype(BF16)

    prev_ref[...] = qkv_ref[c - SUBLANES:c, :]


def _gdn(p, abt, s0, prev0, convw, alr, dtr, alc, dtc, nw, *, nb, t, c, t_valid, lay):
    nh, dk, dv = s0.shape[1:]
    nch = t // c
    wq = lay["qkv"][1]
    shared = s0.shape[0] == 1
    bidx = (lambda b, i: (0, 0, 0, 0)) if shared else (lambda b, i: (b, 0, 0, 0))
    pidx = (lambda b, i: (0, 0, 0)) if shared else (lambda b, i: (b, 0, 0))
    const2 = lambda b, i: (0, 0)
    return pl.pallas_call(
        functools.partial(_gdn_kernel, c=c, nh=nh, dk=dk, dv=dv, t_valid=t_valid, t_total=t),
        out_shape=(jax.ShapeDtypeStruct((nb * t, nh * dv), BF16),
                   jax.ShapeDtypeStruct((nb, nh, dk, dv), F32)),
        grid=(nb, nch),
        in_specs=[
            pl.BlockSpec((c, wq), lambda b, i: (b * nch + i, lay["qkv"][0] // wq)),
            pl.BlockSpec((c, nh * dv), lambda b, i: (b * nch + i, lay["za"][0] // (nh * dv))),
            pl.BlockSpec((c, LANES), lambda b, i: (b * nch + i, lay["ab"][0] // LANES)),
            pl.BlockSpec((1, 1, 2 * nh, c), lambda b, i: (b, i, 0, 0)),
            pl.BlockSpec((1, nh, dk, dv), bidx),
            pl.BlockSpec((1, SUBLANES, wq), pidx),
            pl.BlockSpec(convw.shape, const2),
            pl.BlockSpec(alr.shape, const2),
            pl.BlockSpec(dtr.shape, const2),
            pl.BlockSpec(alc.shape, const2),
            pl.BlockSpec(dtc.shape, const2),
            pl.BlockSpec(nw.shape, const2),
        ],
        out_specs=(pl.BlockSpec((c, nh * dv), lambda b, i: (b * nch + i, 0)),
                   pl.BlockSpec((1, nh, dk, dv), lambda b, i: (b, 0, 0, 0))),
        scratch_shapes=[pltpu.VMEM((SUBLANES, wq), F32)],
        compiler_params=pltpu.CompilerParams(
            dimension_semantics=("parallel", "arbitrary"), vmem_limit_bytes=VMEM_LIMIT),
        name="gdn",
    )(p, p, p, abt, s0, prev0, convw, alr, dtr, alc, dtc, nw)


def _rwkv_kernel(x_ref, wa_ref, s0_ref, prev0_ref, prevwa0_ref, mu_ref, muwa_ref,
                 w0_ref, w2_ref, a0_ref, a2_ref, kk_ref, ka_ref, rk_ref, gnw_ref, gnb_ref,
                 o_ref, s_ref, prev_ref, prevwa_ref, *, c, nh, n, t_valid, t_total):
    ci = pl.program_id(1)
    d = nh * n

    @pl.when(ci == 0)
    def _():
        s_ref[...] = s0_ref[...]
        prev_ref[...] = prev0_ref[0]
        prevwa_ref[...] = prevwa0_ref[0]

    row, col, levels = _tri_masks(c)
    tril = row >= col
    strict = row > col
    eye = (row == col).astype(F32)
    rid = lax.broadcasted_iota(jnp.int32, (c, d), 0)
    rid_wa = lax.broadcasted_iota(jnp.int32, (c, LANES), 0)

    def mix(j):
        cur = x_ref[:, j * d:(j + 1) * d]
        prv = _shift_rows(cur, prev_ref[:, j * d:(j + 1) * d], 1, rid)
        return cur + mu_ref[:, j * d:(j + 1) * d] * (prv - cur)

    r = mix(0)
    kb = mix(1)
    vb = mix(2)
    zb = mix(3)
    wa = wa_ref[...]
    wa = wa + muwa_ref[...] * (_shift_rows(wa, prevwa_ref[...], 1, rid_wa) - wa)
    lora = w2_ref.shape[0]
    w_raw = w0_ref[...] + _mm(jnp.tanh(wa[:, 0:lora]), w2_ref[...])
    w_log = -jnp.exp(-_softplus(-w_raw) - 0.5)
    a = _sigmoid(a0_ref[...] + _mm(wa[:, lora:], a2_ref[...]))
    kku = kb * kk_ref[...]
    k2 = kb * (1.0 + (a - 1.0) * ka_ref[...])
    if t_valid < t_total:
        ok = (ci * c + rid) < t_valid
        w_log = jnp.where(ok, w_log, 0.0)
        kku = jnp.where(ok, kku, 0.0)
        vb = jnp.where(ok, vb, 0.0)

    w_hi = w_log.astype(BF16)
    rem = w_log - w_hi.astype(F32)
    w_mid = rem.astype(BF16)
    w_lo = (rem - w_mid.astype(F32)).astype(BF16)
    tril_b = tril.astype(BF16)
    gcum = (jnp.dot(tril_b, w_hi, preferred_element_type=F32)
            + jnp.dot(tril_b, w_mid, preferred_element_type=F32)
            + jnp.dot(tril_b, w_lo, preferred_element_type=F32))
    glast = gcum[c - 1:c, :]
    g_in = jnp.exp(gcum)
    g_ex = jnp.exp(gcum - w_log)
    g_inv = jnp.exp(-gcum)
    g_tail = jnp.exp(glast - gcum)
    g_all = jnp.exp(glast)

    outs = []
    for h in range(nh):
        sl = slice(h * n, (h + 1) * n)
        kk = kku[:, sl]
        kk = kk * lax.rsqrt(jnp.sum(kk * kk, axis=-1, keepdims=True) + L2_EPS)
        b = kk * a[:, sl]
        r_h = r[:, sl]
        k_h = k2[:, sl]
        v_h = vb[:, sl]
        lhs = jnp.concatenate([kk * g_ex[:, sl], r_h * g_in[:, sl]], axis=0)
        rhs = jnp.concatenate([b * g_inv[:, sl], k_h * g_inv[:, sl]], axis=0)
        big = _mm(lhs, rhs, NT)
        low = jnp.where(strict, big[0:c, 0:c], 0.0)
        m_kv = jnp.where(strict, big[0:c, c:2 * c], 0.0)
        q_b = jnp.where(tril, big[c:2 * c, 0:c], 0.0)
        p_kv = jnp.where(tril, big[c:2 * c, c:2 * c], 0.0)
        s = s_ref[0, h]
        from_s = _mm(lhs, s, NT)
        tinv = _tri_inv(low, eye, levels)
        sa = _mm(tinv, from_s[0:c] + _mm(m_kv, v_h))
        y = from_s[c:2 * c] + _mm(p_kv, v_h) - _mm(q_b, sa)
        s_ref[0, h] = (s * g_all[:, sl] + _mm(v_h, k_h * g_tail[:, sl], TN)
                       - _mm(sa, b * g_tail[:, sl], TN))
        mean = jnp.mean(y, axis=-1, keepdims=True)
        var = jnp.mean(jnp.square(y - mean), axis=-1, keepdims=True)
        yn = (y - mean) * lax.rsqrt(var + n * GN_EPS_PER_CH)
        yn = yn * gnw_ref[:, sl] + gnb_ref[:, sl]
        bonus = jnp.sum(r_h * k_h * rk_ref[:, sl], axis=-1, keepdims=True) * v_h
        outs.append(yn + bonus)
    o = jnp.concatenate(outs, axis=-1) * _silu(zb)
    o_ref[...] = o.astype(BF16)

    prev_ref[...] = x_ref[c - SUBLANES:c, :]
    prevwa_ref[...] = wa_ref[c - SUBLANES:c, :]


def _rwkv(p, s0, prev0, prevwa0, mu, muwa, w0, w2, a0, a2, kk, ka, rk, gnw, gnb,
          *, nb, t, c, t_valid, lay):
    nh, n = s0.shape[1:3]
    d = nh * n
    nch = t // c
    wx = lay["rkvz"][1]
    shared = s0.shape[0] == 1
    bidx = (lambda b, i: (0, 0, 0, 0)) if shared else (lambda b, i: (b, 0, 0, 0))
    pidx = (lambda b, i: (0, 0, 0)) if shared else (lambda b, i: (b, 0, 0))
    const2 = lambda b, i: (0, 0)
    vec = pl.BlockSpec((1, d), const2)
    return pl.pallas_call(
        functools.partial(_rwkv_kernel, c=c, nh=nh, n=n, t_valid=t_valid, t_total=t),
        out_shape=(jax.ShapeDtypeStruct((nb * t, d), BF16),
                   jax.ShapeDtypeStruct((nb, nh, n, n), F32)),
        grid=(nb, nch),
        in_specs=[
            pl.BlockSpec((c, wx), lambda b, i: (b * nch + i, lay["rkvz"][0] // wx)),
            pl.BlockSpec((c, LANES), lambda b, i: (b * nch + i, lay["wa"][0] // LANES)),
            pl.BlockSpec((1, nh, n, n), bidx),
            pl.BlockSpec((1, SUBLANES, wx), pidx),
            pl.BlockSpec((1, SUBLANES, LANES), pidx),
            pl.BlockSpec((1, wx), const2),
            pl.BlockSpec((1, LANES), const2),
            vec,
            pl.BlockSpec(w2.shape, const2),
            vec,
            pl.BlockSpec(a2.shape, const2),
            vec, vec, vec, vec, vec,
        ],
        out_specs=(pl.BlockSpec((c, d), lambda b, i: (b * nch + i, 0)),
                   pl.BlockSpec((1, nh, n, n), lambda b, i: (b, 0, 0, 0))),
        scratch_shapes=[pltpu.VMEM((SUBLANES, wx), F32), pltpu.VMEM((SUBLANES, LANES), F32)],
        compiler_params=pltpu.CompilerParams(
            dimension_semantics=("parallel", "arbitrary"), vmem_limit_bytes=VMEM_LIMIT),
        name="rwkv",
    )(p, p, s0, prev0, prevwa0, mu, muwa, w0, w2, a0, a2, kk, ka, rk, gnw, gnb)


def _merge_kernel(oa_ref, ob_ref, gate_ref, x_ref, woa_ref, wob_ref, wo_ref, lnf_ref, y_ref):
    d = x_ref.shape[1]
    ba = jnp.dot(oa_ref[...], woa_ref[...], preferred_element_type=F32)
    bb = jnp.dot(ob_ref[...], wob_ref[...], preferred_element_type=F32)
    gates = _sigmoid(gate_ref[...])
    merged = gates[:, :d] * ba + gates[:, d:] * bb
    xn = x_ref[...] + jnp.dot(merged.astype(BF16), wo_ref[...], preferred_element_type=F32)
    y_ref[...] = xn * lax.rsqrt(jnp.mean(xn * xn, axis=-1, keepdims=True) + EPS) * lnf_ref[...]


def _merge(oa, ob, p, x, woa, wob, wo, lnf, *, lay):
    m, d = x.shape
    tm = min(m, 512)
    wg = lay["gate"][1]
    const2 = lambda i: (0, 0)
    return pl.pallas_call(
        _merge_kernel,
        out_shape=jax.ShapeDtypeStruct((m, d), F32),
        grid=(m // tm,),
        in_specs=[
            pl.BlockSpec((tm, oa.shape[1]), lambda i: (i, 0)),
            pl.BlockSpec((tm, ob.shape[1]), lambda i: (i, 0)),
            pl.BlockSpec((tm, wg), lambda i: (i, lay["gate"][0] // wg)),
            pl.BlockSpec((tm, d), lambda i: (i, 0)),
            pl.BlockSpec(woa.shape, const2),
            pl.BlockSpec(wob.shape, const2),
            pl.BlockSpec(wo.shape, const2),
            pl.BlockSpec((1, d), const2),
        ],
        out_specs=pl.BlockSpec((tm, d), lambda i: (i, 0)),
        compiler_params=pltpu.CompilerParams(
            dimension_semantics=("parallel",), vmem_limit_bytes=VMEM_LIMIT),
        name="merge",
    )(oa, ob, p, x, woa, wob, wo, lnf)


def _pad_lanes(v, width):
    return jnp.pad(v, ((0, 0), (0, width - v.shape[1])))


def _tail_rows(rows):
    return jnp.pad(rows, ((0, 0), (SUBLANES - rows.shape[1], 0), (0, 0)))


def _token_major_t(ab, nb, nch, c):
    return jnp.transpose(ab.reshape(nb, nch, c, ab.shape[1]), (0, 1, 3, 2))


def kernel(x_prompt, x_sample, state_gdn, state_gdn_conv, state_rwkv, state_shift, meta_tokens,
           ln1_w, w_in, gdn_conv_w, gdn_a_log, gdn_dt_bias, gdn_norm_w, w_out_a, rwkv_mu, rwkv_w0,
           rwkv_w2, rwkv_a0, rwkv_a2, rwkv_k_k, rwkv_k_a, rwkv_r_k, rwkv_gn_w, rwkv_gn_b, w_out_b,
           w_out, lnf_w):
    assert ln1_w.shape[0] == 1, "single-layer trunk"
    bp, seq, d = x_prompt.shape
    bs, tseq, _ = x_sample.shape
    n_meta = meta_tokens.shape[0]
    _, _, nh_a, dk, dv = state_gdn.shape
    _, _, nh_b, n_b, _ = state_rwkv.shape
    kw, w_qkv = gdn_conv_w.shape[1:]
    lora_w = rwkv_w2.shape[1]
    lora_a = rwkv_a2.shape[1]
    d_a = nh_a * dv
    d_b = nh_b * n_b
    assert w_qkv == 2 * nh_a * dk + d_a and d_a == d and d_b == d and lora_w + lora_a == LANES
    assert n_meta % SUBLANES == 0 and n_meta % 16 == 0 and tseq >= kw - 1

    o_a = w_qkv
    o_b = o_a + nh_a
    o_z = o_b + nh_a
    o_r = o_z + d_a
    o_g = o_r + 3 * d_b + lora_w + lora_a + d_b
    w = w_in[0]
    wr = w[:, o_r:o_g]
    mu = rwkv_mu
    lora0 = 3 * d_b
    cols = [w[:, :w_qkv], w[:, o_z:o_r], wr[:, :lora0], wr[:, lora0 + LANES:], w[:, o_g:],
            wr[:, lora0:lora0 + LANES], w[:, o_a:o_z]]
    used = sum(cw.shape[1] for cw in cols)
    tn = 768
    n_pad = -(-(used + LANES - 2 * nh_a) // tn) * tn
    cols.append(jnp.zeros((d, n_pad - used), F32))
    w_all = jnp.concatenate(cols, axis=1).astype(BF16)
    lay = {"qkv": (0, w_qkv), "za": (w_qkv, d_a), "rkvz": (w_qkv + d_a, 4 * d_b),
           "gate": (w_qkv + d_a + 4 * d_b, 2 * d)}
    lay["wa"] = (lay["gate"][0] + 2 * d, LANES)
    lay["ab"] = (lay["wa"][0] + LANES, LANES)
    for off, width in lay.values():
        assert off % width == 0
    mu_x = jnp.concatenate([mu[:, :lora0], mu[:, lora0 + LANES:]], axis=1)
    mu_wa = mu[:, lora0:lora0 + LANES]

    alr = _pad_lanes(gdn_a_log, LANES)
    dtr = _pad_lanes(gdn_dt_bias, LANES)
    alc = jnp.pad(gdn_a_log.reshape(nh_a, 1), ((0, nh_a), (0, 0)))
    dtc = jnp.pad(gdn_dt_bias.reshape(nh_a, 1), ((0, nh_a), (0, 0)))
    convw = gdn_conv_w[0]
    rk = rwkv_r_k.reshape(1, d_b)
    woa = w_out_a[0].astype(BF16)
    wob = w_out_b[0].astype(BF16)
    wo = w_out[0].astype(BF16)
    lnf = lnf_w.reshape(1, d)

    def branches(p, nb, t, c, t_valid, s_gdn, conv_tail, s_rwkv, x_tail, wa_tail):
        nch = t // c
        abt = _token_major_t(p[:, lay["ab"][0]:lay["ab"][0] + 2 * nh_a], nb, nch, c)
        oa, sg = _gdn(p, abt, s_gdn, conv_tail, convw, alr, dtr, alc, dtc, gdn_norm_w,
                      nb=nb, t=t, c=c, t_valid=t_valid, lay=lay)
        ob, sr = _rwkv(p, s_rwkv, x_tail, wa_tail, mu_x, mu_wa, rwkv_w0, rwkv_w2[0], rwkv_a0,
                       rwkv_a2[0], rwkv_k_k, rwkv_k_a, rk, rwkv_gn_w, rwkv_gn_b,
                       nb=nb, t=t, c=c, t_valid=t_valid, lay=lay)
        return oa, sg, ob, sr

    x0, wx = lay["rkvz"]
    a0_, _ = lay["wa"]

    p_m = _proj(meta_tokens, ln1_w, w_all, apply_norm=True)
    _, sg_m, _, sr_m = branches(
        p_m, 1, n_meta, n_meta, n_meta,
        jnp.zeros((1, nh_a, dk, dv), F32), jnp.zeros((1, SUBLANES, w_qkv), F32),
        jnp.zeros((1, nh_b, n_b, n_b), F32), jnp.zeros((1, SUBLANES, wx), F32),
        jnp.zeros((1, SUBLANES, LANES), F32))

    xp = x_prompt.reshape(bp * seq, d)
    p_p = _proj(xp, ln1_w, w_all, apply_norm=True)
    tail_m = p_m[n_meta - SUBLANES:]
    oa_p, sg_p, ob_p, sr_p = branches(
        p_p, bp, seq, 64, seq, sg_m, tail_m[None, :, :w_qkv], sr_m,
        tail_m[None, :, x0:x0 + wx], tail_m[None, :, a0_:a0_ + LANES])
    y_p = _merge(oa_p, ob_p, p_p, xp, woa, wob, wo, lnf, lay=lay)

    tpad = 16
    xs = jnp.pad(x_sample, ((0, 0), (0, tpad - tseq), (0, 0))).reshape(bs * tpad, d)
    p_s = _proj(xs, ln1_w, w_all, apply_norm=True)
    p_first = _proj(state_shift[0], ln1_w, w_all, apply_norm=False)
    oa_s, sg_s, ob_s, sr_s = branches(
        p_s, bs, tpad, tpad, tseq, state_gdn[0], _tail_rows(state_gdn_conv[0]), state_rwkv[0],
        _tail_rows(p_first[:, None, x0:x0 + wx]), _tail_rows(p_first[:, None, a0_:a0_ + LANES]))
    y_s = _merge(oa_s, ob_s, p_s, xs, woa, wob, wo, lnf, lay=lay)

    shift_p = _rmsnorm_rows(x_prompt[:, -1], ln1_w)
    shift_s = _rmsnorm_rows(x_sample[:, -1], ln1_w)
    conv_p = p_p.reshape(bp, seq, n_pad)[:, seq - (kw - 1):, :w_qkv]
    conv_s = p_s.reshape(bs, tpad, n_pad)[:, tseq - (kw - 1):tseq, :w_qkv]
    return (y_p.reshape(bp, seq, d), y_s.reshape(bs, tpad, d)[:, :tseq],
            sg_p[None], conv_p[None], sr_p[None], shift_p[None],
            sg_s[None], conv_s[None], sr_s[None], shift_s[None])
```

```python
import functools

import jax
import jax.numpy as jnp
from jax import lax
from jax.experimental import pallas as pl
from jax.experimental.pallas import tpu as pltpu

F32 = jnp.float32
BF16 = jnp.bfloat16

EPS = 1e-6
L2_EPS = 1e-6
GN_EPS_PER_CH = 1e-5

LANES = 128
SUBLANES = 8
VMEM_LIMIT = 48 * 1024 * 1024

NN = (((1,), (0,)), ((), ()))
NT = (((1,), (1,)), ((), ()))
TN = (((0,), (0,)), ((), ()))


def _mm(a, b, dims=NN):
    return lax.dot_general(a.astype(BF16), b.astype(BF16), dims, preferred_element_type=F32)


def _mm_f32(a, b, dims=NN):
    return lax.dot_general(a, b, dims, precision=lax.Precision.HIGHEST,
                           preferred_element_type=F32)


def _sigmoid(x):
    return 1.0 / (1.0 + jnp.exp(-x))


def _silu(x):
    return x * _sigmoid(x)


def _softplus(x):
    return jnp.maximum(x, 0.0) + jnp.log(1.0 + jnp.exp(-jnp.abs(x)))


def _shift_rows(cur, prev, k, rid):
    rolled = pltpu.roll(cur, k, 0)
    fix = pltpu.roll(prev, k, 0)
    reps = cur.shape[0] // SUBLANES
    fixfull = jnp.concatenate([fix] * reps, axis=0) if reps > 1 else fix
    return jnp.where(rid < k, fixfull, rolled)


def _tri_masks(c):
    row = lax.broadcasted_iota(jnp.int32, (c, c), 0)
    col = lax.broadcasted_iota(jnp.int32, (c, c), 1)
    diff = row ^ col
    levels = [(diff >> 1) == 0]
    sh = 2
    while (1 << sh) <= c:
        levels.append((diff >> (sh - 1)) == 1)
        sh += 1
    return row, col, levels


def _tri_inv_many(lows, levels):
    nn = [-jnp.where(levels[0], low, 0.0) for low in lows]
    for m in levels[1:]:
        off = [jnp.where(m, low, 0.0) for low in lows]
        xc = [o + _mm(x, o) for x, o in zip(nn, off)]
        nn = [x - (y + _mm(y, x)) for x, y in zip(nn, xc)]
    return nn


def _proj_kernel(x_ref, lnw_ref, w_ref, o_ref, h_ref, *, apply_norm):
    @pl.when(pl.program_id(1) == 0)
    def _():
        x = x_ref[...]
        if apply_norm:
            x = x * lax.rsqrt(jnp.mean(x * x, axis=-1, keepdims=True) + EPS)
            x = x * lnw_ref[...]
        h_ref[...] = x.astype(BF16)

    o_ref[...] = jnp.dot(h_ref[...], w_ref[...], preferred_element_type=F32)


def _proj(x, lnw, w, *, apply_norm):
    m, d = x.shape
    n = w.shape[1]
    tm = min(m, 1024)
    tn = 768
    assert m % tm == 0 and n % tn == 0
    return pl.pallas_call(
        functools.partial(_proj_kernel, apply_norm=apply_norm),
        out_shape=jax.ShapeDtypeStruct((m, n), F32),
        grid=(m // tm, n // tn),
        in_specs=[pl.BlockSpec((tm, d), lambda i, j: (i, 0)),
                  pl.BlockSpec((1, d), lambda i, j: (0, 0)),
                  pl.BlockSpec((d, tn), lambda i, j: (0, j))],
        out_specs=pl.BlockSpec((tm, tn), lambda i, j: (i, j)),
        scratch_shapes=[pltpu.VMEM((tm, d), BF16)],
        compiler_params=pltpu.CompilerParams(
            dimension_semantics=("parallel", "arbitrary"), vmem_limit_bytes=VMEM_LIMIT),
        name="proj",
    )(x, lnw, w)


def _rmsnorm_kernel(x_ref, w_ref, o_ref):
    x = x_ref[...]
    o_ref[...] = x * lax.rsqrt(jnp.mean(x * x, axis=-1, keepdims=True) + EPS) * w_ref[...]


def _rmsnorm_rows(x, w):
    return pl.pallas_call(
        _rmsnorm_kernel,
        out_shape=jax.ShapeDtypeStruct(x.shape, F32),
        name="rmsnorm_rows",
    )(x, w)


def _gdn_kernel(qkv_ref, z_ref, ab_ref, abt_ref, s0_ref, prev0_ref, convw_ref,
                alr_ref, dtr_ref, alc_ref, dtc_ref, nw_ref,
                o_ref, s_ref, prev_ref, *, c, nh, dk, dv, t_valid, t_total):
    ci = pl.program_id(1)

    @pl.when(ci == 0)
    def _():
        s_ref[...] = s0_ref[...]
        prev_ref[...] = prev0_ref[0]

    row, col, levels = _tri_masks(c)
    tril = row >= col
    strict = row > col
    kw = convw_ref.shape[0]
    rid = lax.broadcasted_iota(jnp.int32, (c, dk), 0)

    ab = ab_ref[...]
    g_c = -jnp.exp(alr_ref[...]) * _softplus(ab + dtr_ref[...])
    beta_c = _sigmoid(ab)
    abt = abt_ref[0, 0]
    g_r = -jnp.exp(alc_ref[...]) * _softplus(abt + dtc_ref[...])
    if t_valid < t_total:
        tok_c = ci * c + lax.broadcasted_iota(jnp.int32, ab.shape, 0)
        tok_r = ci * c + lax.broadcasted_iota(jnp.int32, abt.shape, 1)
        g_c = jnp.where(tok_c < t_valid, g_c, 0.0)
        beta_c = jnp.where(tok_c < t_valid, beta_c, 0.0)
        g_r = jnp.where(tok_r < t_valid, g_r, 0.0)
    gc_c = _mm_f32(tril.astype(F32), g_c)
    gc_r = _mm_f32(g_r, (row <= col).astype(F32))

    def conv(col0, width):
        cur = qkv_ref[:, col0:col0 + width]
        prev = prev_ref[:, col0:col0 + width]
        acc = None
        for k in range(kw - 1, -1, -1):
            src = cur if k == 0 else _shift_rows(cur, prev, k, rid)
            term = src * convw_ref[kw - 1 - k:kw - k, col0:col0 + width]
            acc = term if acc is None else acc + term
        return _silu(acc)

    hs = range(nh)
    gcc = [gc_c[:, h:h + 1] for h in hs]
    beta = [beta_c[:, nh + h:nh + h + 1] for h in hs]
    glast = [gc_c[c - 1:c, h:h + 1] for h in hs]
    decay = [jnp.where(tril, jnp.exp(jnp.where(tril, gcc[h] - gc_r[h:h + 1, :], 0.0)), 0.0)
             for h in hs]
    q = [conv(h * dk, dk) for h in hs]
    k = [conv(nh * dk + h * dk, dk) for h in hs]
    v = [conv(2 * nh * dk + h * dv, dv) for h in hs]
    q = [x * lax.rsqrt(jnp.sum(x * x, axis=-1, keepdims=True) + L2_EPS) * (dk ** -0.5) for x in q]
    k = [x * lax.rsqrt(jnp.sum(x * x, axis=-1, keepdims=True) + L2_EPS) for x in k]
    kb = [k[h] * beta[h] for h in hs]
    vb = [v[h] * beta[h] for h in hs]
    kbg = [kb[h] * jnp.exp(gcc[h]) for h in hs]
    qg = [q[h] * jnp.exp(gcc[h]) for h in hs]
    k_tail = [k[h] * jnp.exp(glast[h] - gcc[h]) for h in hs]
    lower = [jnp.where(strict, _mm(kb[h], k[h], NT) * decay[h], 0.0) for h in hs]
    qk = [jnp.where(tril, _mm(q[h], k[h], NT) * decay[h], 0.0) for h in hs]
    s_old = [s_ref[0, h] for h in hs]
    qs = [_mm(qg[h], s_old[h]) for h in hs]
    nn = _tri_inv_many(lower, levels)
    u = [vb[h] + _mm(nn[h], vb[h]) for h in hs]
    w = [kbg[h] + _mm(nn[h], kbg[h]) for h in hs]
    v_new = [u[h] - _mm(w[h], s_old[h]) for h in hs]
    kv = [_mm(k_tail[h], v_new[h], TN) for h in hs]
    for h in hs:
        s_ref[0, h] = s_old[h] * jnp.exp(glast[h]) + kv[h]
    o = [qs[h] + _mm(qk[h], v_new[h]) for h in hs]
    for h in hs:
        oh = o[h] * lax.rsqrt(jnp.mean(o[h] * o[h], axis=-1, keepdims=True) + EPS) * nw_ref[...]
        oh = oh * _silu(z_ref[:, h * dv:(h + 1) * dv])
        o_ref[:, h * dv:(h + 1) * dv] = oh.astype(BF16)

    prev_ref[...] = qkv_ref[c - SUBLANES:c, :]


def _gdn(p, abt, s0, prev0, convw, alr, dtr, alc, dtc, nw, *, nb, t, c, t_valid, lay):
    nh, dk, dv = s0.shape[1:]
    nch = t // c
    wq = lay["qkv"][1]
    shared = s0.shape[0] == 1
    bidx = (lambda b, i: (0, 0, 0, 0)) if shared else (lambda b, i: (b, 0, 0, 0))
    pidx = (lambda b, i: (0, 0, 0)) if shared else (lambda b, i: (b, 0, 0))
    const2 = lambda b, i: (0, 0)
    return pl.pallas_call(
        functools.partial(_gdn_kernel, c=c, nh=nh, dk=dk, dv=dv, t_valid=t_valid, t_total=t),
        out_shape=(jax.ShapeDtypeStruct((nb * t, nh * dv), BF16),
                   jax.ShapeDtypeStruct((nb, nh, dk, dv), F32)),
        grid=(nb, nch),
        in_specs=[
            pl.BlockSpec((c, wq), lambda b, i: (b * nch + i, lay["qkv"][0] // wq)),
            pl.BlockSpec((c, nh * dv), lambda b, i: (b * nch + i, lay["za"][0] // (nh * dv))),
            pl.BlockSpec((c, LANES), lambda b, i: (b * nch + i, lay["ab"][0] // LANES)),
            pl.BlockSpec((1, 1, 2 * nh, c), lambda b, i: (b, i, 0, 0)),
            pl.BlockSpec((1, nh, dk, dv), bidx),
            pl.BlockSpec((1, SUBLANES, wq), pidx),
            pl.BlockSpec(convw.shape, const2),
            pl.BlockSpec(alr.shape, const2),
            pl.BlockSpec(dtr.shape, const2),
            pl.BlockSpec(alc.shape, const2),
            pl.BlockSpec(dtc.shape, const2),
            pl.BlockSpec(nw.shape, const2),
        ],
        out_specs=(pl.BlockSpec((c, nh * dv), lambda b, i: (b * nch + i, 0)),
                   pl.BlockSpec((1, nh, dk, dv), lambda b, i: (b, 0, 0, 0))),
        scratch_shapes=[pltpu.VMEM((SUBLANES, wq), F32)],
        compiler_params=pltpu.CompilerParams(
            dimension_semantics=("parallel", "arbitrary"), vmem_limit_bytes=VMEM_LIMIT),
        name="gdn",
    )(p, p, p, abt, s0, prev0, convw, alr, dtr, alc, dtc, nw)


def _rwkv_kernel(x_ref, wa_ref, s0_ref, prev0_ref, prevwa0_ref, mu_ref, muwa_ref,
                 w0_ref, w2_ref, a0_ref, a2_ref, kk_ref, ka_ref, rk_ref, gnw_ref, gnb_ref,
                 o_ref, s_ref, prev_ref, prevwa_ref, *, c, nh, n, t_valid, t_total):
    ci = pl.program_id(1)
    d = nh * n

    @pl.when(ci == 0)
    def _():
        s_ref[...] = s0_ref[...]
        prev_ref[...] = prev0_ref[0]
        prevwa_ref[...] = prevwa0_ref[0]

    row, col, levels = _tri_masks(c)
    tril = row >= col
    strict = row > col
    rid = lax.broadcasted_iota(jnp.int32, (c, d), 0)
    rid_wa = lax.broadcasted_iota(jnp.int32, (c, LANES), 0)

    def mix(j):
        cur = x_ref[:, j * d:(j + 1) * d]
        prv = _shift_rows(cur, prev_ref[:, j * d:(j + 1) * d], 1, rid)
        return cur + mu_ref[:, j * d:(j + 1) * d] * (prv - cur)

    r = mix(0)
    kb = mix(1)
    vb = mix(2)
    zb = mix(3)
    wa = wa_ref[...]
    wa = wa + muwa_ref[...] * (_shift_rows(wa, prevwa_ref[...], 1, rid_wa) - wa)
    lora = w2_ref.shape[0]
    w_raw = w0_ref[...] + _mm(jnp.tanh(wa[:, 0:lora]), w2_ref[...])
    w_log = -jnp.exp(-_softplus(-w_raw) - 0.5)
    a = _sigmoid(a0_ref[...] + _mm(wa[:, lora:], a2_ref[...]))
    kku = kb * kk_ref[...]
    k2 = kb * (1.0 + (a - 1.0) * ka_ref[...])
    if t_valid < t_total:
        ok = (ci * c + rid) < t_valid
        w_log = jnp.where(ok, w_log, 0.0)
        kku = jnp.where(ok, kku, 0.0)
        vb = jnp.where(ok, vb, 0.0)

    w_hi = w_log.astype(BF16)
    rem = w_log - w_hi.astype(F32)
    w_mid = rem.astype(BF16)
    w_lo = (rem - w_mid.astype(F32)).astype(BF16)
    tril_b = tril.astype(BF16)
    gcum = (jnp.dot(tril_b, w_hi, preferred_element_type=F32)
            + jnp.dot(tril_b, w_mid, preferred_element_type=F32)
            + jnp.dot(tril_b, w_lo, preferred_element_type=F32))
    glast = gcum[c - 1:c, :]
    g_in = jnp.exp(gcum)
    g_ex = jnp.exp(gcum - w_log)
    g_inv = jnp.exp(-gcum)
    g_tail = jnp.exp(glast - gcum)
    g_all = jnp.exp(glast)

    hs = range(nh)
    sls = [slice(h * n, (h + 1) * n) for h in hs]
    kk = [kku[:, sl] for sl in sls]
    kk = [x * lax.rsqrt(jnp.sum(x * x, axis=-1, keepdims=True) + L2_EPS) for x in kk]
    b = [kk[h] * a[:, sls[h]] for h in hs]
    r_h = [r[:, sl] for sl in sls]
    k_h = [k2[:, sl] for sl in sls]
    v_h = [vb[:, sl] for sl in sls]
    lhs = [jnp.concatenate([kk[h] * g_ex[:, sls[h]], r_h[h] * g_in[:, sls[h]]], axis=0)
           for h in hs]
    rhs = [jnp.concatenate([b[h] * g_inv[:, sls[h]], k_h[h] * g_inv[:, sls[h]]], axis=0)
           for h in hs]
    k_tail = [k_h[h] * g_tail[:, sls[h]] for h in hs]
    b_tail = [b[h] * g_tail[:, sls[h]] for h in hs]
    big = [_mm(lhs[h], rhs[h], NT) for h in hs]
    s_old = [s_ref[0, h] for h in hs]
    from_s = [_mm(lhs[h], s_old[h], NT) for h in hs]
    low = [jnp.where(strict, x[0:c, 0:c], 0.0) for x in big]
    m_kv = [jnp.where(strict, x[0:c, c:2 * c], 0.0) for x in big]
    q_b = [jnp.where(tril, x[c:2 * c, 0:c], 0.0) for x in big]
    p_kv = [jnp.where(tril, x[c:2 * c, c:2 * c], 0.0) for x in big]
    mv = [_mm(m_kv[h], v_h[h]) for h in hs]
    pv = [_mm(p_kv[h], v_h[h]) for h in hs]
    nn = _tri_inv_many(low, levels)
    rhs_sa = [from_s[h][0:c] + mv[h] for h in hs]
    sa = [rhs_sa[h] + _mm(nn[h], rhs_sa[h]) for h in hs]
    qs = [_mm(q_b[h], sa[h]) for h in hs]
    vk = [_mm(v_h[h], k_tail[h], TN) for h in hs]
    sb = [_mm(sa[h], b_tail[h], TN) for h in hs]
    for h in hs:
        s_ref[0, h] = s_old[h] * g_all[:, sls[h]] + vk[h] - sb[h]
    outs = []
    for h in hs:
        y = from_s[h][c:2 * c] + pv[h] - qs[h]
        mean = jnp.mean(y, axis=-1, keepdims=True)
        var = jnp.mean(jnp.square(y - mean), axis=-1, keepdims=True)
        yn = (y - mean) * lax.rsqrt(var + n * GN_EPS_PER_CH)
        yn = yn * gnw_ref[:, sls[h]] + gnb_ref[:, sls[h]]
        bonus = jnp.sum(r_h[h] * k_h[h] * rk_ref[:, sls[h]], axis=-1, keepdims=True) * v_h[h]
        outs.append(yn + bonus)
    o = jnp.concatenate(outs, axis=-1) * _silu(zb)
    o_ref[...] = o.astype(BF16)

    prev_ref[...] = x_ref[c - SUBLANES:c, :]
    prevwa_ref[...] = wa_ref[c - SUBLANES:c, :]


def _rwkv(p, s0, prev0, prevwa0, mu, muwa, w0, w2, a0, a2, kk, ka, rk, gnw, gnb,
          *, nb, t, c, t_valid, lay):
    nh, n = s0.shape[1:3]
    d = nh * n
    nch = t // c
    wx = lay["rkvz"][1]
    shared = s0.shape[0] == 1
    bidx = (lambda b, i: (0, 0, 0, 0)) if shared else (lambda b, i: (b, 0, 0, 0))
    pidx = (lambda b, i: (0, 0, 0)) if shared else (lambda b, i: (b, 0, 0))
    const2 = lambda b, i: (0, 0)
    vec = pl.BlockSpec((1, d), const2)
    return pl.pallas_call(
        functools.partial(_rwkv_kernel, c=c, nh=nh, n=n, t_valid=t_valid, t_total=t),
        out_shape=(jax.ShapeDtypeStruct((nb * t, d), BF16),
                   jax.ShapeDtypeStruct((nb, nh, n, n), F32)),
        grid=(nb, nch),
        in_specs=[
            pl.BlockSpec((c, wx), lambda b, i: (b * nch + i, lay["rkvz"][0] // wx)),
            pl.BlockSpec((c, LANES), lambda b, i: (b * nch + i, lay["wa"][0] // LANES)),
            pl.BlockSpec((1, nh, n, n), bidx),
            pl.BlockSpec((1, SUBLANES, wx), pidx),
            pl.BlockSpec((1, SUBLANES, LANES), pidx),
            pl.BlockSpec((1, wx), const2),
            pl.BlockSpec((1, LANES), const2),
            vec,
            pl.BlockSpec(w2.shape, const2),
            vec,
            pl.BlockSpec(a2.shape, const2),
            vec, vec, vec, vec, vec,
        ],
        out_specs=(pl.BlockSpec((c, d), lambda b, i: (b * nch + i, 0)),
                   pl.BlockSpec((1, nh, n, n), lambda b, i: (b, 0, 0, 0))),
        scratch_shapes=[pltpu.VMEM((SUBLANES, wx), F32), pltpu.VMEM((SUBLANES, LANES), F32)],
        compiler_params=pltpu.CompilerParams(
            dimension_semantics=("parallel", "arbitrary"), vmem_limit_bytes=VMEM_LIMIT),
        name="rwkv",
    )(p, p, s0, prev0, prevwa0, mu, muwa, w0, w2, a0, a2, kk, ka, rk, gnw, gnb)


def _merge_kernel(oa_ref, ob_ref, gate_ref, x_ref, woa_ref, wob_ref, wo_ref, lnf_ref, y_ref):
    d = x_ref.shape[1]
    ba = jnp.dot(oa_ref[...], woa_ref[...], preferred_element_type=F32)
    bb = jnp.dot(ob_ref[...], wob_ref[...], preferred_element_type=F32)
    gates = _sigmoid(gate_ref[...])
    merged = gates[:, :d] * ba + gates[:, d:] * bb
    xn = x_ref[...] + jnp.dot(merged.astype(BF16), wo_ref[...], preferred_element_type=F32)
    y_ref[...] = xn * lax.rsqrt(jnp.mean(xn * xn, axis=-1, keepdims=True) + EPS) * lnf_ref[...]


def _merge(oa, ob, p, x, woa, wob, wo, lnf, *, lay):
    m, d = x.shape
    tm = min(m, 512)
    wg = lay["gate"][1]
    const2 = lambda i: (0, 0)
    return pl.pallas_call(
        _merge_kernel,
        out_shape=jax.ShapeDtypeStruct((m, d), F32),
        grid=(m // tm,),
        in_specs=[
            pl.BlockSpec((tm, oa.shape[1]), lambda i: (i, 0)),
            pl.BlockSpec((tm, ob.shape[1]), lambda i: (i, 0)),
            pl.BlockSpec((tm, wg), lambda i: (i, lay["gate"][0] // wg)),
            pl.BlockSpec((tm, d), lambda i: (i, 0)),
            pl.BlockSpec(woa.shape, const2),
            pl.BlockSpec(wob.shape, const2),
            pl.BlockSpec(wo.shape, const2),
            pl.BlockSpec((1, d), const2),
        ],
        out_specs=pl.BlockSpec((tm, d), lambda i: (i, 0)),
        compiler_params=pltpu.CompilerParams(
            dimension_semantics=("parallel",), vmem_limit_bytes=VMEM_LIMIT),
        name="merge",
    )(oa, ob, p, x, woa, wob, wo, lnf)


def _pad_lanes(v, width):
    return jnp.pad(v, ((0, 0), (0, width - v.shape[1])))


def _tail_rows(rows):
    return jnp.pad(rows, ((0, 0), (SUBLANES - rows.shape[1], 0), (0, 0)))


def _token_major_t(ab, nb, nch, c):
    return jnp.transpose(ab.reshape(nb, nch, c, ab.shape[1]), (0, 1, 3, 2))


def kernel(x_prompt, x_sample, state_gdn, state_gdn_conv, state_rwkv, state_shift, meta_tokens,
           ln1_w, w_in, gdn_conv_w, gdn_a_log, gdn_dt_bias, gdn_norm_w, w_out_a, rwkv_mu, rwkv_w0,
           rwkv_w2, rwkv_a0, rwkv_a2, rwkv_k_k, rwkv_k_a, rwkv_r_k, rwkv_gn_w, rwkv_gn_b, w_out_b,
           w_out, lnf_w):
    assert ln1_w.shape[0] == 1, "single-layer trunk"
    bp, seq, d = x_prompt.shape
    bs, tseq, _ = x_sample.shape
    n_meta = meta_tokens.shape[0]
    _, _, nh_a, dk, dv = state_gdn.shape
    _, _, nh_b, n_b, _ = state_rwkv.shape
    kw, w_qkv = gdn_conv_w.shape[1:]
    lora_w = rwkv_w2.shape[1]
    lora_a = rwkv_a2.shape[1]
    d_a = nh_a * dv
    d_b = nh_b * n_b
    assert w_qkv == 2 * nh_a * dk + d_a and d_a == d and d_b == d and lora_w + lora_a == LANES
    assert n_meta % SUBLANES == 0 and n_meta % 16 == 0 and tseq >= kw - 1

    o_a = w_qkv
    o_b = o_a + nh_a
    o_z = o_b + nh_a
    o_r = o_z + d_a
    o_g = o_r + 3 * d_b + lora_w + lora_a + d_b
    w = w_in[0]
    wr = w[:, o_r:o_g]
    mu = rwkv_mu
    lora0 = 3 * d_b
    cols = [w[:, :w_qkv], w[:, o_z:o_r], wr[:, :lora0], wr[:, lora0 + LANES:], w[:, o_g:],
            wr[:, lora0:lora0 + LANES], w[:, o_a:o_z]]
    used = sum(cw.shape[1] for cw in cols)
    tn = 768
    n_pad = -(-(used + LANES - 2 * nh_a) // tn) * tn
    cols.append(jnp.zeros((d, n_pad - used), F32))
    w_all = jnp.concatenate(cols, axis=1).astype(BF16)
    lay = {"qkv": (0, w_qkv), "za": (w_qkv, d_a), "rkvz": (w_qkv + d_a, 4 * d_b),
           "gate": (w_qkv + d_a + 4 * d_b, 2 * d)}
    lay["wa"] = (lay["gate"][0] + 2 * d, LANES)
    lay["ab"] = (lay["wa"][0] + LANES, LANES)
    for off, width in lay.values():
        assert off % width == 0
    mu_x = jnp.concatenate([mu[:, :lora0], mu[:, lora0 + LANES:]], axis=1)
    mu_wa = mu[:, lora0:lora0 + LANES]

    alr = _pad_lanes(gdn_a_log, LANES)
    dtr = _pad_lanes(gdn_dt_bias, LANES)
    alc = jnp.pad(gdn_a_log.reshape(nh_a, 1), ((0, nh_a), (0, 0)))
    dtc = jnp.pad(gdn_dt_bias.reshape(nh_a, 1), ((0, nh_a), (0, 0)))
    convw = gdn_conv_w[0]
    rk = rwkv_r_k.reshape(1, d_b)
    woa = w_out_a[0].astype(BF16)
    wob = w_out_b[0].astype(BF16)
    wo = w_out[0].astype(BF16)
    lnf = lnf_w.reshape(1, d)

    def branches(p, nb, t, c, t_valid, s_gdn, conv_tail, s_rwkv, x_tail, wa_tail):
        nch = t // c
        abt = _token_major_t(p[:, lay["ab"][0]:lay["ab"][0] + 2 * nh_a], nb, nch, c)
        oa, sg = _gdn(p, abt, s_gdn, conv_tail, convw, alr, dtr, alc, dtc, gdn_norm_w,
                      nb=nb, t=t, c=c, t_valid=t_valid, lay=lay)
        ob, sr = _rwkv(p, s_rwkv, x_tail, wa_tail, mu_x, mu_wa, rwkv_w0, rwkv_w2[0], rwkv_a0,
                       rwkv_a2[0], rwkv_k_k, rwkv_k_a, rk, rwkv_gn_w, rwkv_gn_b,
                       nb=nb, t=t, c=c, t_valid=t_valid, lay=lay)
        return oa, sg, ob, sr

    x0, wx = lay["rkvz"]
    a0_, _ = lay["wa"]

    p_m = _proj(meta_tokens, ln1_w, w_all, apply_norm=True)
    _, sg_m, _, sr_m = branches(
        p_m, 1, n_meta, n_meta, n_meta,
        jnp.zeros((1, nh_a, dk, dv), F32), jnp.zeros((1, SUBLANES, w_qkv), F32),
        jnp.zeros((1, nh_b, n_b, n_b), F32), jnp.zeros((1, SUBLANES, wx), F32),
        jnp.zeros((1, SUBLANES, LANES), F32))

    xp = x_prompt.reshape(bp * seq, d)
    p_p = _proj(xp, ln1_w, w_all, apply_norm=True)
    tail_m = p_m[n_meta - SUBLANES:]
    oa_p, sg_p, ob_p, sr_p = branches(
        p_p, bp, seq, 64, seq, sg_m, tail_m[None, :, :w_qkv], sr_m,
        tail_m[None, :, x0:x0 + wx], tail_m[None, :, a0_:a0_ + LANES])
    y_p = _merge(oa_p, ob_p, p_p, xp, woa, wob, wo, lnf, lay=lay)

    tpad = 16
    xs = jnp.pad(x_sample, ((0, 0), (0, tpad - tseq), (0, 0))).reshape(bs * tpad, d)
    p_s = _proj(xs, ln1_w, w_all, apply_norm=True)
    p_first = _proj(state_shift[0], ln1_w, w_all, apply_norm=False)
    oa_s, sg_s, ob_s, sr_s = branches(
        p_s, bs, tpad, tpad, tseq, state_gdn[0], _tail_rows(state_gdn_conv[0]), state_rwkv[0],
        _tail_rows(p_first[:, None, x0:x0 + wx]), _tail_rows(p_first[:, None, a0_:a0_ + LANES]))
    y_s = _merge(oa_s, ob_s, p_s, xs, woa, wob, wo, lnf, lay=lay)

    shift_p = _rmsnorm_rows(x_prompt[:, -1], ln1_w)
    shift_s = _rmsnorm_rows(x_sample[:, -1], ln1_w)
    conv_p = p_p.reshape(bp, seq, n_pad)[:, seq - (kw - 1):, :w_qkv]
    conv_s = p_s.reshape(bs, tpad, n_pad)[:, tseq - (kw - 1):tseq, :w_qkv]
    return (y_p.reshape(bp, seq, d), y_s.reshape(bs, tpad, d)[:, :tseq],
            sg_p[None], conv_p[None], sr_p[None], shift_p[None],
            sg_s[None], conv_s[None], sr_s[None], shift_s[None])
```

```python
import functools

import jax
import jax.numpy as jnp
from jax import lax
from jax.experimental import pallas as pl
from jax.experimental.pallas import tpu as pltpu

F32 = jnp.float32
BF16 = jnp.bfloat16

EPS = 1e-6
L2_EPS = 1e-6
GN_EPS_PER_CH = 1e-5

LANES = 128
SUBLANES = 8
VMEM_LIMIT = 48 * 1024 * 1024

NN = (((1,), (0,)), ((), ()))
NT = (((1,), (1,)), ((), ()))
TN = (((0,), (0,)), ((), ()))


def _mm(a, b, dims=NN):
    return lax.dot_general(a.astype(BF16), b.astype(BF16), dims, preferred_element_type=F32)


def _mm_f32(a, b, dims=NN):
    return lax.dot_general(a, b, dims, precision=lax.Precision.HIGHEST,
                           preferred_element_type=F32)


def _sigmoid(x):
    return 1.0 / (1.0 + jnp.exp(-x))


def _silu(x):
    return x * _sigmoid(x)


def _softplus(x):
    return jnp.maximum(x, 0.0) + jnp.log(1.0 + jnp.exp(-jnp.abs(x)))


def _shift_rows(cur, prev, k, rid):
    rolled = pltpu.roll(cur, k, 0)
    fix = pltpu.roll(prev, k, 0)
    reps = cur.shape[0] // SUBLANES
    fixfull = jnp.concatenate([fix] * reps, axis=0) if reps > 1 else fix
    return jnp.where(rid < k, fixfull, rolled)


def _tri_masks(c):
    row = lax.broadcasted_iota(jnp.int32, (c, c), 0)
    col = lax.broadcasted_iota(jnp.int32, (c, c), 1)
    diff = row ^ col
    levels = [(diff >> 1) == 0]
    sh = 2
    while (1 << sh) <= c:
        levels.append((diff >> (sh - 1)) == 1)
        sh += 1
    return row, col, levels


def _tri_inv_many(lows, levels):
    nn = [-jnp.where(levels[0], low, 0.0) for low in lows]
    for m in levels[1:]:
        off = [jnp.where(m, low, 0.0) for low in lows]
        xc = [o + _mm(x, o) for x, o in zip(nn, off)]
        nn = [x - (y + _mm(y, x)) for x, y in zip(nn, xc)]
    return nn


def _proj_kernel(x_ref, lnw_ref, w_ref, o_ref, h_ref, *, apply_norm):
    @pl.when(pl.program_id(1) == 0)
    def _():
        x = x_ref[...]
        if apply_norm:
            x = x * lax.rsqrt(jnp.mean(x * x, axis=-1, keepdims=True) + EPS)
            x = x * lnw_ref[...]
        h_ref[...] = x.astype(BF16)

    o_ref[...] = jnp.dot(h_ref[...], w_ref[...], preferred_element_type=F32)


def _proj(x, lnw, w, *, apply_norm):
    m, d = x.shape
    n = w.shape[1]
    tm = min(m, 1024)
    tn = 768
    assert m % tm == 0 and n % tn == 0
    return pl.pallas_call(
        functools.partial(_proj_kernel, apply_norm=apply_norm),
        out_shape=jax.ShapeDtypeStruct((m, n), F32),
        grid=(m // tm, n // tn),
        in_specs=[pl.BlockSpec((tm, d), lambda i, j: (i, 0)),
                  pl.BlockSpec((1, d), lambda i, j: (0, 0)),
                  pl.BlockSpec((d, tn), lambda i, j: (0, j))],
        out_specs=pl.BlockSpec((tm, tn), lambda i, j: (i, j)),
        scratch_shapes=[pltpu.VMEM((tm, d), BF16)],
        compiler_params=pltpu.CompilerParams(
            dimension_semantics=("parallel", "arbitrary"), vmem_limit_bytes=VMEM_LIMIT),
        name="proj",
    )(x, lnw, w)


def _rmsnorm_kernel(x_ref, w_ref, o_ref):
    x = x_ref[...]
    o_ref[...] = x * lax.rsqrt(jnp.mean(x * x, axis=-1, keepdims=True) + EPS) * w_ref[...]


def _rmsnorm_rows(x, w):
    return pl.pallas_call(
        _rmsnorm_kernel,
        out_shape=jax.ShapeDtypeStruct(x.shape, F32),
        name="rmsnorm_rows",
    )(x, w)


def _gdn_kernel(qkv_ref, z_ref, ab_ref, abt_ref, s0_ref, prev0_ref, convw_ref,
                alr_ref, dtr_ref, alc_ref, dtc_ref, nw_ref,
                o_ref, s_ref, prev_ref, *, c, ns, nh, dk, dv, t_valid, t_total):
    ci = pl.program_id(1)
    shared = s0_ref.shape[0] == 1

    @pl.when(ci == 0)
    def _():
        for j in range(ns):
            s_ref[j] = s0_ref[0 if shared else j]
            prev_ref[j] = prev0_ref[0 if shared else j]

    row, col, levels = _tri_masks(c)
    tril = row >= col
    strict = row > col
    kw = convw_ref.shape[0]
    rid = lax.broadcasted_iota(jnp.int32, (c, dk), 0)
    seqs = range(ns)

    ab = [ab_ref[j] for j in seqs]
    g_c = [-jnp.exp(alr_ref[...]) * _softplus(x + dtr_ref[...]) for x in ab]
    beta_c = [_sigmoid(x) for x in ab]
    abt = [abt_ref[j, 0] for j in seqs]
    g_r = [-jnp.exp(alc_ref[...]) * _softplus(x + dtc_ref[...]) for x in abt]
    if t_valid < t_total:
        tok_c = ci * c + lax.broadcasted_iota(jnp.int32, ab[0].shape, 0)
        tok_r = ci * c + lax.broadcasted_iota(jnp.int32, abt[0].shape, 1)
        g_c = [jnp.where(tok_c < t_valid, x, 0.0) for x in g_c]
        beta_c = [jnp.where(tok_c < t_valid, x, 0.0) for x in beta_c]
        g_r = [jnp.where(tok_r < t_valid, x, 0.0) for x in g_r]
    gc_c = _mm_f32(tril.astype(F32), jnp.concatenate(g_c, axis=1))
    gc_r = _mm_f32(jnp.concatenate(g_r, axis=0), (row <= col).astype(F32))

    def conv(j, col0, width):
        cur = qkv_ref[j, :, col0:col0 + width]
        prev = prev_ref[j, :, col0:col0 + width]
        acc = None
        for k in range(kw - 1, -1, -1):
            src = cur if k == 0 else _shift_rows(cur, prev, k, rid)
            term = src * convw_ref[kw - 1 - k:kw - k, col0:col0 + width]
            acc = term if acc is None else acc + term
        return _silu(acc)

    units = [(j, h) for j in seqs for h in range(nh)]
    us = range(len(units))
    gcc = [gc_c[:, j * LANES + h:j * LANES + h + 1] for j, h in units]
    gcr = [gc_r[j * 2 * nh + h:j * 2 * nh + h + 1, :] for j, h in units]
    beta = [beta_c[j][:, nh + h:nh + h + 1] for j, h in units]
    glast = [x[c - 1:c, :] for x in gcc]
    decay = [jnp.where(tril, jnp.exp(jnp.where(tril, gcc[u] - gcr[u], 0.0)), 0.0) for u in us]
    q = [conv(j, h * dk, dk) for j, h in units]
    k = [conv(j, nh * dk + h * dk, dk) for j, h in units]
    v = [conv(j, 2 * nh * dk + h * dv, dv) for j, h in units]
    q = [x * lax.rsqrt(jnp.sum(x * x, axis=-1, keepdims=True) + L2_EPS) * (dk ** -0.5) for x in q]
    k = [x * lax.rsqrt(jnp.sum(x * x, axis=-1, keepdims=True) + L2_EPS) for x in k]
    kb = [k[u] * beta[u] for u in us]
    vb = [v[u] * beta[u] for u in us]
    kbg = [kb[u] * jnp.exp(gcc[u]) for u in us]
    qg = [q[u] * jnp.exp(gcc[u]) for u in us]
    k_tail = [k[u] * jnp.exp(glast[u] - gcc[u]) for u in us]
    lower = [jnp.where(strict, _mm(kb[u], k[u], NT) * decay[u], 0.0) for u in us]
    qk = [jnp.where(tril, _mm(q[u], k[u], NT) * decay[u], 0.0) for u in us]
    s_old = [s_ref[j, h] for j, h in units]
    qs = [_mm(qg[u], s_old[u]) for u in us]
    nn = _tri_inv_many(lower, levels)
    u_ = [vb[u] + _mm(nn[u], vb[u]) for u in us]
    w_ = [kbg[u] + _mm(nn[u], kbg[u]) for u in us]
    v_new = [u_[u] - _mm(w_[u], s_old[u]) for u in us]
    kv = [_mm(k_tail[u], v_new[u], TN) for u in us]
    for u, (j, h) in enumerate(units):
        s_ref[j, h] = s_old[u] * jnp.exp(glast[u]) + kv[u]
    o = [qs[u] + _mm(qk[u], v_new[u]) for u in us]
    for u, (j, h) in enumerate(units):
        oh = o[u] * lax.rsqrt(jnp.mean(o[u] * o[u], axis=-1, keepdims=True) + EPS) * nw_ref[...]
        oh = oh * _silu(z_ref[j, :, h * dv:(h + 1) * dv])
        o_ref[j, :, h * dv:(h + 1) * dv] = oh.astype(BF16)

    for j in seqs:
        prev_ref[j] = qkv_ref[j, c - SUBLANES:c, :]


def _state_specs(s0, ns, tail_shapes):
    shared = s0.shape[0] == 1
    lead = 1 if shared else ns
    idx4 = (lambda b, i: (0, 0, 0, 0)) if shared else (lambda b, i: (b, 0, 0, 0))
    idx3 = (lambda b, i: (0, 0, 0)) if shared else (lambda b, i: (b, 0, 0))
    return ([pl.BlockSpec((lead,) + s0.shape[1:], idx4)]
            + [pl.BlockSpec((lead,) + ts, idx3) for ts in tail_shapes])


def _gdn(p, abt, s0, prev0, convw, alr, dtr, alc, dtc, nw, *, c, ns, t_valid, lay):
    nb, t, _ = p.shape
    nh, dk, dv = s0.shape[1:]
    wq = lay["qkv"][1]
    const2 = lambda b, i: (0, 0)
    return pl.pallas_call(
        functools.partial(_gdn_kernel, c=c, ns=ns, nh=nh, dk=dk, dv=dv, t_valid=t_valid,
                          t_total=t),
        out_shape=(jax.ShapeDtypeStruct((nb, t, nh * dv), BF16),
                   jax.ShapeDtypeStruct((nb, nh, dk, dv), F32)),
        grid=(nb // ns, t // c),
        in_specs=[
            pl.BlockSpec((ns, c, wq), lambda b, i: (b, i, lay["qkv"][0] // wq)),
            pl.BlockSpec((ns, c, nh * dv), lambda b, i: (b, i, lay["za"][0] // (nh * dv))),
            pl.BlockSpec((ns, c, LANES), lambda b, i: (b, i, lay["ab"][0] // LANES)),
            pl.BlockSpec((ns, 1, 2 * nh, c), lambda b, i: (b, i, 0, 0)),
            *_state_specs(s0, ns, [(SUBLANES, wq)]),
            pl.BlockSpec(convw.shape, const2),
            pl.BlockSpec(alr.shape, const2),
            pl.BlockSpec(dtr.shape, const2),
            pl.BlockSpec(alc.shape, const2),
            pl.BlockSpec(dtc.shape, const2),
            pl.BlockSpec(nw.shape, const2),
        ],
        out_specs=(pl.BlockSpec((ns, c, nh * dv), lambda b, i: (b, i, 0)),
                   pl.BlockSpec((ns, nh, dk, dv), lambda b, i: (b, 0, 0, 0))),
        scratch_shapes=[pltpu.VMEM((ns, SUBLANES, wq), F32)],
        compiler_params=pltpu.CompilerParams(
            dimension_semantics=("parallel", "arbitrary"), vmem_limit_bytes=VMEM_LIMIT),
        name="gdn",
    )(p, p, p, abt, s0, prev0, convw, alr, dtr, alc, dtc, nw)


def _rwkv_kernel(x_ref, wa_ref, s0_ref, prev0_ref, prevwa0_ref, mu_ref, muwa_ref,
                 w0_ref, w2_ref, a0_ref, a2_ref, kk_ref, ka_ref, rk_ref, gnw_ref, gnb_ref,
                 o_ref, s_ref, prev_ref, prevwa_ref, *, c, ns, nh, n, t_valid, t_total):
    ci = pl.program_id(1)
    d = nh * n
    shared = s0_ref.shape[0] == 1

    @pl.when(ci == 0)
    def _():
        for j in range(ns):
            s_ref[j] = s0_ref[0 if shared else j]
            prev_ref[j] = prev0_ref[0 if shared else j]
            prevwa_ref[j] = prevwa0_ref[0 if shared else j]

    row, col, levels = _tri_masks(c)
    tril = row >= col
    strict = row > col
    tril_b = tril.astype(BF16)
    rid = lax.broadcasted_iota(jnp.int32, (c, d), 0)
    rid_wa = lax.broadcasted_iota(jnp.int32, (c, LANES), 0)
    lora = w2_ref.shape[0]
    seqs = range(ns)

    def prepare(j):
        def mix(g):
            cur = x_ref[j, :, g * d:(g + 1) * d]
            prv = _shift_rows(cur, prev_ref[j, :, g * d:(g + 1) * d], 1, rid)
            return cur + mu_ref[:, g * d:(g + 1) * d] * (prv - cur)

        r = mix(0)
        kb = mix(1)
        vb = mix(2)
        wa = wa_ref[j]
        wa = wa + muwa_ref[...] * (_shift_rows(wa, prevwa_ref[j], 1, rid_wa) - wa)
        w_raw = w0_ref[...] + _mm(jnp.tanh(wa[:, 0:lora]), w2_ref[...])
        w_log = -jnp.exp(-_softplus(-w_raw) - 0.5)
        a = _sigmoid(a0_ref[...] + _mm(wa[:, lora:], a2_ref[...]))
        kku = kb * kk_ref[...]
        k2 = kb * (1.0 + (a - 1.0) * ka_ref[...])
        if t_valid < t_total:
            ok = (ci * c + rid) < t_valid
            w_log = jnp.where(ok, w_log, 0.0)
            kku = jnp.where(ok, kku, 0.0)
            vb = jnp.where(ok, vb, 0.0)
        w_hi = w_log.astype(BF16)
        rem = w_log - w_hi.astype(F32)
        w_mid = rem.astype(BF16)
        w_lo = (rem - w_mid.astype(F32)).astype(BF16)
        gcum = (jnp.dot(tril_b, w_hi, preferred_element_type=F32)
                + jnp.dot(tril_b, w_mid, preferred_element_type=F32)
                + jnp.dot(tril_b, w_lo, preferred_element_type=F32))
        glast = gcum[c - 1:c, :]
        return dict(r=r, k2=k2, vb=vb, a=a, kku=kku,
                    g_in=jnp.exp(gcum),
                    g_ex=jnp.exp(gcum - w_log),
                    g_inv=jnp.exp(-gcum),
                    g_tail=jnp.exp(glast - gcum),
                    g_all=jnp.exp(glast))

    pre = [prepare(j) for j in seqs]
    units = [(j, h) for j in seqs for h in range(nh)]
    us = range(len(units))

    def cut(name):
        return [pre[j][name][:, h * n:(h + 1) * n] for j, h in units]

    kk = cut("kku")
    kk = [x * lax.rsqrt(jnp.sum(x * x, axis=-1, keepdims=True) + L2_EPS) for x in kk]
    a_h, r_h, k_h, v_h = cut("a"), cut("r"), cut("k2"), cut("vb")
    g_in, g_ex, g_inv, g_tail, g_all = cut("g_in"), cut("g_ex"), cut("g_inv"), cut("g_tail"), cut("g_all")
    b = [kk[u] * a_h[u] for u in us]
    lhs = [jnp.concatenate([kk[u] * g_ex[u], r_h[u] * g_in[u]], axis=0) for u in us]
    rhs = [jnp.concatenate([b[u] * g_inv[u], k_h[u] * g_inv[u]], axis=0) for u in us]
    k_tail = [k_h[u] * g_tail[u] for u in us]
    b_tail = [b[u] * g_tail[u] for u in us]
    big = [_mm(lhs[u], rhs[u], NT) for u in us]
    s_old = [s_ref[j, h] for j, h in units]
    from_s = [_mm(lhs[u], s_old[u], NT) for u in us]
    low = [jnp.where(strict, x[0:c, 0:c], 0.0) for x in big]
    m_kv = [jnp.where(strict, x[0:c, c:2 * c], 0.0) for x in big]
    q_b = [jnp.where(tril, x[c:2 * c, 0:c], 0.0) for x in big]
    p_kv = [jnp.where(tril, x[c:2 * c, c:2 * c], 0.0) for x in big]
    mv = [_mm(m_kv[u], v_h[u]) for u in us]
    pv = [_mm(p_kv[u], v_h[u]) for u in us]
    nn = _tri_inv_many(low, levels)
    rhs_sa = [from_s[u][0:c] + mv[u] for u in us]
    sa = [rhs_sa[u] + _mm(nn[u], rhs_sa[u]) for u in us]
    qs = [_mm(q_b[u], sa[u]) for u in us]
    vk = [_mm(v_h[u], k_tail[u], TN) for u in us]
    sb = [_mm(sa[u], b_tail[u], TN) for u in us]
    for u, (j, h) in enumerate(units):
        s_ref[j, h] = s_old[u] * g_all[u] + vk[u] - sb[u]
    outs = []
    for u, (j, h) in enumerate(units):
        sl = slice(h * n, (h + 1) * n)
        y = from_s[u][c:2 * c] + pv[u] - qs[u]
        mean = jnp.mean(y, axis=-1, keepdims=True)
        var = jnp.mean(jnp.square(y - mean), axis=-1, keepdims=True)
        yn = (y - mean) * lax.rsqrt(var + n * GN_EPS_PER_CH)
        yn = yn * gnw_ref[:, sl] + gnb_ref[:, sl]
        bonus = jnp.sum(r_h[u] * k_h[u] * rk_ref[:, sl], axis=-1, keepdims=True) * v_h[u]
        outs.append(yn + bonus)
    for j in seqs:
        cur = x_ref[j, :, 3 * d:4 * d]
        prv = _shift_rows(cur, prev_ref[j, :, 3 * d:4 * d], 1, rid)
        zb = cur + mu_ref[:, 3 * d:4 * d] * (prv - cur)
        o = jnp.concatenate(outs[j * nh:(j + 1) * nh], axis=-1) * _silu(zb)
        o_ref[j] = o.astype(BF16)

    for j in seqs:
        prev_ref[j] = x_ref[j, c - SUBLANES:c, :]
        prevwa_ref[j] = wa_ref[j, c - SUBLANES:c, :]


def _rwkv(p, s0, prev0, prevwa0, mu, muwa, w0, w2, a0, a2, kk, ka, rk, gnw, gnb,
          *, c, ns, t_valid, lay):
    nb, t, _ = p.shape
    nh, n = s0.shape[1:3]
    d = nh * n
    wx = lay["rkvz"][1]
    const2 = lambda b, i: (0, 0)
    vec = pl.BlockSpec((1, d), const2)
    return pl.pallas_call(
        functools.partial(_rwkv_kernel, c=c, ns=ns, nh=nh, n=n, t_valid=t_valid, t_total=t),
        out_shape=(jax.ShapeDtypeStruct((nb, t, d), BF16),
                   jax.ShapeDtypeStruct((nb, nh, n, n), F32)),
        grid=(nb // ns, t // c),
        in_specs=[
            pl.BlockSpec((ns, c, wx), lambda b, i: (b, i, lay["rkvz"][0] // wx)),
            pl.BlockSpec((ns, c, LANES), lambda b, i: (b, i, lay["wa"][0] // LANES)),
            *_state_specs(s0, ns, [(SUBLANES, wx), (SUBLANES, LANES)]),
            pl.BlockSpec((1, wx), const2),
            pl.BlockSpec((1, LANES), const2),
            vec,
            pl.BlockSpec(w2.shape, const2),
            vec,
            pl.BlockSpec(a2.shape, const2),
            vec, vec, vec, vec, vec,
        ],
        out_specs=(pl.BlockSpec((ns, c, d), lambda b, i: (b, i, 0)),
                   pl.BlockSpec((ns, nh, n, n), lambda b, i: (b, 0, 0, 0))),
        scratch_shapes=[pltpu.VMEM((ns, SUBLANES, wx), F32),
                        pltpu.VMEM((ns, SUBLANES, LANES), F32)],
        compiler_params=pltpu.CompilerParams(
            dimension_semantics=("parallel", "arbitrary"), vmem_limit_bytes=VMEM_LIMIT),
        name="rwkv",
    )(p, p, s0, prev0, prevwa0, mu, muwa, w0, w2, a0, a2, kk, ka, rk, gnw, gnb)


def _merge_kernel(oa_ref, ob_ref, gate_ref, x_ref, woa_ref, wob_ref, wo_ref, lnf_ref, y_ref):
    d = x_ref.shape[1]
    ba = jnp.dot(oa_ref[...], woa_ref[...], preferred_element_type=F32)
    bb = jnp.dot(ob_ref[...], wob_ref[...], preferred_element_type=F32)
    gates = _sigmoid(gate_ref[...])
    merged = gates[:, :d] * ba + gates[:, d:] * bb
    xn = x_ref[...] + jnp.dot(merged.astype(BF16), wo_ref[...], preferred_element_type=F32)
    y_ref[...] = xn * lax.rsqrt(jnp.mean(xn * xn, axis=-1, keepdims=True) + EPS) * lnf_ref[...]


def _merge(oa, ob, p, x, woa, wob, wo, lnf, *, lay):
    m, d = x.shape
    tm = min(m, 512)
    wg = lay["gate"][1]
    const2 = lambda i: (0, 0)
    return pl.pallas_call(
        _merge_kernel,
        out_shape=jax.ShapeDtypeStruct((m, d), F32),
        grid=(m // tm,),
        in_specs=[
            pl.BlockSpec((tm, oa.shape[1]), lambda i: (i, 0)),
            pl.BlockSpec((tm, ob.shape[1]), lambda i: (i, 0)),
            pl.BlockSpec((tm, wg), lambda i: (i, lay["gate"][0] // wg)),
            pl.BlockSpec((tm, d), lambda i: (i, 0)),
            pl.BlockSpec(woa.shape, const2),
            pl.BlockSpec(wob.shape, const2),
            pl.BlockSpec(wo.shape, const2),
            pl.BlockSpec((1, d), const2),
        ],
        out_specs=pl.BlockSpec((tm, d), lambda i: (i, 0)),
        compiler_params=pltpu.CompilerParams(
            dimension_semantics=("parallel",), vmem_limit_bytes=VMEM_LIMIT),
        name="merge",
    )(oa, ob, p, x, woa, wob, wo, lnf)


def _pad_lanes(v, width):
    return jnp.pad(v, ((0, 0), (0, width - v.shape[1])))


def _tail_rows(rows):
    return jnp.pad(rows, ((0, 0), (SUBLANES - rows.shape[1], 0), (0, 0)))


def _token_major_t(ab, c):
    nb, t, k = ab.shape
    return jnp.transpose(ab.reshape(nb, t // c, c, k), (0, 1, 3, 2))


def kernel(x_prompt, x_sample, state_gdn, state_gdn_conv, state_rwkv, state_shift, meta_tokens,
           ln1_w, w_in, gdn_conv_w, gdn_a_log, gdn_dt_bias, gdn_norm_w, w_out_a, rwkv_mu, rwkv_w0,
           rwkv_w2, rwkv_a0, rwkv_a2, rwkv_k_k, rwkv_k_a, rwkv_r_k, rwkv_gn_w, rwkv_gn_b, w_out_b,
           w_out, lnf_w):
    assert ln1_w.shape[0] == 1, "single-layer trunk"
    bp, seq, d = x_prompt.shape
    bs, tseq, _ = x_sample.shape
    n_meta = meta_tokens.shape[0]
    _, _, nh_a, dk, dv = state_gdn.shape
    _, _, nh_b, n_b, _ = state_rwkv.shape
    kw, w_qkv = gdn_conv_w.shape[1:]
    lora_w = rwkv_w2.shape[1]
    lora_a = rwkv_a2.shape[1]
    d_a = nh_a * dv
    d_b = nh_b * n_b
    assert w_qkv == 2 * nh_a * dk + d_a and d_a == d and d_b == d and lora_w + lora_a == LANES
    assert n_meta % SUBLANES == 0 and n_meta % 16 == 0 and tseq >= kw - 1

    o_a = w_qkv
    o_b = o_a + nh_a
    o_z = o_b + nh_a
    o_r = o_z + d_a
    o_g = o_r + 3 * d_b + lora_w + lora_a + d_b
    w = w_in[0]
    wr = w[:, o_r:o_g]
    mu = rwkv_mu
    lora0 = 3 * d_b
    cols = [w[:, :w_qkv], w[:, o_z:o_r], wr[:, :lora0], wr[:, lora0 + LANES:], w[:, o_g:],
            wr[:, lora0:lora0 + LANES], w[:, o_a:o_z]]
    used = sum(cw.shape[1] for cw in cols)
    tn = 768
    n_pad = -(-(used + LANES - 2 * nh_a) // tn) * tn
    cols.append(jnp.zeros((d, n_pad - used), F32))
    w_all = jnp.concatenate(cols, axis=1).astype(BF16)
    lay = {"qkv": (0, w_qkv), "za": (w_qkv, d_a), "rkvz": (w_qkv + d_a, 4 * d_b),
           "gate": (w_qkv + d_a + 4 * d_b, 2 * d)}
    lay["wa"] = (lay["gate"][0] + 2 * d, LANES)
    lay["ab"] = (lay["wa"][0] + LANES, LANES)
    for off, width in lay.values():
        assert off % width == 0
    mu_x = jnp.concatenate([mu[:, :lora0], mu[:, lora0 + LANES:]], axis=1)
    mu_wa = mu[:, lora0:lora0 + LANES]

    alr = _pad_lanes(gdn_a_log, LANES)
    dtr = _pad_lanes(gdn_dt_bias, LANES)
    alc = jnp.pad(gdn_a_log.reshape(nh_a, 1), ((0, nh_a), (0, 0)))
    dtc = jnp.pad(gdn_dt_bias.reshape(nh_a, 1), ((0, nh_a), (0, 0)))
    convw = gdn_conv_w[0]
    rk = rwkv_r_k.reshape(1, d_b)
    woa = w_out_a[0].astype(BF16)
    wob = w_out_b[0].astype(BF16)
    wo = w_out[0].astype(BF16)
    lnf = lnf_w.reshape(1, d)

    def branches(p, c, ns_a, ns_b, t_valid, s_gdn, conv_tail, s_rwkv, x_tail, wa_tail):
        abt = _token_major_t(p[:, :, lay["ab"][0]:lay["ab"][0] + 2 * nh_a], c)
        oa, sg = _gdn(p, abt, s_gdn, conv_tail, convw, alr, dtr, alc, dtc, gdn_norm_w,
                      c=c, ns=ns_a, t_valid=t_valid, lay=lay)
        ob, sr = _rwkv(p, s_rwkv, x_tail, wa_tail, mu_x, mu_wa, rwkv_w0, rwkv_w2[0], rwkv_a0,
                       rwkv_a2[0], rwkv_k_k, rwkv_k_a, rk, rwkv_gn_w, rwkv_gn_b,
                       c=c, ns=ns_b, t_valid=t_valid, lay=lay)
        return oa, sg, ob, sr

    x0, wx = lay["rkvz"]
    a0_, _ = lay["wa"]

    p_m = _proj(meta_tokens, ln1_w, w_all, apply_norm=True)
    _, sg_m, _, sr_m = branches(
        p_m[None], n_meta, 1, 1, n_meta,
        jnp.zeros((1, nh_a, dk, dv), F32), jnp.zeros((1, SUBLANES, w_qkv), F32),
        jnp.zeros((1, nh_b, n_b, n_b), F32), jnp.zeros((1, SUBLANES, wx), F32),
        jnp.zeros((1, SUBLANES, LANES), F32))

    xp = x_prompt.reshape(bp * seq, d)
    p_p = _proj(xp, ln1_w, w_all, apply_norm=True)
    tail_m = p_m[n_meta - SUBLANES:]
    oa_p, sg_p, ob_p, sr_p = branches(
        p_p.reshape(bp, seq, n_pad), 64, 4, 1, seq, sg_m, tail_m[None, :, :w_qkv], sr_m,
        tail_m[None, :, x0:x0 + wx], tail_m[None, :, a0_:a0_ + LANES])
    y_p = _merge(oa_p.reshape(bp * seq, d), ob_p.reshape(bp * seq, d), p_p, xp, woa, wob, wo, lnf,
                 lay=lay)

    tpad = 16
    xs = jnp.pad(x_sample, ((0, 0), (0, tpad - tseq), (0, 0))).reshape(bs * tpad, d)
    p_s = _proj(xs, ln1_w, w_all, apply_norm=True)
    p_first = _proj(state_shift[0], ln1_w, w_all, apply_norm=False)
    oa_s, sg_s, ob_s, sr_s = branches(
        p_s.reshape(bs, tpad, n_pad), tpad, 8, 1, tseq, state_gdn[0],
        _tail_rows(state_gdn_conv[0]), state_rwkv[0],
        _tail_rows(p_first[:, None, x0:x0 + wx]), _tail_rows(p_first[:, None, a0_:a0_ + LANES]))
    y_s = _merge(oa_s.reshape(bs * tpad, d), ob_s.reshape(bs * tpad, d), p_s, xs, woa, wob, wo,
                 lnf, lay=lay)

    shift_p = _rmsnorm_rows(x_prompt[:, -1], ln1_w)
    shift_s = _rmsnorm_rows(x_sample[:, -1], ln1_w)
    conv_p = p_p.reshape(bp, seq, n_pad)[:, seq - (kw - 1):, :w_qkv]
    conv_s = p_s.reshape(bs, tpad, n_pad)[:, tseq - (kw - 1):tseq, :w_qkv]
    return (y_p.reshape(bp, seq, d), y_s.reshape(bs, tpad, d)[:, :tseq],
            sg_p[None], conv_p[None], sr_p[None], shift_p[None],
            sg_s[None], conv_s[None], sr_s[None], shift_s[None])
```

```python
import functools

import jax
import jax.numpy as jnp
from jax import lax
from jax.experimental import pallas as pl
from jax.experimental.pallas import tpu as pltpu

F32 = jnp.float32
BF16 = jnp.bfloat16

EPS = 1e-6
L2_EPS = 1e-6
GN_EPS_PER_CH = 1e-5

LANES = 128
SUBLANES = 8
VMEM_LIMIT = 48 * 1024 * 1024

NN = (((1,), (0,)), ((), ()))
NT = (((1,), (1,)), ((), ()))
TN = (((0,), (0,)), ((), ()))


def _mm(a, b, dims=NN):
    return lax.dot_general(a.astype(BF16), b.astype(BF16), dims, preferred_element_type=F32)


def _mm_f32(a, b, dims=NN):
    return lax.dot_general(a, b, dims, precision=lax.Precision.HIGHEST,
                           preferred_element_type=F32)


def _sigmoid(x):
    return 1.0 / (1.0 + jnp.exp(-x))


def _silu(x):
    return x * _sigmoid(x)


def _softplus(x):
    return jnp.maximum(x, 0.0) + jnp.log(1.0 + jnp.exp(-jnp.abs(x)))


def _shift_rows(cur, prev, k, rid):
    rolled = pltpu.roll(cur, k, 0)
    fix = pltpu.roll(prev, k, 0)
    reps = cur.shape[0] // SUBLANES
    fixfull = jnp.concatenate([fix] * reps, axis=0) if reps > 1 else fix
    return jnp.where(rid < k, fixfull, rolled)


def _tri_masks(c):
    row = lax.broadcasted_iota(jnp.int32, (c, c), 0)
    col = lax.broadcasted_iota(jnp.int32, (c, c), 1)
    diff = row ^ col
    levels = [(diff >> 1) == 0]
    sh = 2
    while (1 << sh) <= c:
        levels.append((diff >> (sh - 1)) == 1)
        sh += 1
    return row, col, levels


def _tri_inv_many(lows, levels):
    nn = [-jnp.where(levels[0], low, 0.0) for low in lows]
    for m in levels[1:]:
        off = [jnp.where(m, low, 0.0) for low in lows]
        xc = [o + _mm(x, o) for x, o in zip(nn, off)]
        nn = [x - (y + _mm(y, x)) for x, y in zip(nn, xc)]
    return nn


def _proj_kernel(x_ref, lnw_ref, w_ref, o_ref, h_ref, *, apply_norm):
    @pl.when(pl.program_id(1) == 0)
    def _():
        x = x_ref[...]
        if apply_norm:
            x = x * lax.rsqrt(jnp.mean(x * x, axis=-1, keepdims=True) + EPS)
            x = x * lnw_ref[...]
        h_ref[...] = x.astype(BF16)

    o_ref[...] = jnp.dot(h_ref[...], w_ref[...], preferred_element_type=F32)


def _proj(x, lnw, w, *, apply_norm):
    m, d = x.shape
    n = w.shape[1]
    tm = min(m, 1024)
    tn = 768
    assert m % tm == 0 and n % tn == 0
    return pl.pallas_call(
        functools.partial(_proj_kernel, apply_norm=apply_norm),
        out_shape=jax.ShapeDtypeStruct((m, n), F32),
        grid=(m // tm, n // tn),
        in_specs=[pl.BlockSpec((tm, d), lambda i, j: (i, 0)),
                  pl.BlockSpec((1, d), lambda i, j: (0, 0)),
                  pl.BlockSpec((d, tn), lambda i, j: (0, j))],
        out_specs=pl.BlockSpec((tm, tn), lambda i, j: (i, j)),
        scratch_shapes=[pltpu.VMEM((tm, d), BF16)],
        compiler_params=pltpu.CompilerParams(
            dimension_semantics=("parallel", "arbitrary"), vmem_limit_bytes=VMEM_LIMIT),
        name="proj",
    )(x, lnw, w)


def _rmsnorm_kernel(x_ref, w_ref, o_ref):
    x = x_ref[...]
    o_ref[...] = x * lax.rsqrt(jnp.mean(x * x, axis=-1, keepdims=True) + EPS) * w_ref[...]


def _rmsnorm_rows(x, w):
    return pl.pallas_call(
        _rmsnorm_kernel,
        out_shape=jax.ShapeDtypeStruct(x.shape, F32),
        name="rmsnorm_rows",
    )(x, w)


def _gdn_kernel(qkv_ref, z_ref, ab_ref, abt_ref, s0_ref, prev0_ref, convw_ref,
                alr_ref, dtr_ref, alc_ref, dtc_ref, nw_ref,
                o_ref, s_ref, prev_ref, *, c, ns, nh, dk, dv, t_valid, t_total):
    ci = pl.program_id(1)
    shared = s0_ref.shape[0] == 1

    @pl.when(ci == 0)
    def _():
        for j in range(ns):
            s_ref[j] = s0_ref[0 if shared else j]
            prev_ref[j] = prev0_ref[0 if shared else j]

    row, col, levels = _tri_masks(c)
    tril = row >= col
    strict = row > col
    kw = convw_ref.shape[0]
    rid = lax.broadcasted_iota(jnp.int32, (c, dk), 0)
    seqs = range(ns)

    ab = [ab_ref[j] for j in seqs]
    g_c = [-jnp.exp(alr_ref[...]) * _softplus(x + dtr_ref[...]) for x in ab]
    beta_c = [_sigmoid(x) for x in ab]
    abt = [abt_ref[j, 0] for j in seqs]
    g_r = [-jnp.exp(alc_ref[...]) * _softplus(x + dtc_ref[...]) for x in abt]
    if t_valid < t_total:
        tok_c = ci * c + lax.broadcasted_iota(jnp.int32, ab[0].shape, 0)
        tok_r = ci * c + lax.broadcasted_iota(jnp.int32, abt[0].shape, 1)
        g_c = [jnp.where(tok_c < t_valid, x, 0.0) for x in g_c]
        beta_c = [jnp.where(tok_c < t_valid, x, 0.0) for x in beta_c]
        g_r = [jnp.where(tok_r < t_valid, x, 0.0) for x in g_r]
    gc_c = _mm_f32(tril.astype(F32), jnp.concatenate(g_c, axis=1))
    gc_r = _mm_f32(jnp.concatenate(g_r, axis=0), (row <= col).astype(F32))

    def conv(j, col0, width):
        cur = qkv_ref[j, :, col0:col0 + width]
        prev = prev_ref[j, :, col0:col0 + width]
        acc = None
        for k in range(kw - 1, -1, -1):
            src = cur if k == 0 else _shift_rows(cur, prev, k, rid)
            term = src * convw_ref[kw - 1 - k:kw - k, col0:col0 + width]
            acc = term if acc is None else acc + term
        return _silu(acc)

    units = [(j, h) for j in seqs for h in range(nh)]
    us = range(len(units))
    gcc = [gc_c[:, j * LANES + h:j * LANES + h + 1] for j, h in units]
    gcr = [gc_r[j * 2 * nh + h:j * 2 * nh + h + 1, :] for j, h in units]
    beta = [beta_c[j][:, nh + h:nh + h + 1] for j, h in units]
    glast = [x[c - 1:c, :] for x in gcc]
    decay = [jnp.where(tril, jnp.exp(jnp.where(tril, gcc[u] - gcr[u], 0.0)), 0.0) for u in us]
    q = [conv(j, h * dk, dk) for j, h in units]
    k = [conv(j, nh * dk + h * dk, dk) for j, h in units]
    v = [conv(j, 2 * nh * dk + h * dv, dv) for j, h in units]
    q = [x * lax.rsqrt(jnp.sum(x * x, axis=-1, keepdims=True) + L2_EPS) * (dk ** -0.5) for x in q]
    k = [x * lax.rsqrt(jnp.sum(x * x, axis=-1, keepdims=True) + L2_EPS) for x in k]
    kb = [k[u] * beta[u] for u in us]
    vb = [v[u] * beta[u] for u in us]
    kbg = [kb[u] * jnp.exp(gcc[u]) for u in us]
    qg = [q[u] * jnp.exp(gcc[u]) for u in us]
    k_tail = [k[u] * jnp.exp(glast[u] - gcc[u]) for u in us]
    lower = [jnp.where(strict, _mm(kb[u], k[u], NT) * decay[u], 0.0) for u in us]
    qk = [jnp.where(tril, _mm(q[u], k[u], NT) * decay[u], 0.0) for u in us]
    s_old = [s_ref[j, h] for j, h in units]
    qs = [_mm(qg[u], s_old[u]) for u in us]
    nn = _tri_inv_many(lower, levels)
    u_ = [vb[u] + _mm(nn[u], vb[u]) for u in us]
    w_ = [kbg[u] + _mm(nn[u], kbg[u]) for u in us]
    v_new = [u_[u] - _mm(w_[u], s_old[u]) for u in us]
    kv = [_mm(k_tail[u], v_new[u], TN) for u in us]
    for u, (j, h) in enumerate(units):
        s_ref[j, h] = s_old[u] * jnp.exp(glast[u]) + kv[u]
    o = [qs[u] + _mm(qk[u], v_new[u]) for u in us]
    for u, (j, h) in enumerate(units):
        oh = o[u] * lax.rsqrt(jnp.mean(o[u] * o[u], axis=-1, keepdims=True) + EPS) * nw_ref[...]
        oh = oh * _silu(z_ref[j, :, h * dv:(h + 1) * dv])
        o_ref[j, :, h * dv:(h + 1) * dv] = oh.astype(BF16)

    for j in seqs:
        prev_ref[j] = qkv_ref[j, c - SUBLANES:c, :]


def _state_specs(s0, ns, tail_shapes):
    shared = s0.shape[0] == 1
    lead = 1 if shared else ns
    idx4 = (lambda b, i: (0, 0, 0, 0)) if shared else (lambda b, i: (b, 0, 0, 0))
    idx3 = (lambda b, i: (0, 0, 0)) if shared else (lambda b, i: (b, 0, 0))
    return ([pl.BlockSpec((lead,) + s0.shape[1:], idx4)]
            + [pl.BlockSpec((lead,) + ts, idx3) for ts in tail_shapes])


def _gdn(p, abt, s0, prev0, convw, alr, dtr, alc, dtc, nw, *, c, ns, t_valid, lay):
    nb, t, _ = p.shape
    nh, dk, dv = s0.shape[1:]
    wq = lay["qkv"][1]
    const2 = lambda b, i: (0, 0)
    return pl.pallas_call(
        functools.partial(_gdn_kernel, c=c, ns=ns, nh=nh, dk=dk, dv=dv, t_valid=t_valid,
                          t_total=t),
        out_shape=(jax.ShapeDtypeStruct((nb, t, nh * dv), BF16),
                   jax.ShapeDtypeStruct((nb, nh, dk, dv), F32)),
        grid=(nb // ns, t // c),
        in_specs=[
            pl.BlockSpec((ns, c, wq), lambda b, i: (b, i, lay["qkv"][0] // wq)),
            pl.BlockSpec((ns, c, nh * dv), lambda b, i: (b, i, lay["za"][0] // (nh * dv))),
            pl.BlockSpec((ns, c, LANES), lambda b, i: (b, i, lay["ab"][0] // LANES)),
            pl.BlockSpec((ns, 1, 2 * nh, c), lambda b, i: (b, i, 0, 0)),
            *_state_specs(s0, ns, [(SUBLANES, wq)]),
            pl.BlockSpec(convw.shape, const2),
            pl.BlockSpec(alr.shape, const2),
            pl.BlockSpec(dtr.shape, const2),
            pl.BlockSpec(alc.shape, const2),
            pl.BlockSpec(dtc.shape, const2),
            pl.BlockSpec(nw.shape, const2),
        ],
        out_specs=(pl.BlockSpec((ns, c, nh * dv), lambda b, i: (b, i, 0)),
                   pl.BlockSpec((ns, nh, dk, dv), lambda b, i: (b, 0, 0, 0))),
        scratch_shapes=[pltpu.VMEM((ns, SUBLANES, wq), F32)],
        compiler_params=pltpu.CompilerParams(
            dimension_semantics=("parallel", "arbitrary"), vmem_limit_bytes=VMEM_LIMIT),
        name="gdn",
    )(p, p, p, abt, s0, prev0, convw, alr, dtr, alc, dtc, nw)


def _token_shift(x_ref, prev_ref, mu_ref, j, lo, hi):
    cur = x_ref[j, :, lo:hi]
    rid = lax.broadcasted_iota(jnp.int32, cur.shape, 0)
    prv = _shift_rows(cur, prev_ref[j, :, lo:hi], 1, rid)
    return cur + mu_ref[:, lo:hi] * (prv - cur)


def _rwkv_prepare(j, ci, x_ref, wa_ref, prev_ref, prevwa_ref, mu_ref, muwa_ref, w0_ref, w2_ref,
                  a0_ref, a2_ref, kk_ref, ka_ref, *, c, d, t_valid, t_total):
    lora = w2_ref.shape[0]
    r = _token_shift(x_ref, prev_ref, mu_ref, j, 0, d)
    kb = _token_shift(x_ref, prev_ref, mu_ref, j, d, 2 * d)
    vb = _token_shift(x_ref, prev_ref, mu_ref, j, 2 * d, 3 * d)
    wa = wa_ref[j]
    rid_wa = lax.broadcasted_iota(jnp.int32, wa.shape, 0)
    wa = wa + muwa_ref[...] * (_shift_rows(wa, prevwa_ref[j], 1, rid_wa) - wa)
    w_raw = w0_ref[...] + _mm(jnp.tanh(wa[:, 0:lora]), w2_ref[...])
    w_log = -jnp.exp(-_softplus(-w_raw) - 0.5)
    a = _sigmoid(a0_ref[...] + _mm(wa[:, lora:], a2_ref[...]))
    kku = kb * kk_ref[...]
    k2 = kb * (1.0 + (a - 1.0) * ka_ref[...])
    if t_valid < t_total:
        ok = (ci * c + lax.broadcasted_iota(jnp.int32, (c, d), 0)) < t_valid
        w_log = jnp.where(ok, w_log, 0.0)
        kku = jnp.where(ok, kku, 0.0)
        vb = jnp.where(ok, vb, 0.0)
    row = lax.broadcasted_iota(jnp.int32, (c, c), 0)
    col = lax.broadcasted_iota(jnp.int32, (c, c), 1)
    tril_b = (row >= col).astype(BF16)
    w_hi = w_log.astype(BF16)
    rem = w_log - w_hi.astype(F32)
    w_mid = rem.astype(BF16)
    w_lo = (rem - w_mid.astype(F32)).astype(BF16)
    gcum = (jnp.dot(tril_b, w_hi, preferred_element_type=F32)
            + jnp.dot(tril_b, w_mid, preferred_element_type=F32)
            + jnp.dot(tril_b, w_lo, preferred_element_type=F32))
    glast = gcum[c - 1:c, :]
    return dict(r=r, k2=k2, vb=vb, a=a, kku=kku,
                g_in=jnp.exp(gcum),
                g_ex=jnp.exp(gcum - w_log),
                g_inv=jnp.exp(-gcum),
                g_tail=jnp.exp(glast - gcum),
                g_all=jnp.exp(glast))


def _rwkv_kernel(x_ref, wa_ref, s0_ref, prev0_ref, prevwa0_ref, mu_ref, muwa_ref,
                 w0_ref, w2_ref, a0_ref, a2_ref, kk_ref, ka_ref, rk_ref, gnw_ref, gnb_ref,
                 o_ref, s_ref, prev_ref, prevwa_ref, *, c, ns, nh, n, t_valid, t_total):
    ci = pl.program_id(1)
    d = nh * n
    shared = s0_ref.shape[0] == 1

    @pl.when(ci == 0)
    def _():
        for j in range(ns):
            s_ref[j] = s0_ref[0 if shared else j]
            prev_ref[j] = prev0_ref[0 if shared else j]
            prevwa_ref[j] = prevwa0_ref[0 if shared else j]

    row, col, levels = _tri_masks(c)
    tril = row >= col
    strict = row > col
    seqs = range(ns)

    pre = [_rwkv_prepare(j, ci, x_ref, wa_ref, prev_ref, prevwa_ref, mu_ref, muwa_ref, w0_ref,
                         w2_ref, a0_ref, a2_ref, kk_ref, ka_ref, c=c, d=d, t_valid=t_valid,
                         t_total=t_total) for j in seqs]
    units = [(j, h) for j in seqs for h in range(nh)]
    us = range(len(units))

    def cut(name):
        return [pre[j][name][:, h * n:(h + 1) * n] for j, h in units]

    kk = cut("kku")
    kk = [x * lax.rsqrt(jnp.sum(x * x, axis=-1, keepdims=True) + L2_EPS) for x in kk]
    a_h, r_h, k_h, v_h = cut("a"), cut("r"), cut("k2"), cut("vb")
    g_in, g_ex, g_inv, g_tail, g_all = cut("g_in"), cut("g_ex"), cut("g_inv"), cut("g_tail"), cut("g_all")
    b = [kk[u] * a_h[u] for u in us]
    lhs = [jnp.concatenate([kk[u] * g_ex[u], r_h[u] * g_in[u]], axis=0) for u in us]
    rhs = [jnp.concatenate([b[u] * g_inv[u], k_h[u] * g_inv[u]], axis=0) for u in us]
    k_tail = [k_h[u] * g_tail[u] for u in us]
    b_tail = [b[u] * g_tail[u] for u in us]
    big = [_mm(lhs[u], rhs[u], NT) for u in us]
    s_old = [s_ref[j, h] for j, h in units]
    from_s = [_mm(lhs[u], s_old[u], NT) for u in us]
    low = [jnp.where(strict, x[0:c, 0:c], 0.0) for x in big]
    m_kv = [jnp.where(strict, x[0:c, c:2 * c], 0.0) for x in big]
    q_b = [jnp.where(tril, x[c:2 * c, 0:c], 0.0) for x in big]
    p_kv = [jnp.where(tril, x[c:2 * c, c:2 * c], 0.0) for x in big]
    mv = [_mm(m_kv[u], v_h[u]) for u in us]
    pv = [_mm(p_kv[u], v_h[u]) for u in us]
    nn = _tri_inv_many(low, levels)
    rhs_sa = [from_s[u][0:c] + mv[u] for u in us]
    sa = [rhs_sa[u] + _mm(nn[u], rhs_sa[u]) for u in us]
    qs = [_mm(q_b[u], sa[u]) for u in us]
    vk = [_mm(v_h[u], k_tail[u], TN) for u in us]
    sb = [_mm(sa[u], b_tail[u], TN) for u in us]
    for u, (j, h) in enumerate(units):
        s_ref[j, h] = s_old[u] * g_all[u] + vk[u] - sb[u]
    outs = []
    for u, (j, h) in enumerate(units):
        sl = slice(h * n, (h + 1) * n)
        y = from_s[u][c:2 * c] + pv[u] - qs[u]
        mean = jnp.mean(y, axis=-1, keepdims=True)
        var = jnp.mean(jnp.square(y - mean), axis=-1, keepdims=True)
        yn = (y - mean) * lax.rsqrt(var + n * GN_EPS_PER_CH)
        yn = yn * gnw_ref[:, sl] + gnb_ref[:, sl]
        bonus = jnp.sum(r_h[u] * k_h[u] * rk_ref[:, sl], axis=-1, keepdims=True) * v_h[u]
        outs.append(yn + bonus)
    for j in seqs:
        zb = _token_shift(x_ref, prev_ref, mu_ref, j, 3 * d, 4 * d)
        o = jnp.concatenate(outs[j * nh:(j + 1) * nh], axis=-1) * _silu(zb)
        o_ref[j] = o.astype(BF16)

    for j in seqs:
        prev_ref[j] = x_ref[j, c - SUBLANES:c, :]
        prevwa_ref[j] = wa_ref[j, c - SUBLANES:c, :]


def _rwkv_pair_kernel(x_ref, wa_ref, s0_ref, prev0_ref, prevwa0_ref, mu_ref, muwa_ref,
                      w0_ref, w2_ref, a0_ref, a2_ref, kk_ref, ka_ref, rk_ref, gnw_ref, gnb_ref,
                      o_ref, s_ref, prev_ref, prevwa_ref, sbd_ref,
                      *, c, ns, nh, n, t_valid, t_total):
    ci = pl.program_id(1)
    d = nh * n
    npair = nh // 2
    ngrp = d // LANES
    shared = s0_ref.shape[0] == 1
    seqs = range(ns)

    @pl.when(ci == 0)
    def _():
        zero = jnp.zeros((n, n), F32)
        for j in seqs:
            jj = 0 if shared else j
            for p in range(npair):
                top = jnp.concatenate([s0_ref[jj, 2 * p], zero], axis=1)
                bot = jnp.concatenate([zero, s0_ref[jj, 2 * p + 1]], axis=1)
                sbd_ref[j, p] = jnp.concatenate([top, bot], axis=0)
            prev_ref[j] = prev0_ref[jj]
            prevwa_ref[j] = prevwa0_ref[jj]

    rowi = lax.broadcasted_iota(jnp.int32, (c, LANES), 0)
    lane = lax.broadcasted_iota(jnp.int32, (c, LANES), 1)
    lcol = lane & (c - 1)
    tril = rowi >= lcol
    strict = rowi > lcol
    diff = rowi ^ lcol
    levels = [(diff >> 1) == 0]
    sh = 2
    while (1 << sh) <= c:
        levels.append((diff >> (sh - 1)) == 1)
        sh += 1
    head0 = (lane < c).astype(F32).astype(BF16)
    head1 = (lane >= c).astype(F32).astype(BF16)
    row2 = lax.broadcasted_iota(jnp.int32, (2 * c, LANES), 0)
    lane2 = lax.broadcasted_iota(jnp.int32, (2 * c, LANES), 1)
    log2c = c.bit_length() - 1
    same_head = (row2 >> log2c) == (lane2 >> log2c)
    seg_ones = same_head.astype(F32).astype(BF16)

    def bd(xb):
        return jnp.concatenate([xb * head0, xb * head1], axis=0)

    def dot(a, b, dims=NN):
        return lax.dot_general(a, b, dims, preferred_element_type=F32)

    def seg_sum(x, pieces):
        stk = jnp.concatenate([x[:, i * LANES:(i + 1) * LANES] for i in range(ngrp)], axis=0)
        hi = stk.astype(BF16)
        acc = dot(hi, seg_ones)
        if pieces == 2:
            acc = acc + dot((stk - hi.astype(F32)).astype(BF16), seg_ones)
        return jnp.concatenate([acc[i * c:(i + 1) * c] for i in range(ngrp)], axis=1)

    pre = [_rwkv_prepare(j, ci, x_ref, wa_ref, prev_ref, prevwa_ref, mu_ref, muwa_ref, w0_ref,
                         w2_ref, a0_ref, a2_ref, kk_ref, ka_ref, c=c, d=d, t_valid=t_valid,
                         t_total=t_total) for j in seqs]
    full = []
    for q in pre:
        kk = q["kku"] * lax.rsqrt(seg_sum(q["kku"] * q["kku"], 1) + L2_EPS)
        b = kk * q["a"]
        full.append(dict(kkg=(kk * q["g_ex"]).astype(BF16), rg=(q["r"] * q["g_in"]).astype(BF16),
                         bh=(b * q["g_inv"]).astype(BF16), kh=(q["k2"] * q["g_inv"]).astype(BF16),
                         kt=(q["k2"] * q["g_tail"]).astype(BF16), bt=(b * q["g_tail"]).astype(BF16),
                         v=q["vb"].astype(BF16)))

    units = [(j, p) for j in seqs for p in range(npair)]
    us = range(len(units))

    def cut(name):
        return [full[j][name][:, p * LANES:(p + 1) * LANES] for j, p in units]

    kkg, rg, bh, kh, kt, bt, v = (cut(k) for k in ("kkg", "rg", "bh", "kh", "kt", "bt", "v"))
    g_all = [pre[j]["g_all"][:, p * LANES:(p + 1) * LANES] for j, p in units]
    lhs = [jnp.concatenate([kkg[u], rg[u]], axis=0) for u in us]
    rhs = [jnp.concatenate([bd(bh[u]), bd(kh[u])], axis=0) for u in us]
    big = [dot(lhs[u], rhs[u], NT) for u in us]
    s_old = [sbd_ref[j, p] for j, p in units]
    from_s = [dot(lhs[u], s_old[u].astype(BF16), NT) for u in us]
    low = [jnp.where(strict, x[0:c, 0:LANES], 0.0) for x in big]
    m_kv = [jnp.where(strict, x[0:c, LANES:2 * LANES], 0.0) for x in big]
    q_b = [jnp.where(tril, x[c:2 * c, 0:LANES], 0.0) for x in big]
    p_kv = [jnp.where(tril, x[c:2 * c, LANES:2 * LANES], 0.0) for x in big]
    mp = [dot(jnp.concatenate([m_kv[u], p_kv[u]], axis=0).astype(BF16), bd(v[u])) for u in us]
    nn = [-jnp.where(levels[0], x, 0.0) for x in low]
    for m in levels[1:]:
        off = [jnp.where(m, x, 0.0) for x in low]
        nnb = [x.astype(BF16) for x in nn]
        xc = [off[u] + dot(nnb[u], bd(off[u].astype(BF16))) for u in us]
        nn = [nn[u] - (xc[u] + dot(xc[u].astype(BF16), bd(nnb[u]))) for u in us]
    rhs_sa = [from_s[u][0:c] + mp[u][0:c] for u in us]
    sa = [rhs_sa[u] + dot(nn[u].astype(BF16), bd(rhs_sa[u].astype(BF16))) for u in us]
    sab = [x.astype(BF16) for x in sa]
    qs = [dot(q_b[u].astype(BF16), bd(sab[u])) for u in us]
    upd = [dot(jnp.concatenate([v[u], -sab[u]], axis=0),
               jnp.concatenate([kt[u], bt[u]], axis=0), TN) for u in us]
    s_new = [s_old[u] * g_all[u] + jnp.where(same_head, upd[u], 0.0) for u in us]
    for u, (j, p) in enumerate(units):
        sbd_ref[j, p] = s_new[u]

    @pl.when(ci == pl.num_programs(1) - 1)
    def _():
        for u, (j, p) in enumerate(units):
            s_ref[j, 2 * p] = s_new[u][0:n, 0:n]
            s_ref[j, 2 * p + 1] = s_new[u][n:2 * n, n:2 * n]

    y = [from_s[u][c:2 * c] + mp[u][c:2 * c] - qs[u] for u in us]
    for j in seqs:
        q = pre[j]
        yj = jnp.concatenate(y[j * npair:(j + 1) * npair], axis=1)
        yc = yj - seg_sum(yj, 2) * (1.0 / n)
        var = seg_sum(yc * yc, 2) * (1.0 / n)
        yn = yc * lax.rsqrt(var + n * GN_EPS_PER_CH) * gnw_ref[...] + gnb_ref[...]
        bonus = seg_sum(q["r"] * q["k2"] * rk_ref[...], 1) * q["vb"]
        zb = _token_shift(x_ref, prev_ref, mu_ref, j, 3 * d, 4 * d)
        o_ref[j] = ((yn + bonus) * _silu(zb)).astype(BF16)

    for j in seqs:
        prev_ref[j] = x_ref[j, c - SUBLANES:c, :]
        prevwa_ref[j] = wa_ref[j, c - SUBLANES:c, :]


def _rwkv(p, s0, prev0, prevwa0, mu, muwa, w0, w2, a0, a2, kk, ka, rk, gnw, gnb,
          *, c, ns, t_valid, lay):
    nb, t, _ = p.shape
    nh, n = s0.shape[1:3]
    d = nh * n
    wx = lay["rkvz"][1]
    const2 = lambda b, i: (0, 0)
    vec = pl.BlockSpec((1, d), const2)
    paired = c == n and 2 * c == LANES
    body = _rwkv_pair_kernel if paired else _rwkv_kernel
    scratch = [pltpu.VMEM((ns, SUBLANES, wx), F32), pltpu.VMEM((ns, SUBLANES, LANES), F32)]
    if paired:
        scratch.append(pltpu.VMEM((ns, nh // 2, LANES, LANES), F32))
    return pl.pallas_call(
        functools.partial(body, c=c, ns=ns, nh=nh, n=n, t_valid=t_valid, t_total=t),
        out_shape=(jax.ShapeDtypeStruct((nb, t, d), BF16),
                   jax.ShapeDtypeStruct((nb, nh, n, n), F32)),
        grid=(nb // ns, t // c),
        in_specs=[
            pl.BlockSpec((ns, c, wx), lambda b, i: (b, i, lay["rkvz"][0] // wx)),
            pl.BlockSpec((ns, c, LANES), lambda b, i: (b, i, lay["wa"][0] // LANES)),
            *_state_specs(s0, ns, [(SUBLANES, wx), (SUBLANES, LANES)]),
            pl.BlockSpec((1, wx), const2),
            pl.BlockSpec((1, LANES), const2),
            vec,
            pl.BlockSpec(w2.shape, const2),
            vec,
            pl.BlockSpec(a2.shape, const2),
            vec, vec, vec, vec, vec,
        ],
        out_specs=(pl.BlockSpec((ns, c, d), lambda b, i: (b, i, 0)),
                   pl.BlockSpec((ns, nh, n, n), lambda b, i: (b, 0, 0, 0))),
        scratch_shapes=scratch,
        compiler_params=pltpu.CompilerParams(
            dimension_semantics=("parallel", "arbitrary"), vmem_limit_bytes=VMEM_LIMIT),
        name="rwkv",
    )(p, p, s0, prev0, prevwa0, mu, muwa, w0, w2, a0, a2, kk, ka, rk, gnw, gnb)


def _merge_kernel(oa_ref, ob_ref, gate_ref, x_ref, woa_ref, wob_ref, wo_ref, lnf_ref, y_ref):
    d = x_ref.shape[1]
    ba = jnp.dot(oa_ref[...], woa_ref[...], preferred_element_type=F32)
    bb = jnp.dot(ob_ref[...], wob_ref[...], preferred_element_type=F32)
    gates = _sigmoid(gate_ref[...])
    merged = gates[:, :d] * ba + gates[:, d:] * bb
    xn = x_ref[...] + jnp.dot(merged.astype(BF16), wo_ref[...], preferred_element_type=F32)
    y_ref[...] = xn * lax.rsqrt(jnp.mean(xn * xn, axis=-1, keepdims=True) + EPS) * lnf_ref[...]


def _merge(oa, ob, p, x, woa, wob, wo, lnf, *, lay):
    m, d = x.shape
    tm = min(m, 512)
    wg = lay["gate"][1]
    const2 = lambda i: (0, 0)
    return pl.pallas_call(
        _merge_kernel,
        out_shape=jax.ShapeDtypeStruct((m, d), F32),
        grid=(m // tm,),
        in_specs=[
            pl.BlockSpec((tm, oa.shape[1]), lambda i: (i, 0)),
            pl.BlockSpec((tm, ob.shape[1]), lambda i: (i, 0)),
            pl.BlockSpec((tm, wg), lambda i: (i, lay["gate"][0] // wg)),
            pl.BlockSpec((tm, d), lambda i: (i, 0)),
            pl.BlockSpec(woa.shape, const2),
            pl.BlockSpec(wob.shape, const2),
            pl.BlockSpec(wo.shape, const2),
            pl.BlockSpec((1, d), const2),
        ],
        out_specs=pl.BlockSpec((tm, d), lambda i: (i, 0)),
        compiler_params=pltpu.CompilerParams(
            dimension_semantics=("parallel",), vmem_limit_bytes=VMEM_LIMIT),
        name="merge",
    )(oa, ob, p, x, woa, wob, wo, lnf)


def _pad_lanes(v, width):
    return jnp.pad(v, ((0, 0), (0, width - v.shape[1])))


def _tail_rows(rows):
    return jnp.pad(rows, ((0, 0), (SUBLANES - rows.shape[1], 0), (0, 0)))


def _token_major_t(ab, c):
    nb, t, k = ab.shape
    return jnp.transpose(ab.reshape(nb, t // c, c, k), (0, 1, 3, 2))


def kernel(x_prompt, x_sample, state_gdn, state_gdn_conv, state_rwkv, state_shift, meta_tokens,
           ln1_w, w_in, gdn_conv_w, gdn_a_log, gdn_dt_bias, gdn_norm_w, w_out_a, rwkv_mu, rwkv_w0,
           rwkv_w2, rwkv_a0, rwkv_a2, rwkv_k_k, rwkv_k_a, rwkv_r_k, rwkv_gn_w, rwkv_gn_b, w_out_b,
           w_out, lnf_w):
    assert ln1_w.shape[0] == 1, "single-layer trunk"
    bp, seq, d = x_prompt.shape
    bs, tseq, _ = x_sample.shape
    n_meta = meta_tokens.shape[0]
    _, _, nh_a, dk, dv = state_gdn.shape
    _, _, nh_b, n_b, _ = state_rwkv.shape
    kw, w_qkv = gdn_conv_w.shape[1:]
    lora_w = rwkv_w2.shape[1]
    lora_a = rwkv_a2.shape[1]
    d_a = nh_a * dv
    d_b = nh_b * n_b
    assert w_qkv == 2 * nh_a * dk + d_a and d_a == d and d_b == d and lora_w + lora_a == LANES
    assert n_meta % SUBLANES == 0 and n_meta % 16 == 0 and tseq >= kw - 1

    o_a = w_qkv
    o_b = o_a + nh_a
    o_z = o_b + nh_a
    o_r = o_z + d_a
    o_g = o_r + 3 * d_b + lora_w + lora_a + d_b
    w = w_in[0]
    wr = w[:, o_r:o_g]
    mu = rwkv_mu
    lora0 = 3 * d_b
    cols = [w[:, :w_qkv], w[:, o_z:o_r], wr[:, :lora0], wr[:, lora0 + LANES:], w[:, o_g:],
            wr[:, lora0:lora0 + LANES], w[:, o_a:o_z]]
    used = sum(cw.shape[1] for cw in cols)
    tn = 768
    n_pad = -(-(used + LANES - 2 * nh_a) // tn) * tn
    cols.append(jnp.zeros((d, n_pad - used), F32))
    w_all = jnp.concatenate(cols, axis=1).astype(BF16)
    lay = {"qkv": (0, w_qkv), "za": (w_qkv, d_a), "rkvz": (w_qkv + d_a, 4 * d_b),
           "gate": (w_qkv + d_a + 4 * d_b, 2 * d)}
    lay["wa"] = (lay["gate"][0] + 2 * d, LANES)
    lay["ab"] = (lay["wa"][0] + LANES, LANES)
    for off, width in lay.values():
        assert off % width == 0
    mu_x = jnp.concatenate([mu[:, :lora0], mu[:, lora0 + LANES:]], axis=1)
    mu_wa = mu[:, lora0:lora0 + LANES]

    alr = _pad_lanes(gdn_a_log, LANES)
    dtr = _pad_lanes(gdn_dt_bias, LANES)
    alc = jnp.pad(gdn_a_log.reshape(nh_a, 1), ((0, nh_a), (0, 0)))
    dtc = jnp.pad(gdn_dt_bias.reshape(nh_a, 1), ((0, nh_a), (0, 0)))
    convw = gdn_conv_w[0]
    rk = rwkv_r_k.reshape(1, d_b)
    woa = w_out_a[0].astype(BF16)
    wob = w_out_b[0].astype(BF16)
    wo = w_out[0].astype(BF16)
    lnf = lnf_w.reshape(1, d)

    def branches(p, c, ns_a, ns_b, t_valid, s_gdn, conv_tail, s_rwkv, x_tail, wa_tail):
        abt = _token_major_t(p[:, :, lay["ab"][0]:lay["ab"][0] + 2 * nh_a], c)
        oa, sg = _gdn(p, abt, s_gdn, conv_tail, convw, alr, dtr, alc, dtc, gdn_norm_w,
                      c=c, ns=ns_a, t_valid=t_valid, lay=lay)
        ob, sr = _rwkv(p, s_rwkv, x_tail, wa_tail, mu_x, mu_wa, rwkv_w0, rwkv_w2[0], rwkv_a0,
                       rwkv_a2[0], rwkv_k_k, rwkv_k_a, rk, rwkv_gn_w, rwkv_gn_b,
                       c=c, ns=ns_b, t_valid=t_valid, lay=lay)
        return oa, sg, ob, sr

    x0, wx = lay["rkvz"]
    a0_, _ = lay["wa"]

    p_m = _proj(meta_tokens, ln1_w, w_all, apply_norm=True)
    _, sg_m, _, sr_m = branches(
        p_m[None], n_meta, 1, 1, n_meta,
        jnp.zeros((1, nh_a, dk, dv), F32), jnp.zeros((1, SUBLANES, w_qkv), F32),
        jnp.zeros((1, nh_b, n_b, n_b), F32), jnp.zeros((1, SUBLANES, wx), F32),
        jnp.zeros((1, SUBLANES, LANES), F32))

    xp = x_prompt.reshape(bp * seq, d)
    p_p = _proj(xp, ln1_w, w_all, apply_norm=True)
    tail_m = p_m[n_meta - SUBLANES:]
    oa_p, sg_p, ob_p, sr_p = branches(
        p_p.reshape(bp, seq, n_pad), 64, 4, 2, seq, sg_m, tail_m[None, :, :w_qkv], sr_m,
        tail_m[None, :, x0:x0 + wx], tail_m[None, :, a0_:a0_ + LANES])
    y_p = _merge(oa_p.reshape(bp * seq, d), ob_p.reshape(bp * seq, d), p_p, xp, woa, wob, wo, lnf,
                 lay=lay)

    tpad = 16
    xs = jnp.pad(x_sample, ((0, 0), (0, tpad - tseq), (0, 0))).reshape(bs * tpad, d)
    p_s = _proj(xs, ln1_w, w_all, apply_norm=True)
    p_first = _proj(state_shift[0], ln1_w, w_all, apply_norm=False)
    oa_s, sg_s, ob_s, sr_s = branches(
        p_s.reshape(bs, tpad, n_pad), tpad, 8, 1, tseq, state_gdn[0],
        _tail_rows(state_gdn_conv[0]), state_rwkv[0],
        _tail_rows(p_first[:, None, x0:x0 + wx]), _tail_rows(p_first[:, None, a0_:a0_ + LANES]))
    y_s = _merge(oa_s.reshape(bs * tpad, d), ob_s.reshape(bs * tpad, d), p_s, xs, woa, wob, wo,
                 lnf, lay=lay)

    shift_p = _rmsnorm_rows(x_prompt[:, -1], ln1_w)
    shift_s = _rmsnorm_rows(x_sample[:, -1], ln1_w)
    conv_p = p_p.reshape(bp, seq, n_pad)[:, seq - (kw - 1):, :w_qkv]
    conv_s = p_s.reshape(bs, tpad, n_pad)[:, tseq - (kw - 1):tseq, :w_qkv]
    return (y_p.reshape(bp, seq, d), y_s.reshape(bs, tpad, d)[:, :tseq],
            sg_p[None], conv_p[None], sr_p[None], shift_p[None],
            sg_s[None], conv_s[None], sr_s[None], shift_s[None])
```

```python
import functools

import jax
import jax.numpy as jnp
from jax import lax
from jax.experimental import pallas as pl
from jax.experimental.pallas import tpu as pltpu

F32 = jnp.float32
BF16 = jnp.bfloat16

EPS = 1e-6
L2_EPS = 1e-6
GN_EPS_PER_CH = 1e-5

LANES = 128
SUBLANES = 8
VMEM_LIMIT = 48 * 1024 * 1024
PROJ_TM = 1024
PROJ_TN = 1536

NN = (((1,), (0,)), ((), ()))
NT = (((1,), (1,)), ((), ()))
TN = (((0,), (0,)), ((), ()))


def _mm(a, b, dims=NN):
    return lax.dot_general(a.astype(BF16), b.astype(BF16), dims, preferred_element_type=F32)


def _mm_f32(a, b, dims=NN):
    return lax.dot_general(a, b, dims, precision=lax.Precision.HIGHEST,
                           preferred_element_type=F32)


def _sigmoid(x):
    return 1.0 / (1.0 + jnp.exp(-x))


def _silu(x):
    return x * _sigmoid(x)


def _softplus(x):
    return jnp.maximum(x, 0.0) + jnp.log(1.0 + jnp.exp(-jnp.abs(x)))


def _shift_rows(cur, prev, k, rid):
    rolled = pltpu.roll(cur, k, 0)
    fix = pltpu.roll(prev, k, 0)
    reps = cur.shape[0] // SUBLANES
    fixfull = jnp.concatenate([fix] * reps, axis=0) if reps > 1 else fix
    return jnp.where(rid < k, fixfull, rolled)


def _tri_masks(c):
    row = lax.broadcasted_iota(jnp.int32, (c, c), 0)
    col = lax.broadcasted_iota(jnp.int32, (c, c), 1)
    diff = row ^ col
    levels = [(diff >> 1) == 0]
    sh = 2
    while (1 << sh) <= c:
        levels.append((diff >> (sh - 1)) == 1)
        sh += 1
    return row, col, levels


def _tri_inv_many(lows, levels):
    nn = [-jnp.where(levels[0], low, 0.0) for low in lows]
    for m in levels[1:]:
        off = [jnp.where(m, low, 0.0) for low in lows]
        xc = [o + _mm(x, o) for x, o in zip(nn, off)]
        nn = [x - (y + _mm(y, x)) for x, y in zip(nn, xc)]
    return nn


def _proj_kernel(x_ref, lnw_ref, w_ref, o_ref, h_ref, *, apply_norm):
    @pl.when(pl.program_id(1) == 0)
    def _():
        x = x_ref[...]
        if apply_norm:
            x = x * lax.rsqrt(jnp.mean(x * x, axis=-1, keepdims=True) + EPS)
            x = x * lnw_ref[...]
        h_ref[...] = x.astype(BF16)

    o_ref[...] = jnp.dot(h_ref[...], w_ref[...], preferred_element_type=F32)


def _proj(x, lnw, w, *, apply_norm):
    m, d = x.shape
    n = w.shape[1]
    tm = min(m, PROJ_TM)
    tn = PROJ_TN
    assert m % tm == 0 and n % tn == 0
    return pl.pallas_call(
        functools.partial(_proj_kernel, apply_norm=apply_norm),
        out_shape=jax.ShapeDtypeStruct((m, n), F32),
        grid=(m // tm, n // tn),
        in_specs=[pl.BlockSpec((tm, d), lambda i, j: (i, 0)),
                  pl.BlockSpec((1, d), lambda i, j: (0, 0)),
                  pl.BlockSpec((d, tn), lambda i, j: (0, j))],
        out_specs=pl.BlockSpec((tm, tn), lambda i, j: (i, j)),
        scratch_shapes=[pltpu.VMEM((tm, d), BF16)],
        compiler_params=pltpu.CompilerParams(
            dimension_semantics=("parallel", "arbitrary"), vmem_limit_bytes=VMEM_LIMIT),
        name="proj",
    )(x, lnw, w)


def _rmsnorm_kernel(x_ref, w_ref, o_ref):
    x = x_ref[...]
    o_ref[...] = x * lax.rsqrt(jnp.mean(x * x, axis=-1, keepdims=True) + EPS) * w_ref[...]


def _rmsnorm_rows(x, w):
    return pl.pallas_call(
        _rmsnorm_kernel,
        out_shape=jax.ShapeDtypeStruct(x.shape, F32),
        name="rmsnorm_rows",
    )(x, w)


def _gdn_kernel(qkv_ref, z_ref, ab_ref, abt_ref, s0_ref, prev0_ref, convw_ref,
                alr_ref, dtr_ref, alc_ref, dtc_ref, nw_ref,
                o_ref, s_ref, prev_ref, *, c, ns, nh, dk, dv, t_valid, t_total):
    ci = pl.program_id(1)
    shared = s0_ref.shape[0] == 1

    @pl.when(ci == 0)
    def _():
        for j in range(ns):
            s_ref[j] = s0_ref[0 if shared else j]
            prev_ref[j] = prev0_ref[0 if shared else j]

    row, col, levels = _tri_masks(c)
    tril = row >= col
    strict = row > col
    kw = convw_ref.shape[0]
    rid = lax.broadcasted_iota(jnp.int32, (c, dk), 0)
    seqs = range(ns)

    ab = [ab_ref[j] for j in seqs]
    g_c = [-jnp.exp(alr_ref[...]) * _softplus(x + dtr_ref[...]) for x in ab]
    beta_c = [_sigmoid(x) for x in ab]
    abt = [abt_ref[j, 0] for j in seqs]
    g_r = [-jnp.exp(alc_ref[...]) * _softplus(x + dtc_ref[...]) for x in abt]
    if t_valid < t_total:
        tok_c = ci * c + lax.broadcasted_iota(jnp.int32, ab[0].shape, 0)
        tok_r = ci * c + lax.broadcasted_iota(jnp.int32, abt[0].shape, 1)
        g_c = [jnp.where(tok_c < t_valid, x, 0.0) for x in g_c]
        beta_c = [jnp.where(tok_c < t_valid, x, 0.0) for x in beta_c]
        g_r = [jnp.where(tok_r < t_valid, x, 0.0) for x in g_r]
    gc_c = _mm_f32(tril.astype(F32), jnp.concatenate(g_c, axis=1))
    gc_r = _mm_f32(jnp.concatenate(g_r, axis=0), (row <= col).astype(F32))

    def conv(j, col0, width):
        cur = qkv_ref[j, :, col0:col0 + width]
        prev = prev_ref[j, :, col0:col0 + width]
        acc = None
        for k in range(kw - 1, -1, -1):
            src = cur if k == 0 else _shift_rows(cur, prev, k, rid)
            term = src * convw_ref[kw - 1 - k:kw - k, col0:col0 + width]
            acc = term if acc is None else acc + term
        return _silu(acc)

    units = [(j, h) for j in seqs for h in range(nh)]
    us = range(len(units))
    gcc = [gc_c[:, j * LANES + h:j * LANES + h + 1] for j, h in units]
    gcr = [gc_r[j * 2 * nh + h:j * 2 * nh + h + 1, :] for j, h in units]
    beta = [beta_c[j][:, nh + h:nh + h + 1] for j, h in units]
    glast = [x[c - 1:c, :] for x in gcc]
    decay = [jnp.where(tril, jnp.exp(jnp.where(tril, gcc[u] - gcr[u], 0.0)), 0.0) for u in us]
    q = [conv(j, h * dk, dk) for j, h in units]
    k = [conv(j, nh * dk + h * dk, dk) for j, h in units]
    v = [conv(j, 2 * nh * dk + h * dv, dv) for j, h in units]
    q = [x * lax.rsqrt(jnp.sum(x * x, axis=-1, keepdims=True) + L2_EPS) * (dk ** -0.5) for x in q]
    k = [x * lax.rsqrt(jnp.sum(x * x, axis=-1, keepdims=True) + L2_EPS) for x in k]
    kb = [k[u] * beta[u] for u in us]
    vb = [v[u] * beta[u] for u in us]
    kbg = [kb[u] * jnp.exp(gcc[u]) for u in us]
    qg = [q[u] * jnp.exp(gcc[u]) for u in us]
    k_tail = [k[u] * jnp.exp(glast[u] - gcc[u]) for u in us]
    lower = [jnp.where(strict, _mm(kb[u], k[u], NT) * decay[u], 0.0) for u in us]
    qk = [jnp.where(tril, _mm(q[u], k[u], NT) * decay[u], 0.0) for u in us]
    s_old = [s_ref[j, h] for j, h in units]
    qs = [_mm(qg[u], s_old[u]) for u in us]
    nn = _tri_inv_many(lower, levels)
    u_ = [vb[u] + _mm(nn[u], vb[u]) for u in us]
    w_ = [kbg[u] + _mm(nn[u], kbg[u]) for u in us]
    v_new = [u_[u] - _mm(w_[u], s_old[u]) for u in us]
    kv = [_mm(k_tail[u], v_new[u], TN) for u in us]
    for u, (j, h) in enumerate(units):
        s_ref[j, h] = s_old[u] * jnp.exp(glast[u]) + kv[u]
    o = [qs[u] + _mm(qk[u], v_new[u]) for u in us]
    for u, (j, h) in enumerate(units):
        oh = o[u] * lax.rsqrt(jnp.mean(o[u] * o[u], axis=-1, keepdims=True) + EPS) * nw_ref[...]
        oh = oh * _silu(z_ref[j, :, h * dv:(h + 1) * dv])
        o_ref[j, :, h * dv:(h + 1) * dv] = oh.astype(BF16)

    for j in seqs:
        prev_ref[j] = qkv_ref[j, c - SUBLANES:c, :]


def _state_specs(s0, ns, tail_shapes):
    shared = s0.shape[0] == 1
    lead = 1 if shared else ns
    idx4 = (lambda b, i: (0, 0, 0, 0)) if shared else (lambda b, i: (b, 0, 0, 0))
    idx3 = (lambda b, i: (0, 0, 0)) if shared else (lambda b, i: (b, 0, 0))
    return ([pl.BlockSpec((lead,) + s0.shape[1:], idx4)]
            + [pl.BlockSpec((lead,) + ts, idx3) for ts in tail_shapes])


def _gdn(p, abt, s0, prev0, convw, alr, dtr, alc, dtc, nw, *, c, ns, t_valid, lay):
    nb, t, _ = p.shape
    nh, dk, dv = s0.shape[1:]
    wq = lay["qkv"][1]
    const2 = lambda b, i: (0, 0)
    return pl.pallas_call(
        functools.partial(_gdn_kernel, c=c, ns=ns, nh=nh, dk=dk, dv=dv, t_valid=t_valid,
                          t_total=t),
        out_shape=(jax.ShapeDtypeStruct((nb, t, nh * dv), BF16),
                   jax.ShapeDtypeStruct((nb, nh, dk, dv), F32)),
        grid=(nb // ns, t // c),
        in_specs=[
            pl.BlockSpec((ns, c, wq), lambda b, i: (b, i, lay["qkv"][0] // wq)),
            pl.BlockSpec((ns, c, nh * dv), lambda b, i: (b, i, lay["za"][0] // (nh * dv))),
            pl.BlockSpec((ns, c, LANES), lambda b, i: (b, i, lay["ab"][0] // LANES)),
            pl.BlockSpec((ns, 1, 2 * nh, c), lambda b, i: (b, i, 0, 0)),
            *_state_specs(s0, ns, [(SUBLANES, wq)]),
            pl.BlockSpec(convw.shape, const2),
            pl.BlockSpec(alr.shape, const2),
            pl.BlockSpec(dtr.shape, const2),
            pl.BlockSpec(alc.shape, const2),
            pl.BlockSpec(dtc.shape, const2),
            pl.BlockSpec(nw.shape, const2),
        ],
        out_specs=(pl.BlockSpec((ns, c, nh * dv), lambda b, i: (b, i, 0)),
                   pl.BlockSpec((ns, nh, dk, dv), lambda b, i: (b, 0, 0, 0))),
        scratch_shapes=[pltpu.VMEM((ns, SUBLANES, wq), F32)],
        compiler_params=pltpu.CompilerParams(
            dimension_semantics=("parallel", "arbitrary"), vmem_limit_bytes=VMEM_LIMIT),
        name="gdn",
    )(p, p, p, abt, s0, prev0, convw, alr, dtr, alc, dtc, nw)


def _token_shift(x_ref, prev_ref, mu_ref, j, lo, hi):
    cur = x_ref[j, :, lo:hi]
    rid = lax.broadcasted_iota(jnp.int32, cur.shape, 0)
    prv = _shift_rows(cur, prev_ref[j, :, lo:hi], 1, rid)
    return cur + mu_ref[:, lo:hi] * (prv - cur)


def _rwkv_prepare(j, ci, x_ref, wa_ref, prev_ref, prevwa_ref, mu_ref, muwa_ref, w0_ref, w2_ref,
                  a0_ref, a2_ref, kk_ref, ka_ref, *, c, d, t_valid, t_total):
    lora = w2_ref.shape[0]
    r = _token_shift(x_ref, prev_ref, mu_ref, j, 0, d)
    kb = _token_shift(x_ref, prev_ref, mu_ref, j, d, 2 * d)
    vb = _token_shift(x_ref, prev_ref, mu_ref, j, 2 * d, 3 * d)
    wa = wa_ref[j]
    rid_wa = lax.broadcasted_iota(jnp.int32, wa.shape, 0)
    wa = wa + muwa_ref[...] * (_shift_rows(wa, prevwa_ref[j], 1, rid_wa) - wa)
    w_raw = w0_ref[...] + _mm(jnp.tanh(wa[:, 0:lora]), w2_ref[...])
    w_log = -jnp.exp(-_softplus(-w_raw) - 0.5)
    a = _sigmoid(a0_ref[...] + _mm(wa[:, lora:], a2_ref[...]))
    kku = kb * kk_ref[...]
    k2 = kb * (1.0 + (a - 1.0) * ka_ref[...])
    if t_valid < t_total:
        ok = (ci * c + lax.broadcasted_iota(jnp.int32, (c, d), 0)) < t_valid
        w_log = jnp.where(ok, w_log, 0.0)
        kku = jnp.where(ok, kku, 0.0)
        vb = jnp.where(ok, vb, 0.0)
    row = lax.broadcasted_iota(jnp.int32, (c, c), 0)
    col = lax.broadcasted_iota(jnp.int32, (c, c), 1)
    tril_b = (row >= col).astype(BF16)
    w_hi = w_log.astype(BF16)
    rem = w_log - w_hi.astype(F32)
    w_mid = rem.astype(BF16)
    w_lo = (rem - w_mid.astype(F32)).astype(BF16)
    gcum = (jnp.dot(tril_b, w_hi, preferred_element_type=F32)
            + jnp.dot(tril_b, w_mid, preferred_element_type=F32)
            + jnp.dot(tril_b, w_lo, preferred_element_type=F32))
    glast = gcum[c - 1:c, :]
    return dict(r=r, k2=k2, vb=vb, a=a, kku=kku,
                g_in=jnp.exp(gcum),
                g_ex=jnp.exp(gcum - w_log),
                g_inv=jnp.exp(-gcum),
                g_tail=jnp.exp(glast - gcum),
                g_all=jnp.exp(glast))


def _rwkv_kernel(x_ref, wa_ref, s0_ref, prev0_ref, prevwa0_ref, mu_ref, muwa_ref,
                 w0_ref, w2_ref, a0_ref, a2_ref, kk_ref, ka_ref, rk_ref, gnw_ref, gnb_ref,
                 o_ref, s_ref, prev_ref, prevwa_ref, sbd_ref,
                 *, c, ns, nh, n, t_valid, t_total):
    ci = pl.program_id(1)
    d = nh * n
    npair = nh // 2
    ngrp = d // LANES
    shared = s0_ref.shape[0] == 1
    seqs = range(ns)

    @pl.when(ci == 0)
    def _():
        zero = jnp.zeros((n, n), F32)
        for j in seqs:
            jj = 0 if shared else j
            for p in range(npair):
                top = jnp.concatenate([s0_ref[jj, 2 * p], zero], axis=1)
                bot = jnp.concatenate([zero, s0_ref[jj, 2 * p + 1]], axis=1)
                sbd_ref[j, p] = jnp.concatenate([top, bot], axis=0)
            prev_ref[j] = prev0_ref[jj]
            prevwa_ref[j] = prevwa0_ref[jj]

    w2 = 2 * c
    rowi = lax.broadcasted_iota(jnp.int32, (c, w2), 0)
    lane = lax.broadcasted_iota(jnp.int32, (c, w2), 1)
    lcol = lane & (c - 1)
    tril = rowi >= lcol
    strict = rowi > lcol
    diff = rowi ^ lcol
    levels = [(diff >> 1) == 0]
    sh = 2
    while (1 << sh) <= c:
        levels.append((diff >> (sh - 1)) == 1)
        sh += 1
    tok0 = (lane < c).astype(F32).astype(BF16)
    tok1 = (lane >= c).astype(F32).astype(BF16)
    chl = lax.broadcasted_iota(jnp.int32, (c, LANES), 1)
    ch0 = (chl < n).astype(F32).astype(BF16)
    ch1 = (chl >= n).astype(F32).astype(BF16)
    row2 = lax.broadcasted_iota(jnp.int32, (LANES, LANES), 0)
    lane2 = lax.broadcasted_iota(jnp.int32, (LANES, LANES), 1)
    log2n = n.bit_length() - 1
    same_head = (row2 >> log2n) == (lane2 >> log2n)
    seg_ones = same_head.astype(F32).astype(BF16)

    def bd_tok(xb):
        return jnp.concatenate([xb * tok0, xb * tok1], axis=0)

    def bd_ch(xb):
        return jnp.concatenate([xb * ch0, xb * ch1], axis=0)

    def dot(a, b, dims=NN):
        return lax.dot_general(a, b, dims, preferred_element_type=F32)

    def seg_sum(x, pieces):
        stk = jnp.concatenate([x[:, i * LANES:(i + 1) * LANES] for i in range(ngrp)], axis=0)
        hi = stk.astype(BF16)
        acc = dot(hi, seg_ones)
        if pieces == 2:
            acc = acc + dot((stk - hi.astype(F32)).astype(BF16), seg_ones)
        return jnp.concatenate([acc[i * c:(i + 1) * c] for i in range(ngrp)], axis=1)

    pre = [_rwkv_prepare(j, ci, x_ref, wa_ref, prev_ref, prevwa_ref, mu_ref, muwa_ref, w0_ref,
                         w2_ref, a0_ref, a2_ref, kk_ref, ka_ref, c=c, d=d, t_valid=t_valid,
                         t_total=t_total) for j in seqs]
    full = []
    for q in pre:
        kk = q["kku"] * lax.rsqrt(seg_sum(q["kku"] * q["kku"], 1) + L2_EPS)
        b = kk * q["a"]
        full.append(dict(kkg=(kk * q["g_ex"]).astype(BF16), rg=(q["r"] * q["g_in"]).astype(BF16),
                         bh=(b * q["g_inv"]).astype(BF16), kh=(q["k2"] * q["g_inv"]).astype(BF16),
                         kt=(q["k2"] * q["g_tail"]).astype(BF16), bt=(b * q["g_tail"]).astype(BF16),
                         v=q["vb"].astype(BF16)))

    units = [(j, p) for j in seqs for p in range(npair)]
    us = range(len(units))

    def cut(name):
        return [full[j][name][:, p * LANES:(p + 1) * LANES] for j, p in units]

    kkg, rg, bh, kh, kt, bt, v = (cut(k) for k in ("kkg", "rg", "bh", "kh", "kt", "bt", "v"))
    g_all = [pre[j]["g_all"][:, p * LANES:(p + 1) * LANES] for j, p in units]
    lhs = [jnp.concatenate([kkg[u], rg[u]], axis=0) for u in us]
    rhs = [jnp.concatenate([bd_ch(bh[u]), bd_ch(kh[u])], axis=0) for u in us]
    big = [dot(lhs[u], rhs[u], NT) for u in us]
    s_old = [sbd_ref[j, p] for j, p in units]
    from_s = [dot(lhs[u], s_old[u].astype(BF16), NT) for u in us]
    low = [jnp.where(strict, x[0:c, 0:w2], 0.0) for x in big]
    m_kv = [jnp.where(strict, x[0:c, w2:2 * w2], 0.0) for x in big]
    q_b = [jnp.where(tril, x[c:2 * c, 0:w2], 0.0) for x in big]
    p_kv = [jnp.where(tril, x[c:2 * c, w2:2 * w2], 0.0) for x in big]
    mp = [dot(jnp.concatenate([m_kv[u], p_kv[u]], axis=0).astype(BF16), bd_ch(v[u])) for u in us]
    nn = [-jnp.where(levels[0], x, 0.0) for x in low]
    for m in levels[1:]:
        off = [jnp.where(m, x, 0.0) for x in low]
        nnb = [x.astype(BF16) for x in nn]
        xc = [off[u] + dot(nnb[u], bd_tok(off[u].astype(BF16))) for u in us]
        nn = [nn[u] - (xc[u] + dot(xc[u].astype(BF16), bd_tok(nnb[u]))) for u in us]
    rhs_sa = [from_s[u][0:c] + mp[u][0:c] for u in us]
    sa = [rhs_sa[u] + dot(nn[u].astype(BF16), bd_ch(rhs_sa[u].astype(BF16))) for u in us]
    sab = [x.astype(BF16) for x in sa]
    qs = [dot(q_b[u].astype(BF16), bd_ch(sab[u])) for u in us]
    upd = [dot(jnp.concatenate([v[u], -sab[u]], axis=0),
               jnp.concatenate([kt[u], bt[u]], axis=0), TN) for u in us]
    s_new = [s_old[u] * g_all[u] + jnp.where(same_head, upd[u], 0.0) for u in us]
    for u, (j, p) in enumerate(units):
        sbd_ref[j, p] = s_new[u]

    @pl.when(ci == pl.num_programs(1) - 1)
    def _():
        for u, (j, p) in enumerate(units):
            s_ref[j, 2 * p] = s_new[u][0:n, 0:n]
            s_ref[j, 2 * p + 1] = s_new[u][n:2 * n, n:2 * n]

    y = [from_s[u][c:2 * c] + mp[u][c:2 * c] - qs[u] for u in us]
    for j in seqs:
        q = pre[j]
        yj = jnp.concatenate(y[j * npair:(j + 1) * npair], axis=1)
        yc = yj - seg_sum(yj, 2) * (1.0 / n)
        var = seg_sum(yc * yc, 2) * (1.0 / n)
        yn = yc * lax.rsqrt(var + n * GN_EPS_PER_CH) * gnw_ref[...] + gnb_ref[...]
        bonus = seg_sum(q["r"] * q["k2"] * rk_ref[...], 1) * q["vb"]
        zb = _token_shift(x_ref, prev_ref, mu_ref, j, 3 * d, 4 * d)
        o_ref[j] = ((yn + bonus) * _silu(zb)).astype(BF16)

    for j in seqs:
        prev_ref[j] = x_ref[j, c - SUBLANES:c, :]
        prevwa_ref[j] = wa_ref[j, c - SUBLANES:c, :]


def _rwkv(p, s0, prev0, prevwa0, mu, muwa, w0, w2, a0, a2, kk, ka, rk, gnw, gnb,
          *, c, ns, t_valid, lay):
    nb, t, _ = p.shape
    nh, n = s0.shape[1:3]
    d = nh * n
    wx = lay["rkvz"][1]
    const2 = lambda b, i: (0, 0)
    vec = pl.BlockSpec((1, d), const2)
    assert 2 * n == LANES and c <= n and c & (c - 1) == 0, "two heads per lane group"
    scratch = [pltpu.VMEM((ns, SUBLANES, wx), F32), pltpu.VMEM((ns, SUBLANES, LANES), F32),
               pltpu.VMEM((ns, nh // 2, LANES, LANES), F32)]
    return pl.pallas_call(
        functools.partial(_rwkv_kernel, c=c, ns=ns, nh=nh, n=n, t_valid=t_valid, t_total=t),
        out_shape=(jax.ShapeDtypeStruct((nb, t, d), BF16),
                   jax.ShapeDtypeStruct((nb, nh, n, n), F32)),
        grid=(nb // ns, t // c),
        in_specs=[
            pl.BlockSpec((ns, c, wx), lambda b, i: (b, i, lay["rkvz"][0] // wx)),
            pl.BlockSpec((ns, c, LANES), lambda b, i: (b, i, lay["wa"][0] // LANES)),
            *_state_specs(s0, ns, [(SUBLANES, wx), (SUBLANES, LANES)]),
            pl.BlockSpec((1, wx), const2),
            pl.BlockSpec((1, LANES), const2),
            vec,
            pl.BlockSpec(w2.shape, const2),
            vec,
            pl.BlockSpec(a2.shape, const2),
            vec, vec, vec, vec, vec,
        ],
        out_specs=(pl.BlockSpec((ns, c, d), lambda b, i: (b, i, 0)),
                   pl.BlockSpec((ns, nh, n, n), lambda b, i: (b, 0, 0, 0))),
        scratch_shapes=scratch,
        compiler_params=pltpu.CompilerParams(
            dimension_semantics=("parallel", "arbitrary"), vmem_limit_bytes=VMEM_LIMIT),
        name="rwkv",
    )(p, p, s0, prev0, prevwa0, mu, muwa, w0, w2, a0, a2, kk, ka, rk, gnw, gnb)


def _merge_kernel(oa_ref, ob_ref, gate_ref, x_ref, woa_ref, wob_ref, wo_ref, lnf_ref, y_ref):
    d = x_ref.shape[1]
    ba = jnp.dot(oa_ref[...], woa_ref[...], preferred_element_type=F32)
    bb = jnp.dot(ob_ref[...], wob_ref[...], preferred_element_type=F32)
    gates = _sigmoid(gate_ref[...])
    merged = gates[:, :d] * ba + gates[:, d:] * bb
    xn = x_ref[...] + jnp.dot(merged.astype(BF16), wo_ref[...], preferred_element_type=F32)
    y_ref[...] = xn * lax.rsqrt(jnp.mean(xn * xn, axis=-1, keepdims=True) + EPS) * lnf_ref[...]


def _merge(oa, ob, p, x, woa, wob, wo, lnf, *, lay):
    m, d = x.shape
    tm = min(m, 512)
    wg = lay["gate"][1]
    const2 = lambda i: (0, 0)
    return pl.pallas_call(
        _merge_kernel,
        out_shape=jax.ShapeDtypeStruct((m, d), F32),
        grid=(m // tm,),
        in_specs=[
            pl.BlockSpec((tm, oa.shape[1]), lambda i: (i, 0)),
            pl.BlockSpec((tm, ob.shape[1]), lambda i: (i, 0)),
            pl.BlockSpec((tm, wg), lambda i: (i, lay["gate"][0] // wg)),
            pl.BlockSpec((tm, d), lambda i: (i, 0)),
            pl.BlockSpec(woa.shape, const2),
            pl.BlockSpec(wob.shape, const2),
            pl.BlockSpec(wo.shape, const2),
            pl.BlockSpec((1, d), const2),
        ],
        out_specs=pl.BlockSpec((tm, d), lambda i: (i, 0)),
        compiler_params=pltpu.CompilerParams(
            dimension_semantics=("parallel",), vmem_limit_bytes=VMEM_LIMIT),
        name="merge",
    )(oa, ob, p, x, woa, wob, wo, lnf)


def _pad_lanes(v, width):
    return jnp.pad(v, ((0, 0), (0, width - v.shape[1])))


def _tail_rows(rows):
    return jnp.pad(rows, ((0, 0), (SUBLANES - rows.shape[1], 0), (0, 0)))


def _token_major_t(ab, c):
    nb, t, k = ab.shape
    return jnp.transpose(ab.reshape(nb, t // c, c, k), (0, 1, 3, 2))


def kernel(x_prompt, x_sample, state_gdn, state_gdn_conv, state_rwkv, state_shift, meta_tokens,
           ln1_w, w_in, gdn_conv_w, gdn_a_log, gdn_dt_bias, gdn_norm_w, w_out_a, rwkv_mu, rwkv_w0,
           rwkv_w2, rwkv_a0, rwkv_a2, rwkv_k_k, rwkv_k_a, rwkv_r_k, rwkv_gn_w, rwkv_gn_b, w_out_b,
           w_out, lnf_w):
    assert ln1_w.shape[0] == 1, "single-layer trunk"
    bp, seq, d = x_prompt.shape
    bs, tseq, _ = x_sample.shape
    n_meta = meta_tokens.shape[0]
    _, _, nh_a, dk, dv = state_gdn.shape
    _, _, nh_b, n_b, _ = state_rwkv.shape
    kw, w_qkv = gdn_conv_w.shape[1:]
    lora_w = rwkv_w2.shape[1]
    lora_a = rwkv_a2.shape[1]
    d_a = nh_a * dv
    d_b = nh_b * n_b
    assert w_qkv == 2 * nh_a * dk + d_a and d_a == d and d_b == d and lora_w + lora_a == LANES
    assert n_meta % SUBLANES == 0 and n_meta % 16 == 0 and tseq >= kw - 1

    o_a = w_qkv
    o_b = o_a + nh_a
    o_z = o_b + nh_a
    o_r = o_z + d_a
    o_g = o_r + 3 * d_b + lora_w + lora_a + d_b
    w = w_in[0]
    wr = w[:, o_r:o_g]
    mu = rwkv_mu
    lora0 = 3 * d_b
    cols = [w[:, :w_qkv], w[:, o_z:o_r], wr[:, :lora0], wr[:, lora0 + LANES:], w[:, o_g:],
            wr[:, lora0:lora0 + LANES], w[:, o_a:o_z]]
    used = sum(cw.shape[1] for cw in cols)
    n_pad = -(-(used + LANES - 2 * nh_a) // PROJ_TN) * PROJ_TN
    cols.append(jnp.zeros((d, n_pad - used), F32))
    w_all = jnp.concatenate([cw.astype(BF16) for cw in cols], axis=1)
    lay = {"qkv": (0, w_qkv), "za": (w_qkv, d_a), "rkvz": (w_qkv + d_a, 4 * d_b),
           "gate": (w_qkv + d_a + 4 * d_b, 2 * d)}
    lay["wa"] = (lay["gate"][0] + 2 * d, LANES)
    lay["ab"] = (lay["wa"][0] + LANES, LANES)
    for off, width in lay.values():
        assert off % width == 0
    mu_x = jnp.concatenate([mu[:, :lora0], mu[:, lora0 + LANES:]], axis=1)
    mu_wa = mu[:, lora0:lora0 + LANES]

    alr = _pad_lanes(gdn_a_log, LANES)
    dtr = _pad_lanes(gdn_dt_bias, LANES)
    alc = jnp.pad(gdn_a_log.reshape(nh_a, 1), ((0, nh_a), (0, 0)))
    dtc = jnp.pad(gdn_dt_bias.reshape(nh_a, 1), ((0, nh_a), (0, 0)))
    convw = gdn_conv_w[0]
    rk = rwkv_r_k.reshape(1, d_b)
    woa = w_out_a[0].astype(BF16)
    wob = w_out_b[0].astype(BF16)
    wo = w_out[0].astype(BF16)
    lnf = lnf_w.reshape(1, d)

    def branches(p, c, ns_a, ns_b, t_valid, s_gdn, conv_tail, s_rwkv, x_tail, wa_tail):
        abt = _token_major_t(p[:, :, lay["ab"][0]:lay["ab"][0] + 2 * nh_a], c)
        oa, sg = _gdn(p, abt, s_gdn, conv_tail, convw, alr, dtr, alc, dtc, gdn_norm_w,
                      c=c, ns=ns_a, t_valid=t_valid, lay=lay)
        ob, sr = _rwkv(p, s_rwkv, x_tail, wa_tail, mu_x, mu_wa, rwkv_w0, rwkv_w2[0], rwkv_a0,
                       rwkv_a2[0], rwkv_k_k, rwkv_k_a, rk, rwkv_gn_w, rwkv_gn_b,
                       c=c, ns=ns_b, t_valid=t_valid, lay=lay)
        return oa, sg, ob, sr

    x0, wx = lay["rkvz"]
    a0_, _ = lay["wa"]

    p_m = _proj(meta_tokens, ln1_w, w_all, apply_norm=True)
    _, sg_m, _, sr_m = branches(
        p_m[None], n_meta, 1, 1, n_meta,
        jnp.zeros((1, nh_a, dk, dv), F32), jnp.zeros((1, SUBLANES, w_qkv), F32),
        jnp.zeros((1, nh_b, n_b, n_b), F32), jnp.zeros((1, SUBLANES, wx), F32),
        jnp.zeros((1, SUBLANES, LANES), F32))

    xp = x_prompt.reshape(bp * seq, d)
    p_p = _proj(xp, ln1_w, w_all, apply_norm=True)
    tail_m = p_m[n_meta - SUBLANES:]
    oa_p, sg_p, ob_p, sr_p = branches(
        p_p.reshape(bp, seq, n_pad), 64, 4, 2, seq, sg_m, tail_m[None, :, :w_qkv], sr_m,
        tail_m[None, :, x0:x0 + wx], tail_m[None, :, a0_:a0_ + LANES])
    y_p = _merge(oa_p.reshape(bp * seq, d), ob_p.reshape(bp * seq, d), p_p, xp, woa, wob, wo, lnf,
                 lay=lay)

    tpad = 16
    xs = jnp.pad(x_sample, ((0, 0), (0, tpad - tseq), (0, 0))).reshape(bs * tpad, d)
    p_s = _proj(xs, ln1_w, w_all, apply_norm=True)
    p_first = _proj(state_shift[0], ln1_w, w_all, apply_norm=False)
    oa_s, sg_s, ob_s, sr_s = branches(
        p_s.reshape(bs, tpad, n_pad), tpad, 8, 8, tseq, state_gdn[0],
        _tail_rows(state_gdn_conv[0]), state_rwkv[0],
        _tail_rows(p_first[:, None, x0:x0 + wx]), _tail_rows(p_first[:, None, a0_:a0_ + LANES]))
    y_s = _merge(oa_s.reshape(bs * tpad, d), ob_s.reshape(bs * tpad, d), p_s, xs, woa, wob, wo,
                 lnf, lay=lay)

    shift_p = _rmsnorm_rows(x_prompt[:, -1], ln1_w)
    shift_s = _rmsnorm_rows(x_sample[:, -1], ln1_w)
    conv_p = p_p.reshape(bp, seq, n_pad)[:, seq - (kw - 1):, :w_qkv]
    conv_s = p_s.reshape(bs, tpad, n_pad)[:, tseq - (kw - 1):tseq, :w_qkv]
    return (y_p.reshape(bp, seq, d), y_s.reshape(bs, tpad, d)[:, :tseq],
            sg_p[None], conv_p[None], sr_p[None], shift_p[None],
            sg_s[None], conv_s[None], sr_s[None], shift_s[None])
```

```python
import functools

import jax
import jax.numpy as jnp
from jax import lax
from jax.experimental import pallas as pl
from jax.experimental.pallas import tpu as pltpu

F32 = jnp.float32
BF16 = jnp.bfloat16

EPS = 1e-6
L2_EPS = 1e-6
GN_EPS_PER_CH = 1e-5

LANES = 128
SUBLANES = 8
VMEM_LIMIT = 48 * 1024 * 1024
PROJ_TM = 1024
PROJ_TN = 1536

NN = (((1,), (0,)), ((), ()))
NT = (((1,), (1,)), ((), ()))
TN = (((0,), (0,)), ((), ()))


def _mm(a, b, dims=NN):
    return lax.dot_general(a.astype(BF16), b.astype(BF16), dims, preferred_element_type=F32)


def _mm_f32(a, b, dims=NN):
    return lax.dot_general(a, b, dims, precision=lax.Precision.HIGHEST,
                           preferred_element_type=F32)


def _sigmoid(x):
    return 1.0 / (1.0 + jnp.exp(-x))


def _silu(x):
    return x * _sigmoid(x)


def _softplus(x):
    return jnp.maximum(x, 0.0) + jnp.log(1.0 + jnp.exp(-jnp.abs(x)))


def _shift_rows(cur, prev, k):
    rolled = pltpu.roll(cur, k, 0)
    fix = pltpu.roll(prev, k, 0)
    wrap = lax.broadcasted_iota(jnp.int32, fix.shape, 0) < k
    head = jnp.where(wrap, fix, rolled[0:SUBLANES])
    if cur.shape[0] == SUBLANES:
        return head
    return jnp.concatenate([head, rolled[SUBLANES:]], axis=0)


def _tri_masks(c):
    row = lax.broadcasted_iota(jnp.int32, (c, c), 0)
    col = lax.broadcasted_iota(jnp.int32, (c, c), 1)
    diff = row ^ col
    levels = [(diff >> 1) == 0]
    sh = 2
    while (1 << sh) <= c:
        levels.append((diff >> (sh - 1)) == 1)
        sh += 1
    return row, col, levels


def _tri_inv_many(lows, levels):
    nn = [-jnp.where(levels[0], low, 0.0) for low in lows]
    for m in levels[1:]:
        off = [jnp.where(m, low, 0.0) for low in lows]
        xc = [o + _mm(x, o) for x, o in zip(nn, off)]
        nn = [x - (y + _mm(y, x)) for x, y in zip(nn, xc)]
    return nn


def _proj_kernel(x_ref, lnw_ref, w_ref, o_ref, h_ref, *, apply_norm):
    @pl.when(pl.program_id(1) == 0)
    def _():
        x = x_ref[...]
        if apply_norm:
            x = x * lax.rsqrt(jnp.mean(x * x, axis=-1, keepdims=True) + EPS)
            x = x * lnw_ref[...]
        h_ref[...] = x.astype(BF16)

    o_ref[...] = lax.dot_general(h_ref[...], w_ref[...], NT, preferred_element_type=F32)


def _proj(x, lnw, w, *, apply_norm):
    m, d = x.shape
    n = w.shape[0]
    tm = min(m, PROJ_TM)
    tn = PROJ_TN
    assert m % tm == 0 and n % tn == 0
    return pl.pallas_call(
        functools.partial(_proj_kernel, apply_norm=apply_norm),
        out_shape=jax.ShapeDtypeStruct((m, n), F32),
        grid=(m // tm, n // tn),
        in_specs=[pl.BlockSpec((tm, d), lambda i, j: (i, 0)),
                  pl.BlockSpec((1, d), lambda i, j: (0, 0)),
                  pl.BlockSpec((tn, d), lambda i, j: (j, 0))],
        out_specs=pl.BlockSpec((tm, tn), lambda i, j: (i, j)),
        scratch_shapes=[pltpu.VMEM((tm, d), BF16)],
        compiler_params=pltpu.CompilerParams(
            dimension_semantics=("parallel", "arbitrary"), vmem_limit_bytes=VMEM_LIMIT),
        name="proj",
    )(x, lnw, w)


def _rmsnorm_kernel(x_ref, w_ref, o_ref):
    x = x_ref[...]
    o_ref[...] = x * lax.rsqrt(jnp.mean(x * x, axis=-1, keepdims=True) + EPS) * w_ref[...]


def _rmsnorm_rows(x, w):
    return pl.pallas_call(
        _rmsnorm_kernel,
        out_shape=jax.ShapeDtypeStruct(x.shape, F32),
        name="rmsnorm_rows",
    )(x, w)


def _gdn_kernel(qkv_ref, z_ref, ab_ref, abt_ref, s0_ref, prev0_ref, convw_ref,
                alr_ref, dtr_ref, alc_ref, dtc_ref, nw_ref,
                o_ref, s_ref, prev_ref, *, c, ns, nh, dk, dv, t_valid, t_total):
    ci = pl.program_id(1)
    shared = s0_ref.shape[0] == 1

    @pl.when(ci == 0)
    def _():
        for j in range(ns):
            s_ref[j] = s0_ref[0 if shared else j]
            prev_ref[j] = prev0_ref[0 if shared else j]

    row, col, levels = _tri_masks(c)
    tril = row >= col
    strict = row > col
    kw = convw_ref.shape[0]
    seqs = range(ns)

    ab = [ab_ref[j] for j in seqs]
    g_c = [-jnp.exp(alr_ref[...]) * _softplus(x + dtr_ref[...]) for x in ab]
    beta_c = [_sigmoid(x) for x in ab]
    abt = [abt_ref[j, 0] for j in seqs]
    g_r = [-jnp.exp(alc_ref[...]) * _softplus(x + dtc_ref[...]) for x in abt]
    if t_valid < t_total:
        tok_c = ci * c + lax.broadcasted_iota(jnp.int32, ab[0].shape, 0)
        tok_r = ci * c + lax.broadcasted_iota(jnp.int32, abt[0].shape, 1)
        g_c = [jnp.where(tok_c < t_valid, x, 0.0) for x in g_c]
        beta_c = [jnp.where(tok_c < t_valid, x, 0.0) for x in beta_c]
        g_r = [jnp.where(tok_r < t_valid, x, 0.0) for x in g_r]
    gc_c = _mm_f32(tril.astype(F32), jnp.concatenate(g_c, axis=1))
    gc_r = _mm_f32(jnp.concatenate(g_r, axis=0), (row <= col).astype(F32))

    def conv(j, col0, width):
        cur = qkv_ref[j, :, col0:col0 + width]
        prev = prev_ref[j, :, col0:col0 + width]
        acc = None
        for k in range(kw - 1, -1, -1):
            src = cur if k == 0 else _shift_rows(cur, prev, k)
            term = src * convw_ref[kw - 1 - k:kw - k, col0:col0 + width]
            acc = term if acc is None else acc + term
        return _silu(acc)

    units = [(j, h) for j in seqs for h in range(nh)]
    us = range(len(units))
    gcc = [gc_c[:, j * LANES + h:j * LANES + h + 1] for j, h in units]
    gcr = [gc_r[j * 2 * nh + h:j * 2 * nh + h + 1, :] for j, h in units]
    beta = [beta_c[j][:, nh + h:nh + h + 1] for j, h in units]
    glast = [x[c - 1:c, :] for x in gcc]
    decay = [jnp.where(tril, jnp.exp(jnp.where(tril, gcc[u] - gcr[u], 0.0)), 0.0) for u in us]
    q = [conv(j, h * dk, dk) for j, h in units]
    k = [conv(j, nh * dk + h * dk, dk) for j, h in units]
    v = [conv(j, 2 * nh * dk + h * dv, dv) for j, h in units]
    q = [x * lax.rsqrt(jnp.sum(x * x, axis=-1, keepdims=True) + L2_EPS) * (dk ** -0.5) for x in q]
    k = [x * lax.rsqrt(jnp.sum(x * x, axis=-1, keepdims=True) + L2_EPS) for x in k]
    kb = [k[u] * beta[u] for u in us]
    vb = [v[u] * beta[u] for u in us]
    kbg = [kb[u] * jnp.exp(gcc[u]) for u in us]
    qg = [q[u] * jnp.exp(gcc[u]) for u in us]
    k_tail = [k[u] * jnp.exp(glast[u] - gcc[u]) for u in us]
    lower = [jnp.where(strict, _mm(kb[u], k[u], NT) * decay[u], 0.0) for u in us]
    qk = [jnp.where(tril, _mm(q[u], k[u], NT) * decay[u], 0.0) for u in us]
    s_old = [s_ref[j, h] for j, h in units]
    qs = [_mm(qg[u], s_old[u]) for u in us]
    nn = _tri_inv_many(lower, levels)
    u_ = [vb[u] + _mm(nn[u], vb[u]) for u in us]
    w_ = [kbg[u] + _mm(nn[u], kbg[u]) for u in us]
    v_new = [u_[u] - _mm(w_[u], s_old[u]) for u in us]
    kv = [_mm(k_tail[u], v_new[u], TN) for u in us]
    for u, (j, h) in enumerate(units):
        s_ref[j, h] = s_old[u] * jnp.exp(glast[u]) + kv[u]
    o = [qs[u] + _mm(qk[u], v_new[u]) for u in us]
    for u, (j, h) in enumerate(units):
        oh = o[u] * lax.rsqrt(jnp.mean(o[u] * o[u], axis=-1, keepdims=True) + EPS) * nw_ref[...]
        oh = oh * _silu(z_ref[j, :, h * dv:(h + 1) * dv])
        o_ref[j, :, h * dv:(h + 1) * dv] = oh.astype(BF16)

    for j in seqs:
        prev_ref[j] = qkv_ref[j, c - SUBLANES:c, :]


def _state_specs(s0, ns, tail_shapes):
    shared = s0.shape[0] == 1
    lead = 1 if shared else ns
    idx4 = (lambda b, i: (0, 0, 0, 0)) if shared else (lambda b, i: (b, 0, 0, 0))
    idx3 = (lambda b, i: (0, 0, 0)) if shared else (lambda b, i: (b, 0, 0))
    return ([pl.BlockSpec((lead,) + s0.shape[1:], idx4)]
            + [pl.BlockSpec((lead,) + ts, idx3) for ts in tail_shapes])


def _gdn(p, abt, s0, prev0, convw, alr, dtr, alc, dtc, nw, *, c, ns, t_valid, lay):
    nb, t, _ = p.shape
    nh, dk, dv = s0.shape[1:]
    wq = lay["qkv"][1]
    const2 = lambda b, i: (0, 0)
    return pl.pallas_call(
        functools.partial(_gdn_kernel, c=c, ns=ns, nh=nh, dk=dk, dv=dv, t_valid=t_valid,
                          t_total=t),
        out_shape=(jax.ShapeDtypeStruct((nb, t, nh * dv), BF16),
                   jax.ShapeDtypeStruct((nb, nh, dk, dv), F32)),
        grid=(nb // ns, t // c),
        in_specs=[
            pl.BlockSpec((ns, c, wq), lambda b, i: (b, i, lay["qkv"][0] // wq)),
            pl.BlockSpec((ns, c, nh * dv), lambda b, i: (b, i, lay["za"][0] // (nh * dv))),
            pl.BlockSpec((ns, c, LANES), lambda b, i: (b, i, lay["ab"][0] // LANES)),
            pl.BlockSpec((ns, 1, 2 * nh, c), lambda b, i: (b, i, 0, 0)),
            *_state_specs(s0, ns, [(SUBLANES, wq)]),
            pl.BlockSpec(convw.shape, const2),
            pl.BlockSpec(alr.shape, const2),
            pl.BlockSpec(dtr.shape, const2),
            pl.BlockSpec(alc.shape, const2),
            pl.BlockSpec(dtc.shape, const2),
            pl.BlockSpec(nw.shape, const2),
        ],
        out_specs=(pl.BlockSpec((ns, c, nh * dv), lambda b, i: (b, i, 0)),
                   pl.BlockSpec((ns, nh, dk, dv), lambda b, i: (b, 0, 0, 0))),
        scratch_shapes=[pltpu.VMEM((ns, SUBLANES, wq), F32)],
        compiler_params=pltpu.CompilerParams(
            dimension_semantics=("parallel", "arbitrary"), vmem_limit_bytes=VMEM_LIMIT),
        name="gdn",
    )(p, p, p, abt, s0, prev0, convw, alr, dtr, alc, dtc, nw)


def _token_shift(x_ref, prev_ref, mu_ref, j, lo, hi):
    cur = x_ref[j, :, lo:hi]
    prv = _shift_rows(cur, prev_ref[j, :, lo:hi], 1)
    return cur + mu_ref[:, lo:hi] * (prv - cur)


_RWKV_OPERANDS = ("kkg", "rg", "bh", "kh", "kt", "bt", "v")


def _rwkv_kernel(x_ref, wa_ref, s0_ref, prev0_ref, prevwa0_ref, mu_ref, muwa_ref,
                 w0_ref, w2_ref, a0_ref, a2_ref, kk_ref, ka_ref, rk_ref, gnw_ref, gnb_ref,
                 o_ref, s_ref, prev_ref, prevwa_ref, sbd_ref,
                 *, c, ns, nh, n, t_valid, t_total):
    step = pl.program_id(1)
    d = nh * n
    npair = nh // 2
    ngrp = d // LANES
    shared = s0_ref.shape[0] == 1
    seqs = range(ns)

    @pl.when(step == 0)
    def _():
        zero = jnp.zeros((n, n), F32)
        for j in seqs:
            jj = 0 if shared else j
            for p in range(npair):
                top = jnp.concatenate([s0_ref[jj, 2 * p], zero], axis=1)
                bot = jnp.concatenate([zero, s0_ref[jj, 2 * p + 1]], axis=1)
                sbd_ref[j, p] = jnp.concatenate([top, bot], axis=0)
            prev_ref[j] = prev0_ref[jj]
            prevwa_ref[j] = prevwa0_ref[jj]

    w2 = 2 * c
    rowi = lax.broadcasted_iota(jnp.int32, (c, w2), 0)
    lane = lax.broadcasted_iota(jnp.int32, (c, w2), 1)
    lcol = lane & (c - 1)
    tril = rowi >= lcol
    strict = rowi > lcol
    diff = rowi ^ lcol
    levels = [(diff >> 1) == 0]
    sh = 2
    while (1 << sh) <= c:
        levels.append((diff >> (sh - 1)) == 1)
        sh += 1
    tok0 = (lane < c).astype(F32).astype(BF16)
    tok1 = (lane >= c).astype(F32).astype(BF16)
    chl = lax.broadcasted_iota(jnp.int32, (c, LANES), 1)
    ch0 = (chl < n).astype(F32).astype(BF16)
    ch1 = (chl >= n).astype(F32).astype(BF16)
    row2 = lax.broadcasted_iota(jnp.int32, (LANES, LANES), 0)
    lane2 = lax.broadcasted_iota(jnp.int32, (LANES, LANES), 1)
    log2n = n.bit_length() - 1
    same_head = (row2 >> log2n) == (lane2 >> log2n)
    seg_ones = same_head.astype(F32).astype(BF16)

    def bd_tok(xb):
        return jnp.concatenate([xb * tok0, xb * tok1], axis=0)

    def bd_ch(xb):
        return jnp.concatenate([xb * ch0, xb * ch1], axis=0)

    def dot(a, b, dims=NN):
        return lax.dot_general(a, b, dims, preferred_element_type=F32)

    def seg_sum(x, pieces):
        stk = jnp.concatenate([x[:, i * LANES:(i + 1) * LANES] for i in range(ngrp)], axis=0)
        hi = stk.astype(BF16)
        acc = dot(hi, seg_ones)
        if pieces == 2:
            acc = acc + dot((stk - hi.astype(F32)).astype(BF16), seg_ones)
        return jnp.concatenate([acc[i * c:(i + 1) * c] for i in range(ngrp)], axis=1)

    lora = w2_ref.shape[0]
    row_c = lax.broadcasted_iota(jnp.int32, (c, c), 0)
    col_c = lax.broadcasted_iota(jnp.int32, (c, c), 1)
    tril_b = (row_c >= col_c).astype(BF16)

    def pre_steps(out):
        r = [_token_shift(x_ref, prev_ref, mu_ref, j, 0, d) for j in seqs]
        kb = [_token_shift(x_ref, prev_ref, mu_ref, j, d, 2 * d) for j in seqs]
        vb = [_token_shift(x_ref, prev_ref, mu_ref, j, 2 * d, 3 * d) for j in seqs]
        sz = [_silu(_token_shift(x_ref, prev_ref, mu_ref, j, 3 * d, 4 * d)) for j in seqs]
        wa = [wa_ref[j] for j in seqs]
        wa = [wa[j] + muwa_ref[...] * (_shift_rows(wa[j], prevwa_ref[j], 1) - wa[j])
              for j in seqs]
        tw = [jnp.tanh(x[:, 0:lora]) for x in wa]
        w_raw = [w0_ref[...] + _mm(tw[j], w2_ref[...]) for j in seqs]
        a = [_sigmoid(a0_ref[...] + _mm(wa[j][:, lora:], a2_ref[...])) for j in seqs]
        w_log = [-jnp.exp(-_softplus(-x) - 0.5) for x in w_raw]
        kku = [kb[j] * kk_ref[...] for j in seqs]
        k2 = [kb[j] * (1.0 + (a[j] - 1.0) * ka_ref[...]) for j in seqs]
        if t_valid < t_total:
            ok = (step * c + lax.broadcasted_iota(jnp.int32, (c, d), 0)) < t_valid
            w_log = [jnp.where(ok, x, 0.0) for x in w_log]
            kku = [jnp.where(ok, x, 0.0) for x in kku]
            vb = [jnp.where(ok, x, 0.0) for x in vb]
        w_hi = [x.astype(BF16) for x in w_log]
        rem = [w_log[j] - w_hi[j].astype(F32) for j in seqs]
        w_mid = [x.astype(BF16) for x in rem]
        w_lo = [(rem[j] - w_mid[j].astype(F32)).astype(BF16) for j in seqs]
        gcum = [dot(tril_b, w_hi[j]) + dot(tril_b, w_mid[j]) + dot(tril_b, w_lo[j]) for j in seqs]
        glast = [x[c - 1:c, :] for x in gcum]
        g_in = [jnp.exp(x) for x in gcum]
        g_ex = [jnp.exp(gcum[j] - w_log[j]) for j in seqs]
        g_inv = [jnp.exp(-x) for x in gcum]
        g_tail = [jnp.exp(glast[j] - gcum[j]) for j in seqs]
        ssq = [x * x for x in kku]
        bonus_in = [r[j] * k2[j] * rk_ref[...] for j in seqs]
        kk = [kku[j] * lax.rsqrt(seg_sum(ssq[j], 1) + L2_EPS) for j in seqs]
        bonus = [seg_sum(bonus_in[j], 1) * vb[j] for j in seqs]
        for j in seqs:
            b = kk[j] * a[j]
            out[j].update(
                kkg=(kk[j] * g_ex[j]).astype(BF16), rg=(r[j] * g_in[j]).astype(BF16),
                bh=(b * g_inv[j]).astype(BF16), kh=(k2[j] * g_inv[j]).astype(BF16),
                kt=(k2[j] * g_tail[j]).astype(BF16), bt=(b * g_tail[j]).astype(BF16),
                v=vb[j].astype(BF16), bonus=bonus[j], sz=sz[j], g_all=jnp.exp(glast[j]))

    units = [(j, p) for j in seqs for p in range(npair)]
    us = range(len(units))

    def stage_steps(ops):
        kkg, rg, bh, kh, kt, bt, v = (
            [ops[j][k][:, p * LANES:(p + 1) * LANES] for j, p in units] for k in _RWKV_OPERANDS)
        g_all = [ops[j]["g_all"][:, p * LANES:(p + 1) * LANES] for j, p in units]
        lhs = [jnp.concatenate([kkg[u], rg[u]], axis=0) for u in us]
        rhs = [jnp.concatenate([bd_ch(bh[u]), bd_ch(kh[u])], axis=0) for u in us]
        big = [dot(lhs[u], rhs[u], NT) for u in us]
        s_old = [sbd_ref[j, p] for j, p in units]
        from_s = [dot(lhs[u], s_old[u].astype(BF16), NT) for u in us]
        low = [jnp.where(strict, x[0:c, 0:w2], 0.0) for x in big]
        m_kv = [jnp.where(strict, x[0:c, w2:2 * w2], 0.0) for x in big]
        q_b = [jnp.where(tril, x[c:2 * c, 0:w2], 0.0) for x in big]
        p_kv = [jnp.where(tril, x[c:2 * c, w2:2 * w2], 0.0) for x in big]
        mp = [dot(jnp.concatenate([m_kv[u], p_kv[u]], axis=0).astype(BF16), bd_ch(v[u]))
              for u in us]
        nn = [-jnp.where(levels[0], x, 0.0) for x in low]
        for m in levels[1:]:
            off = [jnp.where(m, x, 0.0) for x in low]
            nnb = [x.astype(BF16) for x in nn]
            xc = [off[u] + dot(nnb[u], bd_tok(off[u].astype(BF16))) for u in us]
            nn = [nn[u] - (xc[u] + dot(xc[u].astype(BF16), bd_tok(nnb[u]))) for u in us]
        rhs_sa = [from_s[u][0:c] + mp[u][0:c] for u in us]
        sa = [rhs_sa[u] + dot(nn[u].astype(BF16), bd_ch(rhs_sa[u].astype(BF16))) for u in us]
        sab = [x.astype(BF16) for x in sa]
        qs = [dot(q_b[u].astype(BF16), bd_ch(sab[u])) for u in us]
        upd = [dot(jnp.concatenate([v[u], -sab[u]], axis=0),
                   jnp.concatenate([kt[u], bt[u]], axis=0), TN) for u in us]
        s_new = [s_old[u] * g_all[u] + jnp.where(same_head, upd[u], 0.0) for u in us]
        for u, (j, p) in enumerate(units):
            sbd_ref[j, p] = s_new[u]

        @pl.when(step == pl.num_programs(1) - 1)
        def _():
            for u, (j, p) in enumerate(units):
                s_ref[j, 2 * p] = s_new[u][0:n, 0:n]
                s_ref[j, 2 * p + 1] = s_new[u][n:2 * n, n:2 * n]

        y = [from_s[u][c:2 * c] + mp[u][c:2 * c] - qs[u] for u in us]
        yj = [jnp.concatenate(y[j * npair:(j + 1) * npair], axis=1) for j in seqs]
        yc = [yj[j] - seg_sum(yj[j], 2) * (1.0 / n) for j in seqs]
        var = [seg_sum(yc[j] * yc[j], 2) * (1.0 / n) for j in seqs]
        for j in seqs:
            yn = yc[j] * lax.rsqrt(var[j] + n * GN_EPS_PER_CH) * gnw_ref[...] + gnb_ref[...]
            o_ref[j] = ((yn + ops[j]["bonus"]) * ops[j]["sz"]).astype(BF16)

    ops = [dict() for _ in seqs]
    pre_steps(ops)
    stage_steps(ops)

    for j in seqs:
        prev_ref[j] = x_ref[j, c - SUBLANES:c, :]
        prevwa_ref[j] = wa_ref[j, c - SUBLANES:c, :]


def _rwkv(p, s0, prev0, prevwa0, mu, muwa, w0, w2, a0, a2, kk, ka, rk, gnw, gnb,
          *, c, ns, t_valid, lay):
    nb, t, _ = p.shape
    nh, n = s0.shape[1:3]
    d = nh * n
    wx = lay["rkvz"][1]
    const2 = lambda b, i: (0, 0)
    vec = pl.BlockSpec((1, d), const2)
    assert 2 * n == LANES and c <= n and c & (c - 1) == 0, "two heads per lane group"
    scratch = [pltpu.VMEM((ns, SUBLANES, wx), F32), pltpu.VMEM((ns, SUBLANES, LANES), F32),
               pltpu.VMEM((ns, nh // 2, LANES, LANES), F32)]
    return pl.pallas_call(
        functools.partial(_rwkv_kernel, c=c, ns=ns, nh=nh, n=n, t_valid=t_valid, t_total=t),
        out_shape=(jax.ShapeDtypeStruct((nb, t, d), BF16),
                   jax.ShapeDtypeStruct((nb, nh, n, n), F32)),
        grid=(nb // ns, t // c),
        in_specs=[
            pl.BlockSpec((ns, c, wx), lambda b, i: (b, i, lay["rkvz"][0] // wx)),
            pl.BlockSpec((ns, c, LANES), lambda b, i: (b, i, lay["wa"][0] // LANES)),
            *_state_specs(s0, ns, [(SUBLANES, wx), (SUBLANES, LANES)]),
            pl.BlockSpec((1, wx), const2),
            pl.BlockSpec((1, LANES), const2),
            vec,
            pl.BlockSpec(w2.shape, const2),
            vec,
            pl.BlockSpec(a2.shape, const2),
            vec, vec, vec, vec, vec,
        ],
        out_specs=(pl.BlockSpec((ns, c, d), lambda b, i: (b, i, 0)),
                   pl.BlockSpec((ns, nh, n, n), lambda b, i: (b, 0, 0, 0))),
        scratch_shapes=scratch,
        compiler_params=pltpu.CompilerParams(
            dimension_semantics=("parallel", "arbitrary"), vmem_limit_bytes=VMEM_LIMIT),
        name="rwkv",
    )(p, p, s0, prev0, prevwa0, mu, muwa, w0, w2, a0, a2, kk, ka, rk, gnw, gnb)


def _merge_kernel(oa_ref, ob_ref, gate_ref, x_ref, woa_ref, wob_ref, wo_ref, lnf_ref, y_ref):
    d = x_ref.shape[1]
    ba = jnp.dot(oa_ref[...], woa_ref[...], preferred_element_type=F32)
    bb = jnp.dot(ob_ref[...], wob_ref[...], preferred_element_type=F32)
    gates = _sigmoid(gate_ref[...])
    merged = gates[:, :d] * ba + gates[:, d:] * bb
    xn = x_ref[...] + jnp.dot(merged.astype(BF16), wo_ref[...], preferred_element_type=F32)
    y_ref[...] = xn * lax.rsqrt(jnp.mean(xn * xn, axis=-1, keepdims=True) + EPS) * lnf_ref[...]


def _merge(oa, ob, p, x, woa, wob, wo, lnf, *, lay):
    m, d = x.shape
    tm = min(m, 512)
    wg = lay["gate"][1]
    const2 = lambda i: (0, 0)
    return pl.pallas_call(
        _merge_kernel,
        out_shape=jax.ShapeDtypeStruct((m, d), F32),
        grid=(m // tm,),
        in_specs=[
            pl.BlockSpec((tm, oa.shape[1]), lambda i: (i, 0)),
            pl.BlockSpec((tm, ob.shape[1]), lambda i: (i, 0)),
            pl.BlockSpec((tm, wg), lambda i: (i, lay["gate"][0] // wg)),
            pl.BlockSpec((tm, d), lambda i: (i, 0)),
            pl.BlockSpec(woa.shape, const2),
            pl.BlockSpec(wob.shape, const2),
            pl.BlockSpec(wo.shape, const2),
            pl.BlockSpec((1, d), const2),
        ],
        out_specs=pl.BlockSpec((tm, d), lambda i: (i, 0)),
        compiler_params=pltpu.CompilerParams(
            dimension_semantics=("parallel",), vmem_limit_bytes=VMEM_LIMIT),
        name="merge",
    )(oa, ob, p, x, woa, wob, wo, lnf)


def _pad_lanes(v, width):
    return jnp.pad(v, ((0, 0), (0, width - v.shape[1])))


def _tail_rows(rows):
    return jnp.pad(rows, ((0, 0), (SUBLANES - rows.shape[1], 0), (0, 0)))


def _token_major_t(ab, c):
    nb, t, k = ab.shape
    return jnp.transpose(ab.reshape(nb, t // c, c, k), (0, 1, 3, 2))


def kernel(x_prompt, x_sample, state_gdn, state_gdn_conv, state_rwkv, state_shift, meta_tokens,
           ln1_w, w_in, gdn_conv_w, gdn_a_log, gdn_dt_bias, gdn_norm_w, w_out_a, rwkv_mu, rwkv_w0,
           rwkv_w2, rwkv_a0, rwkv_a2, rwkv_k_k, rwkv_k_a, rwkv_r_k, rwkv_gn_w, rwkv_gn_b, w_out_b,
           w_out, lnf_w):
    assert ln1_w.shape[0] == 1, "single-layer trunk"
    bp, seq, d = x_prompt.shape
    bs, tseq, _ = x_sample.shape
    n_meta = meta_tokens.shape[0]
    _, _, nh_a, dk, dv = state_gdn.shape
    _, _, nh_b, n_b, _ = state_rwkv.shape
    kw, w_qkv = gdn_conv_w.shape[1:]
    lora_w = rwkv_w2.shape[1]
    lora_a = rwkv_a2.shape[1]
    d_a = nh_a * dv
    d_b = nh_b * n_b
    assert w_qkv == 2 * nh_a * dk + d_a and d_a == d and d_b == d and lora_w + lora_a == LANES
    assert n_meta % SUBLANES == 0 and n_meta % 16 == 0 and tseq >= kw - 1

    o_a = w_qkv
    o_b = o_a + nh_a
    o_z = o_b + nh_a
    o_r = o_z + d_a
    o_g = o_r + 3 * d_b + lora_w + lora_a + d_b
    w = w_in[0].T
    wr = w[o_r:o_g]
    mu = rwkv_mu
    lora0 = 3 * d_b
    rows = [w[:w_qkv], w[o_z:o_r], wr[:lora0], wr[lora0 + LANES:], w[o_g:],
            wr[lora0:lora0 + LANES], w[o_a:o_z]]
    used = sum(rw.shape[0] for rw in rows)
    n_pad = -(-(used + LANES - 2 * nh_a) // PROJ_TN) * PROJ_TN
    rows.append(jnp.zeros((n_pad - used, d), F32))
    w_all = jnp.concatenate([rw.astype(BF16) for rw in rows], axis=0)
    lay = {"qkv": (0, w_qkv), "za": (w_qkv, d_a), "rkvz": (w_qkv + d_a, 4 * d_b),
           "gate": (w_qkv + d_a + 4 * d_b, 2 * d)}
    lay["wa"] = (lay["gate"][0] + 2 * d, LANES)
    lay["ab"] = (lay["wa"][0] + LANES, LANES)
    for off, width in lay.values():
        assert off % width == 0
    mu_x = jnp.concatenate([mu[:, :lora0], mu[:, lora0 + LANES:]], axis=1)
    mu_wa = mu[:, lora0:lora0 + LANES]

    alr = _pad_lanes(gdn_a_log, LANES)
    dtr = _pad_lanes(gdn_dt_bias, LANES)
    alc = jnp.pad(gdn_a_log.reshape(nh_a, 1), ((0, nh_a), (0, 0)))
    dtc = jnp.pad(gdn_dt_bias.reshape(nh_a, 1), ((0, nh_a), (0, 0)))
    convw = gdn_conv_w[0]
    rk = rwkv_r_k.reshape(1, d_b)
    woa = w_out_a[0].astype(BF16)
    wob = w_out_b[0].astype(BF16)
    wo = w_out[0].astype(BF16)
    lnf = lnf_w.reshape(1, d)

    def branches(p, c, ns_a, ns_b, t_valid, s_gdn, conv_tail, s_rwkv, x_tail, wa_tail):
        abt = _token_major_t(p[:, :, lay["ab"][0]:lay["ab"][0] + 2 * nh_a], c)
        oa, sg = _gdn(p, abt, s_gdn, conv_tail, convw, alr, dtr, alc, dtc, gdn_norm_w,
                      c=c, ns=ns_a, t_valid=t_valid, lay=lay)
        ob, sr = _rwkv(p, s_rwkv, x_tail, wa_tail, mu_x, mu_wa, rwkv_w0, rwkv_w2[0], rwkv_a0,
                       rwkv_a2[0], rwkv_k_k, rwkv_k_a, rk, rwkv_gn_w, rwkv_gn_b,
                       c=c, ns=ns_b, t_valid=t_valid, lay=lay)
        return oa, sg, ob, sr

    x0, wx = lay["rkvz"]
    a0_, _ = lay["wa"]

    p_m = _proj(meta_tokens, ln1_w, w_all, apply_norm=True)
    _, sg_m, _, sr_m = branches(
        p_m[None], n_meta, 1, 1, n_meta,
        jnp.zeros((1, nh_a, dk, dv), F32), jnp.zeros((1, SUBLANES, w_qkv), F32),
        jnp.zeros((1, nh_b, n_b, n_b), F32), jnp.zeros((1, SUBLANES, wx), F32),
        jnp.zeros((1, SUBLANES, LANES), F32))

    xp = x_prompt.reshape(bp * seq, d)
    p_p = _proj(xp, ln1_w, w_all, apply_norm=True)
    tail_m = p_m[n_meta - SUBLANES:]
    oa_p, sg_p, ob_p, sr_p = branches(
        p_p.reshape(bp, seq, n_pad), 64, 4, 2, seq, sg_m, tail_m[None, :, :w_qkv], sr_m,
        tail_m[None, :, x0:x0 + wx], tail_m[None, :, a0_:a0_ + LANES])
    y_p = _merge(oa_p.reshape(bp * seq, d), ob_p.reshape(bp * seq, d), p_p, xp, woa, wob, wo, lnf,
                 lay=lay)

    tpad = 16
    xs = jnp.pad(x_sample, ((0, 0), (0, tpad - tseq), (0, 0))).reshape(bs * tpad, d)
    p_s = _proj(xs, ln1_w, w_all, apply_norm=True)
    p_first = _proj(state_shift[0], ln1_w, w_all, apply_norm=False)
    oa_s, sg_s, ob_s, sr_s = branches(
        p_s.reshape(bs, tpad, n_pad), tpad, 8, 8, tseq, state_gdn[0],
        _tail_rows(state_gdn_conv[0]), state_rwkv[0],
        _tail_rows(p_first[:, None, x0:x0 + wx]), _tail_rows(p_first[:, None, a0_:a0_ + LANES]))
    y_s = _merge(oa_s.reshape(bs * tpad, d), ob_s.reshape(bs * tpad, d), p_s, xs, woa, wob, wo,
                 lnf, lay=lay)

    shift_p = _rmsnorm_rows(x_prompt[:, -1], ln1_w)
    shift_s = _rmsnorm_rows(x_sample[:, -1], ln1_w)
    conv_p = p_p.reshape(bp, seq, n_pad)[:, seq - (kw - 1):, :w_qkv]
    conv_s = p_s.reshape(bs, tpad, n_pad)[:, tseq - (kw - 1):tseq, :w_qkv]
    return (y_p.reshape(bp, seq, d), y_s.reshape(bs, tpad, d)[:, :tseq],
            sg_p[None], conv_p[None], sr_p[None], shift_p[None],
            sg_s[None], conv_s[None], sr_s[None], shift_s[None])
```

```python
import functools

import jax
import jax.numpy as jnp
from jax import lax
from jax.experimental import pallas as pl
from jax.experimental.pallas import tpu as pltpu

F32 = jnp.float32
BF16 = jnp.bfloat16

EPS = 1e-6
L2_EPS = 1e-6
GN_EPS_PER_CH = 1e-5

LANES = 128
SUBLANES = 8
VMEM_LIMIT = 48 * 1024 * 1024
PROJ_TM = 1024
PROJ_TN = 1536

NN = (((1,), (0,)), ((), ()))
NT = (((1,), (1,)), ((), ()))
TN = (((0,), (0,)), ((), ()))


def _mm(a, b, dims=NN):
    return lax.dot_general(a.astype(BF16), b.astype(BF16), dims, preferred_element_type=F32)


def _mm_f32(a, b, dims=NN):
    return lax.dot_general(a, b, dims, precision=lax.Precision.HIGHEST,
                           preferred_element_type=F32)


def _sigmoid(x):
    return 1.0 / (1.0 + jnp.exp(-x))


def _silu(x):
    return x * _sigmoid(x)


def _softplus(x):
    return jnp.maximum(x, 0.0) + jnp.log(1.0 + jnp.exp(-jnp.abs(x)))


def _shift_rows(cur, prev, k):
    rolled = pltpu.roll(cur, k, 0)
    fix = pltpu.roll(prev, k, 0)
    wrap = lax.broadcasted_iota(jnp.int32, fix.shape, 0) < k
    head = jnp.where(wrap, fix, rolled[0:SUBLANES])
    if cur.shape[0] == SUBLANES:
        return head
    return jnp.concatenate([head, rolled[SUBLANES:]], axis=0)


def _tri_masks(c):
    row = lax.broadcasted_iota(jnp.int32, (c, c), 0)
    col = lax.broadcasted_iota(jnp.int32, (c, c), 1)
    diff = row ^ col
    levels = [(diff >> 1) == 0]
    sh = 2
    while (1 << sh) <= c:
        levels.append((diff >> (sh - 1)) == 1)
        sh += 1
    return row, col, levels


def _tri_inv_many(lows, levels):
    nn = [-jnp.where(levels[0], low, 0.0) for low in lows]
    for m in levels[1:]:
        off = [jnp.where(m, low, 0.0) for low in lows]
        xc = [o + _mm(x, o) for x, o in zip(nn, off)]
        nn = [x - (y + _mm(y, x)) for x, y in zip(nn, xc)]
    return nn


def _proj_kernel(x_ref, lnw_ref, w_ref, o_ref, h_ref, *, apply_norm):
    @pl.when(pl.program_id(1) == 0)
    def _():
        x = x_ref[...]
        if apply_norm:
            x = x * lax.rsqrt(jnp.mean(x * x, axis=-1, keepdims=True) + EPS)
            x = x * lnw_ref[...]
        h_ref[...] = x.astype(BF16)

    o_ref[...] = lax.dot_general(h_ref[...], w_ref[...], NT, preferred_element_type=F32)


def _proj(x, lnw, w, *, apply_norm):
    m, d = x.shape
    n = w.shape[0]
    tm = min(m, PROJ_TM)
    tn = PROJ_TN
    assert m % tm == 0 and n % tn == 0
    return pl.pallas_call(
        functools.partial(_proj_kernel, apply_norm=apply_norm),
        out_shape=jax.ShapeDtypeStruct((m, n), F32),
        grid=(m // tm, n // tn),
        in_specs=[pl.BlockSpec((tm, d), lambda i, j: (i, 0)),
                  pl.BlockSpec((1, d), lambda i, j: (0, 0)),
                  pl.BlockSpec((tn, d), lambda i, j: (j, 0))],
        out_specs=pl.BlockSpec((tm, tn), lambda i, j: (i, j)),
        scratch_shapes=[pltpu.VMEM((tm, d), BF16)],
        compiler_params=pltpu.CompilerParams(
            dimension_semantics=("parallel", "arbitrary"), vmem_limit_bytes=VMEM_LIMIT),
        name="proj",
    )(x, lnw, w)


def _rmsnorm_kernel(x_ref, w_ref, o_ref):
    x = x_ref[...]
    o_ref[...] = x * lax.rsqrt(jnp.mean(x * x, axis=-1, keepdims=True) + EPS) * w_ref[...]


def _rmsnorm_rows(x, w):
    return pl.pallas_call(
        _rmsnorm_kernel,
        out_shape=jax.ShapeDtypeStruct(x.shape, F32),
        name="rmsnorm_rows",
    )(x, w)


def _gdn_kernel(qkv_ref, z_ref, ab_ref, abt_ref, s0_ref, prev0_ref, convw_ref,
                alr_ref, dtr_ref, alc_ref, dtc_ref, nw_ref,
                o_ref, s_ref, prev_ref, *, c, ns, nh, dk, dv, t_valid, t_total):
    ci = pl.program_id(1)
    shared = s0_ref.shape[0] == 1

    @pl.when(ci == 0)
    def _():
        for j in range(ns):
            s_ref[j] = s0_ref[0 if shared else j]
            _init_carry(prev_ref, prev0_ref, j, 0 if shared else j)

    row, col, levels = _tri_masks(c)
    tril = row >= col
    strict = row > col
    kw = convw_ref.shape[0]
    seqs = range(ns)

    ab = [ab_ref[j] for j in seqs]
    g_c = [-jnp.exp(alr_ref[...]) * _softplus(x + dtr_ref[...]) for x in ab]
    beta_c = [_sigmoid(x) for x in ab]
    abt = [abt_ref[j, 0] for j in seqs]
    g_r = [-jnp.exp(alc_ref[...]) * _softplus(x + dtc_ref[...]) for x in abt]
    if t_valid < t_total:
        tok_c = ci * c + lax.broadcasted_iota(jnp.int32, ab[0].shape, 0)
        tok_r = ci * c + lax.broadcasted_iota(jnp.int32, abt[0].shape, 1)
        g_c = [jnp.where(tok_c < t_valid, x, 0.0) for x in g_c]
        beta_c = [jnp.where(tok_c < t_valid, x, 0.0) for x in beta_c]
        g_r = [jnp.where(tok_r < t_valid, x, 0.0) for x in g_r]
    gc_c = _mm_f32(tril.astype(F32), jnp.concatenate(g_c, axis=1))
    gc_r = _mm_f32(jnp.concatenate(g_r, axis=0), (row <= col).astype(F32))

    def conv(j, col0, width):
        cur = qkv_ref[j, :, col0:col0 + width]
        prev = prev_ref[j, :, col0:col0 + width]
        acc = None
        for k in range(kw - 1, -1, -1):
            src = cur if k == 0 else _shift_rows(cur, prev, k)
            term = src * convw_ref[kw - 1 - k:kw - k, col0:col0 + width]
            acc = term if acc is None else acc + term
        return _silu(acc)

    units = [(j, h) for j in seqs for h in range(nh)]
    us = range(len(units))
    gcc = [gc_c[:, j * LANES + h:j * LANES + h + 1] for j, h in units]
    gcr = [gc_r[j * 2 * nh + h:j * 2 * nh + h + 1, :] for j, h in units]
    beta = [beta_c[j][:, nh + h:nh + h + 1] for j, h in units]
    glast = [x[c - 1:c, :] for x in gcc]
    decay = [jnp.where(tril, jnp.exp(jnp.where(tril, gcc[u] - gcr[u], 0.0)), 0.0) for u in us]
    q = [conv(j, h * dk, dk) for j, h in units]
    k = [conv(j, nh * dk + h * dk, dk) for j, h in units]
    v = [conv(j, 2 * nh * dk + h * dv, dv) for j, h in units]
    q = [x * lax.rsqrt(jnp.sum(x * x, axis=-1, keepdims=True) + L2_EPS) * (dk ** -0.5) for x in q]
    k = [x * lax.rsqrt(jnp.sum(x * x, axis=-1, keepdims=True) + L2_EPS) for x in k]
    kb = [k[u] * beta[u] for u in us]
    vb = [v[u] * beta[u] for u in us]
    kbg = [kb[u] * jnp.exp(gcc[u]) for u in us]
    qg = [q[u] * jnp.exp(gcc[u]) for u in us]
    k_tail = [k[u] * jnp.exp(glast[u] - gcc[u]) for u in us]
    lower = [jnp.where(strict, _mm(kb[u], k[u], NT) * decay[u], 0.0) for u in us]
    qk = [jnp.where(tril, _mm(q[u], k[u], NT) * decay[u], 0.0) for u in us]
    s_old = [s_ref[j, h] for j, h in units]
    qs = [_mm(qg[u], s_old[u]) for u in us]
    nn = _tri_inv_many(lower, levels)
    u_ = [vb[u] + _mm(nn[u], vb[u]) for u in us]
    w_ = [kbg[u] + _mm(nn[u], kbg[u]) for u in us]
    v_new = [u_[u] - _mm(w_[u], s_old[u]) for u in us]
    kv = [_mm(k_tail[u], v_new[u], TN) for u in us]
    for u, (j, h) in enumerate(units):
        s_ref[j, h] = s_old[u] * jnp.exp(glast[u]) + kv[u]
    o = [qs[u] + _mm(qk[u], v_new[u]) for u in us]
    for u, (j, h) in enumerate(units):
        oh = o[u] * lax.rsqrt(jnp.mean(o[u] * o[u], axis=-1, keepdims=True) + EPS) * nw_ref[...]
        oh = oh * _silu(z_ref[j, :, h * dv:(h + 1) * dv])
        o_ref[j, :, h * dv:(h + 1) * dv] = oh.astype(BF16)

    for j in seqs:
        prev_ref[j] = qkv_ref[j, c - SUBLANES:c, :]


def _state_specs(s0, ns, tails):
    shared = s0.shape[0] == 1
    lead = 1 if shared else ns
    idx4 = (lambda b, i: (0, 0, 0, 0)) if shared else (lambda b, i: (b, 0, 0, 0))
    idx3 = (lambda b, i: (0, 0, 0)) if shared else (lambda b, i: (0, b, 0))
    return ([pl.BlockSpec((lead,) + s0.shape[1:], idx4)]
            + [pl.BlockSpec((tl.shape[0], lead, tl.shape[2]), idx3) for tl in tails])


def _init_carry(prev_ref, tail_ref, j, jj):
    k = tail_ref.shape[0]
    if k < SUBLANES:
        prev_ref[j] = jnp.zeros(prev_ref.shape[1:], prev_ref.dtype)
    for r in range(k):
        row = SUBLANES - k + r
        prev_ref[j, row:row + 1, :] = tail_ref[r, jj:jj + 1, :]


def _gdn_parts(p, abt, s0, prev0, convw, alr, dtr, alc, dtc, nw, *, c, ns, t_valid, lay):
    nb, t, _ = p.shape
    nh, dk, dv = s0.shape[1:]
    wq = lay["qkv"][1]
    const2 = lambda b, i: (0, 0)
    return dict(
        operands=[p, p, p, abt, s0, prev0, convw, alr, dtr, alc, dtc, nw],
        in_specs=[
            pl.BlockSpec((ns, c, wq), lambda b, i: (b, i, lay["qkv"][0] // wq)),
            pl.BlockSpec((ns, c, nh * dv), lambda b, i: (b, i, lay["za"][0] // (nh * dv))),
            pl.BlockSpec((ns, c, LANES), lambda b, i: (b, i, lay["ab"][0] // LANES)),
            pl.BlockSpec((ns, 1, 2 * nh, c), lambda b, i: (b, i, 0, 0)),
            *_state_specs(s0, ns, [prev0]),
            pl.BlockSpec(convw.shape, const2),
            pl.BlockSpec(alr.shape, const2),
            pl.BlockSpec(dtr.shape, const2),
            pl.BlockSpec(alc.shape, const2),
            pl.BlockSpec(dtc.shape, const2),
            pl.BlockSpec(nw.shape, const2),
        ],
        out_shape=[jax.ShapeDtypeStruct((nb, t, nh * dv), BF16),
                   jax.ShapeDtypeStruct((nb, nh, dk, dv), F32)],
        out_specs=[pl.BlockSpec((ns, c, nh * dv), lambda b, i: (b, i, 0)),
                   pl.BlockSpec((ns, nh, dk, dv), lambda b, i: (b, 0, 0, 0))],
        scratch=[pltpu.VMEM((ns, SUBLANES, wq), F32)],
        statics=dict(c=c, ns=ns, nh=nh, dk=dk, dv=dv, t_valid=t_valid, t_total=t))


def _token_shift(x_ref, prev_ref, mu_ref, j, lo, hi):
    cur = x_ref[j, :, lo:hi]
    prv = _shift_rows(cur, prev_ref[j, :, lo:hi], 1)
    return cur + mu_ref[:, lo:hi] * (prv - cur)


_RWKV_OPERANDS = ("kkg", "rg", "bh", "kh", "kt", "bt", "v")


def _rwkv_kernel(x_ref, wa_ref, s0_ref, prev0_ref, prevwa0_ref, mu_ref, muwa_ref,
                 w0_ref, w2_ref, a0_ref, a2_ref, kk_ref, ka_ref, rk_ref, gnw_ref, gnb_ref,
                 o_ref, s_ref, prev_ref, prevwa_ref, sbd_ref,
                 *, c, ns, nh, n, t_valid, t_total):
    step = pl.program_id(1)
    d = nh * n
    npair = nh // 2
    ngrp = d // LANES
    shared = s0_ref.shape[0] == 1
    seqs = range(ns)

    @pl.when(step == 0)
    def _():
        zero = jnp.zeros((n, n), F32)
        for j in seqs:
            jj = 0 if shared else j
            for p in range(npair):
                top = jnp.concatenate([s0_ref[jj, 2 * p], zero], axis=1)
                bot = jnp.concatenate([zero, s0_ref[jj, 2 * p + 1]], axis=1)
                sbd_ref[j, p] = jnp.concatenate([top, bot], axis=0)
            _init_carry(prev_ref, prev0_ref, j, jj)
            _init_carry(prevwa_ref, prevwa0_ref, j, jj)

    w2 = 2 * c
    rowi = lax.broadcasted_iota(jnp.int32, (c, w2), 0)
    lane = lax.broadcasted_iota(jnp.int32, (c, w2), 1)
    lcol = lane & (c - 1)
    tril = rowi >= lcol
    strict = rowi > lcol
    diff = rowi ^ lcol
    levels = [(diff >> 1) == 0]
    sh = 2
    while (1 << sh) <= c:
        levels.append((diff >> (sh - 1)) == 1)
        sh += 1
    tok0 = (lane < c).astype(F32).astype(BF16)
    tok1 = (lane >= c).astype(F32).astype(BF16)
    chl = lax.broadcasted_iota(jnp.int32, (c, LANES), 1)
    ch0 = (chl < n).astype(F32).astype(BF16)
    ch1 = (chl >= n).astype(F32).astype(BF16)
    row2 = lax.broadcasted_iota(jnp.int32, (LANES, LANES), 0)
    lane2 = lax.broadcasted_iota(jnp.int32, (LANES, LANES), 1)
    log2n = n.bit_length() - 1
    same_head = (row2 >> log2n) == (lane2 >> log2n)
    seg_ones = same_head.astype(F32).astype(BF16)

    def bd_tok(xb):
        return jnp.concatenate([xb * tok0, xb * tok1], axis=0)

    def bd_ch(xb):
        return jnp.concatenate([xb * ch0, xb * ch1], axis=0)

    def dot(a, b, dims=NN):
        return lax.dot_general(a, b, dims, preferred_element_type=F32)

    def seg_sum(x):
        stk = jnp.concatenate([x[:, i * LANES:(i + 1) * LANES] for i in range(ngrp)], axis=0)
        acc = dot(stk.astype(BF16), seg_ones)
        return jnp.concatenate([acc[i * c:(i + 1) * c] for i in range(ngrp)], axis=1)

    lora = w2_ref.shape[0]
    row_c = lax.broadcasted_iota(jnp.int32, (c, c), 0)
    col_c = lax.broadcasted_iota(jnp.int32, (c, c), 1)
    tril_b = (row_c >= col_c).astype(BF16)

    def pre_steps(out):
        r = [_token_shift(x_ref, prev_ref, mu_ref, j, 0, d) for j in seqs]
        kb = [_token_shift(x_ref, prev_ref, mu_ref, j, d, 2 * d) for j in seqs]
        vb = [_token_shift(x_ref, prev_ref, mu_ref, j, 2 * d, 3 * d) for j in seqs]
        sz = [_silu(_token_shift(x_ref, prev_ref, mu_ref, j, 3 * d, 4 * d)) for j in seqs]
        wa = [wa_ref[j] for j in seqs]
        wa = [wa[j] + muwa_ref[...] * (_shift_rows(wa[j], prevwa_ref[j], 1) - wa[j])
              for j in seqs]
        tw = [jnp.tanh(x[:, 0:lora]) for x in wa]
        w_raw = [w0_ref[...] + _mm(tw[j], w2_ref[...]) for j in seqs]
        a = [_sigmoid(a0_ref[...] + _mm(wa[j][:, lora:], a2_ref[...])) for j in seqs]
        w_log = [-jnp.exp(-_softplus(-x) - 0.5) for x in w_raw]
        kku = [kb[j] * kk_ref[...] for j in seqs]
        k2 = [kb[j] * (1.0 + (a[j] - 1.0) * ka_ref[...]) for j in seqs]
        if t_valid < t_total:
            ok = (step * c + lax.broadcasted_iota(jnp.int32, (c, d), 0)) < t_valid
            w_log = [jnp.where(ok, x, 0.0) for x in w_log]
            kku = [jnp.where(ok, x, 0.0) for x in kku]
            vb = [jnp.where(ok, x, 0.0) for x in vb]
        w_hi = [x.astype(BF16) for x in w_log]
        rem = [w_log[j] - w_hi[j].astype(F32) for j in seqs]
        w_mid = [x.astype(BF16) for x in rem]
        w_lo = [(rem[j] - w_mid[j].astype(F32)).astype(BF16) for j in seqs]
        gcum = [dot(tril_b, w_hi[j]) + dot(tril_b, w_mid[j]) + dot(tril_b, w_lo[j]) for j in seqs]
        glast = [x[c - 1:c, :] for x in gcum]
        g_in = [jnp.exp(x) for x in gcum]
        g_ex = [jnp.exp(gcum[j] - w_log[j]) for j in seqs]
        g_inv = [jnp.exp(-x) for x in gcum]
        g_tail = [jnp.exp(glast[j] - gcum[j]) for j in seqs]
        ssq = [x * x for x in kku]
        bonus_in = [r[j] * k2[j] * rk_ref[...] for j in seqs]
        kk = [kku[j] * lax.rsqrt(seg_sum(ssq[j]) + L2_EPS) for j in seqs]
        bonus = [seg_sum(bonus_in[j]) * vb[j] for j in seqs]
        for j in seqs:
            b = kk[j] * a[j]
            out[j].update(
                kkg=(kk[j] * g_ex[j]).astype(BF16), rg=(r[j] * g_in[j]).astype(BF16),
                bh=(b * g_inv[j]).astype(BF16), kh=(k2[j] * g_inv[j]).astype(BF16),
                kt=(k2[j] * g_tail[j]).astype(BF16), bt=(b * g_tail[j]).astype(BF16),
                v=vb[j].astype(BF16), bonus=bonus[j], sz=sz[j], g_all=jnp.exp(glast[j]))

    units = [(j, p) for j in seqs for p in range(npair)]
    us = range(len(units))

    def stage_steps(ops):
        kkg, rg, bh, kh, kt, bt, v = (
            [ops[j][k][:, p * LANES:(p + 1) * LANES] for j, p in units] for k in _RWKV_OPERANDS)
        g_all = [ops[j]["g_all"][:, p * LANES:(p + 1) * LANES] for j, p in units]
        lhs = [jnp.concatenate([kkg[u], rg[u]], axis=0) for u in us]
        rhs = [jnp.concatenate([bd_ch(bh[u]), bd_ch(kh[u])], axis=0) for u in us]
        big = [dot(lhs[u], rhs[u], NT) for u in us]
        s_old = [sbd_ref[j, p] for j, p in units]
        from_s = [dot(lhs[u], s_old[u].astype(BF16), NT) for u in us]
        low = [jnp.where(strict, x[0:c, 0:w2], 0.0) for x in big]
        m_kv = [jnp.where(strict, x[0:c, w2:2 * w2], 0.0) for x in big]
        q_b = [jnp.where(tril, x[c:2 * c, 0:w2], 0.0) for x in big]
        p_kv = [jnp.where(tril, x[c:2 * c, w2:2 * w2], 0.0) for x in big]
        mp = [dot(jnp.concatenate([m_kv[u], p_kv[u]], axis=0).astype(BF16), bd_ch(v[u]))
              for u in us]
        nn = [-jnp.where(levels[0], x, 0.0) for x in low]
        for m in levels[1:]:
            off = [jnp.where(m, x, 0.0) for x in low]
            nnb = [x.astype(BF16) for x in nn]
            xc = [off[u] + dot(nnb[u], bd_tok(off[u].astype(BF16))) for u in us]
            nn = [nn[u] - (xc[u] + dot(xc[u].astype(BF16), bd_tok(nnb[u]))) for u in us]
        rhs_sa = [from_s[u][0:c] + mp[u][0:c] for u in us]
        sa = [rhs_sa[u] + dot(nn[u].astype(BF16), bd_ch(rhs_sa[u].astype(BF16))) for u in us]
        sab = [x.astype(BF16) for x in sa]
        qs = [dot(q_b[u].astype(BF16), bd_ch(sab[u])) for u in us]
        upd = [dot(jnp.concatenate([v[u], -sab[u]], axis=0),
                   jnp.concatenate([kt[u], bt[u]], axis=0), TN) for u in us]
        s_new = [s_old[u] * g_all[u] + jnp.where(same_head, upd[u], 0.0) for u in us]
        for u, (j, p) in enumerate(units):
            sbd_ref[j, p] = s_new[u]

        @pl.when(step == pl.num_programs(1) - 1)
        def _():
            for u, (j, p) in enumerate(units):
                s_ref[j, 2 * p] = s_new[u][0:n, 0:n]
                s_ref[j, 2 * p + 1] = s_new[u][n:2 * n, n:2 * n]

        y = [from_s[u][c:2 * c] + mp[u][c:2 * c] - qs[u] for u in us]
        yj = [jnp.concatenate(y[j * npair:(j + 1) * npair], axis=1) for j in seqs]
        yc = [yj[j] - seg_sum(yj[j]) * (1.0 / n) for j in seqs]
        var = [seg_sum(yc[j] * yc[j]) * (1.0 / n) for j in seqs]
        for j in seqs:
            yn = yc[j] * lax.rsqrt(var[j] + n * GN_EPS_PER_CH) * gnw_ref[...] + gnb_ref[...]
            o_ref[j] = ((yn + ops[j]["bonus"]) * ops[j]["sz"]).astype(BF16)

    ops = [dict() for _ in seqs]
    pre_steps(ops)
    stage_steps(ops)

    for j in seqs:
        prev_ref[j] = x_ref[j, c - SUBLANES:c, :]
        prevwa_ref[j] = wa_ref[j, c - SUBLANES:c, :]


def _rwkv_parts(p, s0, prev0, prevwa0, mu, muwa, w0, w2, a0, a2, kk, ka, rk, gnw, gnb,
                *, c, ns, t_valid, lay):
    nb, t, _ = p.shape
    nh, n = s0.shape[1:3]
    d = nh * n
    wx = lay["rkvz"][1]
    const2 = lambda b, i: (0, 0)
    vec = pl.BlockSpec((1, d), const2)
    assert 2 * n == LANES and c <= n and c & (c - 1) == 0, "two heads per lane group"
    return dict(
        operands=[p, p, s0, prev0, prevwa0, mu, muwa, w0, w2, a0, a2, kk, ka, rk, gnw, gnb],
        in_specs=[
            pl.BlockSpec((ns, c, wx), lambda b, i: (b, i, lay["rkvz"][0] // wx)),
            pl.BlockSpec((ns, c, LANES), lambda b, i: (b, i, lay["wa"][0] // LANES)),
            *_state_specs(s0, ns, [prev0, prevwa0]),
            pl.BlockSpec((1, wx), const2),
            pl.BlockSpec((1, LANES), const2),
            vec,
            pl.BlockSpec(w2.shape, const2),
            vec,
            pl.BlockSpec(a2.shape, const2),
            vec, vec, vec, vec, vec,
        ],
        out_shape=[jax.ShapeDtypeStruct((nb, t, d), BF16),
                   jax.ShapeDtypeStruct((nb, nh, n, n), F32)],
        out_specs=[pl.BlockSpec((ns, c, d), lambda b, i: (b, i, 0)),
                   pl.BlockSpec((ns, nh, n, n), lambda b, i: (b, 0, 0, 0))],
        scratch=[pltpu.VMEM((ns, SUBLANES, wx), F32), pltpu.VMEM((ns, SUBLANES, LANES), F32),
                 pltpu.VMEM((ns, nh // 2, LANES, LANES), F32)],
        statics=dict(c=c, ns=ns, nh=nh, n=n, t_valid=t_valid, t_total=t))


def _recurrence(body, parts, *, name):
    statics = parts["statics"]
    nb, t = parts["out_shape"][0].shape[:2]
    return pl.pallas_call(
        functools.partial(body, **statics),
        out_shape=tuple(parts["out_shape"]),
        grid=(nb // statics["ns"], t // statics["c"]),
        in_specs=parts["in_specs"],
        out_specs=tuple(parts["out_specs"]),
        scratch_shapes=parts["scratch"],
        compiler_params=pltpu.CompilerParams(
            dimension_semantics=("parallel", "arbitrary"), vmem_limit_bytes=VMEM_LIMIT),
        name=name,
    )(*parts["operands"])


def _merge_kernel(oa_ref, ob_ref, gate_ref, x_ref, woa_ref, wob_ref, wo_ref, lnf_ref, y_ref):
    d = x_ref.shape[1]
    ba = jnp.dot(oa_ref[...], woa_ref[...], preferred_element_type=F32)
    bb = jnp.dot(ob_ref[...], wob_ref[...], preferred_element_type=F32)
    gates = _sigmoid(gate_ref[...])
    merged = gates[:, :d] * ba + gates[:, d:] * bb
    xn = x_ref[...] + jnp.dot(merged.astype(BF16), wo_ref[...], preferred_element_type=F32)
    y_ref[...] = xn * lax.rsqrt(jnp.mean(xn * xn, axis=-1, keepdims=True) + EPS) * lnf_ref[...]


def _merge(oa, ob, p, x, woa, wob, wo, lnf, *, lay):
    m, d = x.shape
    tm = min(m, 512)
    wg = lay["gate"][1]
    const2 = lambda i: (0, 0)
    return pl.pallas_call(
        _merge_kernel,
        out_shape=jax.ShapeDtypeStruct((m, d), F32),
        grid=(m // tm,),
        in_specs=[
            pl.BlockSpec((tm, oa.shape[1]), lambda i: (i, 0)),
            pl.BlockSpec((tm, ob.shape[1]), lambda i: (i, 0)),
            pl.BlockSpec((tm, wg), lambda i: (i, lay["gate"][0] // wg)),
            pl.BlockSpec((tm, d), lambda i: (i, 0)),
            pl.BlockSpec(woa.shape, const2),
            pl.BlockSpec(wob.shape, const2),
            pl.BlockSpec(wo.shape, const2),
            pl.BlockSpec((1, d), const2),
        ],
        out_specs=pl.BlockSpec((tm, d), lambda i: (i, 0)),
        compiler_params=pltpu.CompilerParams(
            dimension_semantics=("parallel",), vmem_limit_bytes=VMEM_LIMIT),
        name="merge",
    )(oa, ob, p, x, woa, wob, wo, lnf)


def _pad_lanes(v, width):
    return jnp.pad(v, ((0, 0), (0, width - v.shape[1])))


def _transpose_kernel(x_ref, o_ref):
    o_ref[...] = x_ref[...].T


def _rows_to_lanes(x, block):
    r, l = x.shape
    assert r % block == 0
    return pl.pallas_call(
        _transpose_kernel,
        out_shape=jax.ShapeDtypeStruct((l, r), x.dtype),
        grid=(r // block,),
        in_specs=[pl.BlockSpec((block, l), lambda i: (i, 0))],
        out_specs=pl.BlockSpec((l, block), lambda i: (0, i)),
        compiler_params=pltpu.CompilerParams(dimension_semantics=("parallel",)),
        name="rows_to_lanes",
    )(x)


def _token_major_t(ab, c):
    nb, t, k = ab.shape
    return jnp.transpose(ab.reshape(nb, t // c, c, k), (0, 1, 3, 2))


def kernel(x_prompt, x_sample, state_gdn, state_gdn_conv, state_rwkv, state_shift, meta_tokens,
           ln1_w, w_in, gdn_conv_w, gdn_a_log, gdn_dt_bias, gdn_norm_w, w_out_a, rwkv_mu, rwkv_w0,
           rwkv_w2, rwkv_a0, rwkv_a2, rwkv_k_k, rwkv_k_a, rwkv_r_k, rwkv_gn_w, rwkv_gn_b, w_out_b,
           w_out, lnf_w):
    assert ln1_w.shape[0] == 1, "single-layer trunk"
    bp, seq, d = x_prompt.shape
    bs, tseq, _ = x_sample.shape
    n_meta = meta_tokens.shape[0]
    _, _, nh_a, dk, dv = state_gdn.shape
    _, _, nh_b, n_b, _ = state_rwkv.shape
    kw, w_qkv = gdn_conv_w.shape[1:]
    lora_w = rwkv_w2.shape[1]
    lora_a = rwkv_a2.shape[1]
    d_a = nh_a * dv
    d_b = nh_b * n_b
    assert w_qkv == 2 * nh_a * dk + d_a and d_a == d and d_b == d and lora_w + lora_a == LANES
    assert n_meta % SUBLANES == 0 and n_meta % 16 == 0 and tseq >= kw - 1

    o_a = w_qkv
    o_b = o_a + nh_a
    o_z = o_b + nh_a
    o_r = o_z + d_a
    o_g = o_r + 3 * d_b + lora_w + lora_a + d_b
    w = w_in[0].T
    wr = w[o_r:o_g]
    mu = rwkv_mu
    lora0 = 3 * d_b
    rows = [w[:w_qkv], w[o_z:o_r], wr[:lora0], wr[lora0 + LANES:], w[o_g:],
            wr[lora0:lora0 + LANES], w[o_a:o_z]]
    used = sum(rw.shape[0] for rw in rows)
    n_pad = -(-(used + LANES - 2 * nh_a) // PROJ_TN) * PROJ_TN
    rows.append(jnp.zeros((n_pad - used, d), F32))
    w_all = jnp.concatenate([rw.astype(BF16) for rw in rows], axis=0)
    lay = {"qkv": (0, w_qkv), "za": (w_qkv, d_a), "rkvz": (w_qkv + d_a, 4 * d_b),
           "gate": (w_qkv + d_a + 4 * d_b, 2 * d)}
    lay["wa"] = (lay["gate"][0] + 2 * d, LANES)
    lay["ab"] = (lay["wa"][0] + LANES, LANES)
    for off, width in lay.values():
        assert off % width == 0
    mu_x = jnp.concatenate([mu[:, :lora0], mu[:, lora0 + LANES:]], axis=1)
    mu_wa = mu[:, lora0:lora0 + LANES]

    alr = _pad_lanes(gdn_a_log, LANES)
    dtr = _pad_lanes(gdn_dt_bias, LANES)
    alc = jnp.pad(gdn_a_log.reshape(nh_a, 1), ((0, nh_a), (0, 0)))
    dtc = jnp.pad(gdn_dt_bias.reshape(nh_a, 1), ((0, nh_a), (0, 0)))
    convw = gdn_conv_w[0]
    rk = rwkv_r_k.reshape(1, d_b)
    woa = w_out_a[0].astype(BF16)
    wob = w_out_b[0].astype(BF16)
    wo = w_out[0].astype(BF16)
    lnf = lnf_w.reshape(1, d)

    def branches(p, c, ns_a, ns_b, t_valid, s_gdn, conv_tail, s_rwkv, x_tail, wa_tail):
        abt = _token_major_t(p[:, :, lay["ab"][0]:lay["ab"][0] + 2 * nh_a], c)
        oa, sg = _recurrence(_gdn_kernel, _gdn_parts(
            p, abt, s_gdn, conv_tail, convw, alr, dtr, alc, dtc, gdn_norm_w,
            c=c, ns=ns_a, t_valid=t_valid, lay=lay), name="gdn")
        ob, sr = _recurrence(_rwkv_kernel, _rwkv_parts(
            p, s_rwkv, x_tail, wa_tail, mu_x, mu_wa, rwkv_w0, rwkv_w2[0], rwkv_a0, rwkv_a2[0],
            rwkv_k_k, rwkv_k_a, rk, rwkv_gn_w, rwkv_gn_b,
            c=c, ns=ns_b, t_valid=t_valid, lay=lay), name="rwkv")
        return oa, sg, ob, sr

    x0, wx = lay["rkvz"]
    a0_, _ = lay["wa"]

    p_m = _proj(meta_tokens, ln1_w, w_all, apply_norm=True)
    _, sg_m, _, sr_m = branches(
        p_m[None], n_meta, 1, 1, n_meta,
        jnp.zeros((1, nh_a, dk, dv), F32), jnp.zeros((1, 1, w_qkv), F32),
        jnp.zeros((1, nh_b, n_b, n_b), F32), jnp.zeros((1, 1, wx), F32),
        jnp.zeros((1, 1, LANES), F32))

    xp = x_prompt.reshape(bp * seq, d)
    p_p = _proj(xp, ln1_w, w_all, apply_norm=True)
    last_m = p_m[n_meta - 1:, None]
    oa_p, sg_p, ob_p, sr_p = branches(
        p_p.reshape(bp, seq, n_pad), 64, 4, 2, seq, sg_m,
        p_m[n_meta - (kw - 1):, None, :w_qkv], sr_m,
        last_m[:, :, x0:x0 + wx], last_m[:, :, a0_:a0_ + LANES])
    y_p = _merge(oa_p.reshape(bp * seq, d), ob_p.reshape(bp * seq, d), p_p, xp, woa, wob, wo, lnf,
                 lay=lay)

    tpad = 16
    xs = jnp.pad(x_sample, ((0, 0), (0, tpad - tseq), (0, 0))).reshape(bs * tpad, d)
    p_s = _proj(xs, ln1_w, w_all, apply_norm=True)
    p_first = _proj(state_shift[0], ln1_w, w_all, apply_norm=False)[None]
    s_rwkv = _rows_to_lanes(jnp.transpose(state_rwkv[0], (1, 2, 3, 0)).reshape(-1, bs), 4096)
    oa_s, sg_s, ob_s, sr_s = branches(
        p_s.reshape(bs, tpad, n_pad), tpad, 8, 8, tseq, state_gdn[0],
        jnp.transpose(state_gdn_conv[0], (1, 0, 2)), s_rwkv.reshape(bs, nh_b, n_b, n_b),
        p_first[:, :, x0:x0 + wx], p_first[:, :, a0_:a0_ + LANES])
    y_s = _merge(oa_s.reshape(bs * tpad, d), ob_s.reshape(bs * tpad, d), p_s, xs, woa, wob, wo,
                 lnf, lay=lay)

    shift_p = _rmsnorm_rows(x_prompt[:, -1], ln1_w)
    shift_s = _rmsnorm_rows(x_sample[:, -1], ln1_w)
    conv_p = p_p.reshape(bp, seq, n_pad)[:, seq - (kw - 1):, :w_qkv]
    conv_s = p_s.reshape(bs, tpad, n_pad)[:, tseq - (kw - 1):tseq, :w_qkv]
    return (y_p.reshape(bp, seq, d), y_s.reshape(bs, tpad, d)[:, :tseq],
            sg_p[None], conv_p[None], sr_p[None], shift_p[None],
            sg_s[None], conv_s[None], sr_s[None], shift_s[None])
```

```python
import functools

import jax
import jax.numpy as jnp
from jax import lax
from jax.experimental import pallas as pl
from jax.experimental.pallas import tpu as pltpu

F32 = jnp.float32
BF16 = jnp.bfloat16

EPS = 1e-6
L2_EPS = 1e-6
GN_EPS_PER_CH = 1e-5

LANES = 128
SUBLANES = 8
VMEM_LIMIT = 48 * 1024 * 1024
PROJ_TM = 1024
PROJ_TN = 1536

NN = (((1,), (0,)), ((), ()))
NT = (((1,), (1,)), ((), ()))
TN = (((0,), (0,)), ((), ()))


def _mm(a, b, dims=NN):
    return lax.dot_general(a.astype(BF16), b.astype(BF16), dims, preferred_element_type=F32)


def _mm_f32(a, b, dims=NN):
    return lax.dot_general(a, b, dims, precision=lax.Precision.HIGHEST,
                           preferred_element_type=F32)


def _sigmoid(x):
    return 1.0 / (1.0 + jnp.exp(-x))


def _silu(x):
    return x * _sigmoid(x)


def _softplus(x):
    return jnp.maximum(x, 0.0) + jnp.log(1.0 + jnp.exp(-jnp.abs(x)))


def _shift_rows(cur, prev, k):
    rolled = pltpu.roll(cur, k, 0)
    fix = pltpu.roll(prev, k, 0)
    wrap = lax.broadcasted_iota(jnp.int32, fix.shape, 0) < k
    head = jnp.where(wrap, fix, rolled[0:SUBLANES])
    if cur.shape[0] == SUBLANES:
        return head
    return jnp.concatenate([head, rolled[SUBLANES:]], axis=0)


def _tri_masks(c):
    row = lax.broadcasted_iota(jnp.int32, (c, c), 0)
    col = lax.broadcasted_iota(jnp.int32, (c, c), 1)
    diff = row ^ col
    levels = [(diff >> 1) == 0]
    sh = 2
    while (1 << sh) <= c:
        levels.append((diff >> (sh - 1)) == 1)
        sh += 1
    return row, col, levels


def _tri_inv_many(lows, levels):
    nn = [-jnp.where(levels[0], low, 0.0) for low in lows]
    for m in levels[1:]:
        off = [jnp.where(m, low, 0.0) for low in lows]
        xc = [o + _mm(x, o) for x, o in zip(nn, off)]
        nn = [x - (y + _mm(y, x)) for x, y in zip(nn, xc)]
    return nn


def _proj_kernel(x_ref, lnw_ref, w_ref, o_ref, h_ref, *, apply_norm):
    @pl.when(pl.program_id(1) == 0)
    def _():
        x = x_ref[...]
        if apply_norm:
            x = x * lax.rsqrt(jnp.mean(x * x, axis=-1, keepdims=True) + EPS)
            x = x * lnw_ref[...]
        h_ref[...] = x.astype(BF16)

    o_ref[...] = lax.dot_general(h_ref[...], w_ref[...], NT, preferred_element_type=F32)


def _proj(x, lnw, w, *, apply_norm):
    m, d = x.shape
    n = w.shape[0]
    tm = min(m, PROJ_TM)
    tn = PROJ_TN
    assert m % tm == 0 and n % tn == 0
    return pl.pallas_call(
        functools.partial(_proj_kernel, apply_norm=apply_norm),
        out_shape=jax.ShapeDtypeStruct((m, n), F32),
        grid=(m // tm, n // tn),
        in_specs=[pl.BlockSpec((tm, d), lambda i, j: (i, 0)),
                  pl.BlockSpec((1, d), lambda i, j: (0, 0)),
                  pl.BlockSpec((tn, d), lambda i, j: (j, 0))],
        out_specs=pl.BlockSpec((tm, tn), lambda i, j: (i, j)),
        scratch_shapes=[pltpu.VMEM((tm, d), BF16)],
        compiler_params=pltpu.CompilerParams(
            dimension_semantics=("parallel", "arbitrary"), vmem_limit_bytes=VMEM_LIMIT),
        name="proj",
    )(x, lnw, w)


def _rmsnorm_kernel(x_ref, w_ref, o_ref):
    x = x_ref[...]
    o_ref[...] = x * lax.rsqrt(jnp.mean(x * x, axis=-1, keepdims=True) + EPS) * w_ref[...]


def _rmsnorm_rows(x, w):
    return pl.pallas_call(
        _rmsnorm_kernel,
        out_shape=jax.ShapeDtypeStruct(x.shape, F32),
        name="rmsnorm_rows",
    )(x, w)


def _gdn_kernel(qkv_ref, z_ref, ab_ref, abt_ref, s0_ref, prev0_ref, convw_ref,
                alr_ref, dtr_ref, alc_ref, dtc_ref, nw_ref,
                o_ref, s_ref, prev_ref, *, c, ns, nh, dk, dv, t_valid, t_total):
    ci = pl.program_id(1)
    shared = s0_ref.shape[0] == 1

    @pl.when(ci == 0)
    def _():
        for j in range(ns):
            s_ref[j] = s0_ref[0 if shared else j]
            _init_carry(prev_ref, prev0_ref, j, 0 if shared else j)

    row, col, levels = _tri_masks(c)
    tril = row >= col
    strict = row > col
    kw = convw_ref.shape[0]
    seqs = range(ns)

    ab = [ab_ref[j] for j in seqs]
    g_c = [-jnp.exp(alr_ref[...]) * _softplus(x + dtr_ref[...]) for x in ab]
    beta_c = [_sigmoid(x) for x in ab]
    abt = [abt_ref[j, 0] for j in seqs]
    g_r = [-jnp.exp(alc_ref[...]) * _softplus(x + dtc_ref[...]) for x in abt]
    if t_valid < t_total:
        tok_c = ci * c + lax.broadcasted_iota(jnp.int32, ab[0].shape, 0)
        tok_r = ci * c + lax.broadcasted_iota(jnp.int32, abt[0].shape, 1)
        g_c = [jnp.where(tok_c < t_valid, x, 0.0) for x in g_c]
        beta_c = [jnp.where(tok_c < t_valid, x, 0.0) for x in beta_c]
        g_r = [jnp.where(tok_r < t_valid, x, 0.0) for x in g_r]
    gc_c = _mm_f32(tril.astype(F32), jnp.concatenate(g_c, axis=1))
    gc_r = _mm_f32(jnp.concatenate(g_r, axis=0), (row <= col).astype(F32))

    def conv(j, col0, width):
        cur = qkv_ref[j, :, col0:col0 + width]
        prev = prev_ref[j, :, col0:col0 + width]
        acc = None
        for k in range(kw - 1, -1, -1):
            src = cur if k == 0 else _shift_rows(cur, prev, k)
            term = src * convw_ref[kw - 1 - k:kw - k, col0:col0 + width]
            acc = term if acc is None else acc + term
        return _silu(acc)

    units = [(j, h) for j in seqs for h in range(nh)]
    us = range(len(units))
    gcc = [gc_c[:, j * LANES + h:j * LANES + h + 1] for j, h in units]
    gcr = [gc_r[j * 2 * nh + h:j * 2 * nh + h + 1, :] for j, h in units]
    beta = [beta_c[j][:, nh + h:nh + h + 1] for j, h in units]
    glast = [x[c - 1:c, :] for x in gcc]
    decay = [jnp.where(tril, jnp.exp(jnp.where(tril, gcc[u] - gcr[u], 0.0)), 0.0) for u in us]
    q = [conv(j, h * dk, dk) for j, h in units]
    k = [conv(j, nh * dk + h * dk, dk) for j, h in units]
    v = [conv(j, 2 * nh * dk + h * dv, dv) for j, h in units]
    q = [x * lax.rsqrt(jnp.sum(x * x, axis=-1, keepdims=True) + L2_EPS) * (dk ** -0.5) for x in q]
    k = [x * lax.rsqrt(jnp.sum(x * x, axis=-1, keepdims=True) + L2_EPS) for x in k]
    kb = [k[u] * beta[u] for u in us]
    vb = [v[u] * beta[u] for u in us]
    kbg = [kb[u] * jnp.exp(gcc[u]) for u in us]
    qg = [q[u] * jnp.exp(gcc[u]) for u in us]
    k_tail = [k[u] * jnp.exp(glast[u] - gcc[u]) for u in us]
    lower = [jnp.where(strict, _mm(kb[u], k[u], NT) * decay[u], 0.0) for u in us]
    qk = [jnp.where(tril, _mm(q[u], k[u], NT) * decay[u], 0.0) for u in us]
    s_old = [s_ref[j, h] for j, h in units]
    qs = [_mm(qg[u], s_old[u]) for u in us]
    nn = _tri_inv_many(lower, levels)
    u_ = [vb[u] + _mm(nn[u], vb[u]) for u in us]
    w_ = [kbg[u] + _mm(nn[u], kbg[u]) for u in us]
    v_new = [u_[u] - _mm(w_[u], s_old[u]) for u in us]
    kv = [_mm(k_tail[u], v_new[u], TN) for u in us]
    for u, (j, h) in enumerate(units):
        s_ref[j, h] = s_old[u] * jnp.exp(glast[u]) + kv[u]
    o = [qs[u] + _mm(qk[u], v_new[u]) for u in us]
    for u, (j, h) in enumerate(units):
        oh = o[u] * lax.rsqrt(jnp.mean(o[u] * o[u], axis=-1, keepdims=True) + EPS) * nw_ref[...]
        oh = oh * _silu(z_ref[j, :, h * dv:(h + 1) * dv])
        o_ref[j, :, h * dv:(h + 1) * dv] = oh.astype(BF16)

    for j in seqs:
        prev_ref[j] = qkv_ref[j, c - SUBLANES:c, :]


def _state_specs(s0, ns, tails):
    shared = s0.shape[0] == 1
    lead = 1 if shared else ns
    idx4 = (lambda b, i: (0, 0, 0, 0)) if shared else (lambda b, i: (b, 0, 0, 0))
    idx3 = (lambda b, i: (0, 0, 0)) if shared else (lambda b, i: (0, b, 0))
    return ([pl.BlockSpec((lead,) + s0.shape[1:], idx4)]
            + [pl.BlockSpec((tl.shape[0], lead, tl.shape[2]), idx3) for tl in tails])


def _init_carry(prev_ref, tail_ref, j, jj):
    k = tail_ref.shape[0]
    if k < SUBLANES:
        prev_ref[j] = jnp.zeros(prev_ref.shape[1:], prev_ref.dtype)
    for r in range(k):
        row = SUBLANES - k + r
        prev_ref[j, row:row + 1, :] = tail_ref[r, jj:jj + 1, :]


def _gdn_parts(p, abt, s0, prev0, convw, alr, dtr, alc, dtc, nw, *, c, ns, t_valid, lay):
    nb, t, _ = p.shape
    nh, dk, dv = s0.shape[1:]
    wq = lay["qkv"][1]
    const2 = lambda b, i: (0, 0)
    return dict(
        operands=[p, p, p, abt, s0, prev0, convw, alr, dtr, alc, dtc, nw],
        in_specs=[
            pl.BlockSpec((ns, c, wq), lambda b, i: (b, i, lay["qkv"][0] // wq)),
            pl.BlockSpec((ns, c, nh * dv), lambda b, i: (b, i, lay["za"][0] // (nh * dv))),
            pl.BlockSpec((ns, c, LANES), lambda b, i: (b, i, lay["ab"][0] // LANES)),
            pl.BlockSpec((ns, 1, 2 * nh, c), lambda b, i: (b, i, 0, 0)),
            *_state_specs(s0, ns, [prev0]),
            pl.BlockSpec(convw.shape, const2),
            pl.BlockSpec(alr.shape, const2),
            pl.BlockSpec(dtr.shape, const2),
            pl.BlockSpec(alc.shape, const2),
            pl.BlockSpec(dtc.shape, const2),
            pl.BlockSpec(nw.shape, const2),
        ],
        out_shape=[jax.ShapeDtypeStruct((nb, t, nh * dv), BF16),
                   jax.ShapeDtypeStruct((nb, nh, dk, dv), F32)],
        out_specs=[pl.BlockSpec((ns, c, nh * dv), lambda b, i: (b, i, 0)),
                   pl.BlockSpec((ns, nh, dk, dv), lambda b, i: (b, 0, 0, 0))],
        scratch=[pltpu.VMEM((ns, SUBLANES, wq), F32)],
        statics=dict(c=c, ns=ns, nh=nh, dk=dk, dv=dv, t_valid=t_valid, t_total=t))


def _token_shift(x_ref, prev_ref, mu_ref, j, lo, hi):
    cur = x_ref[j, :, lo:hi]
    prv = _shift_rows(cur, prev_ref[j, :, lo:hi], 1)
    return cur + mu_ref[:, lo:hi] * (prv - cur)


_RWKV_OPERANDS = ("kkg", "rg", "bh", "kh", "kt", "bt", "v")


def _rwkv_kernel(x_ref, wa_ref, s0_ref, prev0_ref, prevwa0_ref, mu_ref, muwa_ref,
                 w0_ref, w2_ref, a0_ref, a2_ref, kk_ref, ka_ref, rk_ref, gnw_ref, gnb_ref,
                 o_ref, s_ref, prev_ref, prevwa_ref, sbd_ref,
                 *, c, ns, nh, n, t_valid, t_total):
    step = pl.program_id(1)
    d = nh * n
    npair = nh // 2
    ngrp = d // LANES
    shared = s0_ref.shape[0] == 1
    seqs = range(ns)

    @pl.when(step == 0)
    def _():
        zero = jnp.zeros((n, n), F32)
        for j in seqs:
            jj = 0 if shared else j
            for p in range(npair):
                top = jnp.concatenate([s0_ref[jj, 2 * p], zero], axis=1)
                bot = jnp.concatenate([zero, s0_ref[jj, 2 * p + 1]], axis=1)
                sbd_ref[j, p] = jnp.concatenate([top, bot], axis=0)
            _init_carry(prev_ref, prev0_ref, j, jj)
            _init_carry(prevwa_ref, prevwa0_ref, j, jj)

    w2 = 2 * c
    rowi = lax.broadcasted_iota(jnp.int32, (c, w2), 0)
    lane = lax.broadcasted_iota(jnp.int32, (c, w2), 1)
    lcol = lane & (c - 1)
    tril = rowi >= lcol
    strict = rowi > lcol
    diff = rowi ^ lcol
    levels = [(diff >> 1) == 0]
    sh = 2
    while (1 << sh) <= c:
        levels.append((diff >> (sh - 1)) == 1)
        sh += 1
    tok0 = (lane < c).astype(F32).astype(BF16)
    tok1 = (lane >= c).astype(F32).astype(BF16)
    chl = lax.broadcasted_iota(jnp.int32, (c, LANES), 1)
    ch0 = (chl < n).astype(F32).astype(BF16)
    ch1 = (chl >= n).astype(F32).astype(BF16)
    row2 = lax.broadcasted_iota(jnp.int32, (LANES, LANES), 0)
    lane2 = lax.broadcasted_iota(jnp.int32, (LANES, LANES), 1)
    log2n = n.bit_length() - 1
    same_head = (row2 >> log2n) == (lane2 >> log2n)
    seg_ones = same_head.astype(F32).astype(BF16)

    def bd_tok(xb):
        return jnp.concatenate([xb * tok0, xb * tok1], axis=0)

    def bd_ch(xb):
        return jnp.concatenate([xb * ch0, xb * ch1], axis=0)

    def dot(a, b, dims=NN):
        return lax.dot_general(a, b, dims, preferred_element_type=F32)

    def seg_sum(x):
        stk = jnp.concatenate([x[:, i * LANES:(i + 1) * LANES] for i in range(ngrp)], axis=0)
        acc = dot(stk.astype(BF16), seg_ones)
        return jnp.concatenate([acc[i * c:(i + 1) * c] for i in range(ngrp)], axis=1)

    lora = w2_ref.shape[0]
    row_c = lax.broadcasted_iota(jnp.int32, (c, c), 0)
    col_c = lax.broadcasted_iota(jnp.int32, (c, c), 1)
    tril_b = (row_c >= col_c).astype(BF16)

    def pre_steps(out):
        r = [_token_shift(x_ref, prev_ref, mu_ref, j, 0, d) for j in seqs]
        kb = [_token_shift(x_ref, prev_ref, mu_ref, j, d, 2 * d) for j in seqs]
        vb = [_token_shift(x_ref, prev_ref, mu_ref, j, 2 * d, 3 * d) for j in seqs]
        sz = [_silu(_token_shift(x_ref, prev_ref, mu_ref, j, 3 * d, 4 * d)) for j in seqs]
        wa = [wa_ref[j] for j in seqs]
        wa = [wa[j] + muwa_ref[...] * (_shift_rows(wa[j], prevwa_ref[j], 1) - wa[j])
              for j in seqs]
        tw = [jnp.tanh(x[:, 0:lora]) for x in wa]
        w_raw = [w0_ref[...] + _mm(tw[j], w2_ref[...]) for j in seqs]
        a = [_sigmoid(a0_ref[...] + _mm(wa[j][:, lora:], a2_ref[...])) for j in seqs]
        w_log = [-jnp.exp(-_softplus(-x) - 0.5) for x in w_raw]
        kku = [kb[j] * kk_ref[...] for j in seqs]
        k2 = [kb[j] * (1.0 + (a[j] - 1.0) * ka_ref[...]) for j in seqs]
        if t_valid < t_total:
            ok = (step * c + lax.broadcasted_iota(jnp.int32, (c, d), 0)) < t_valid
            w_log = [jnp.where(ok, x, 0.0) for x in w_log]
            kku = [jnp.where(ok, x, 0.0) for x in kku]
            vb = [jnp.where(ok, x, 0.0) for x in vb]
        w_hi = [x.astype(BF16) for x in w_log]
        rem = [w_log[j] - w_hi[j].astype(F32) for j in seqs]
        w_mid = [x.astype(BF16) for x in rem]
        w_lo = [(rem[j] - w_mid[j].astype(F32)).astype(BF16) for j in seqs]
        gcum = [dot(tril_b, w_hi[j]) + dot(tril_b, w_mid[j]) + dot(tril_b, w_lo[j]) for j in seqs]
        glast = [x[c - 1:c, :] for x in gcum]
        g_in = [jnp.exp(x) for x in gcum]
        g_ex = [jnp.exp(gcum[j] - w_log[j]) for j in seqs]
        g_inv = [jnp.exp(-x) for x in gcum]
        g_tail = [jnp.exp(glast[j] - gcum[j]) for j in seqs]
        ssq = [x * x for x in kku]
        bonus_in = [r[j] * k2[j] * rk_ref[...] for j in seqs]
        kk = [kku[j] * lax.rsqrt(seg_sum(ssq[j]) + L2_EPS) for j in seqs]
        bonus = [seg_sum(bonus_in[j]) * vb[j] for j in seqs]
        for j in seqs:
            b = kk[j] * a[j]
            out[j].update(
                kkg=(kk[j] * g_ex[j]).astype(BF16), rg=(r[j] * g_in[j]).astype(BF16),
                bh=(b * g_inv[j]).astype(BF16), kh=(k2[j] * g_inv[j]).astype(BF16),
                kt=(k2[j] * g_tail[j]).astype(BF16), bt=(b * g_tail[j]).astype(BF16),
                v=vb[j].astype(BF16), bonus=bonus[j], sz=sz[j], g_all=jnp.exp(glast[j]))

    units = [(j, p) for j in seqs for p in range(npair)]
    us = range(len(units))

    def stage_steps(ops):
        kkg, rg, bh, kh, kt, bt, v = (
            [ops[j][k][:, p * LANES:(p + 1) * LANES] for j, p in units] for k in _RWKV_OPERANDS)
        g_all = [ops[j]["g_all"][:, p * LANES:(p + 1) * LANES] for j, p in units]
        lhs = [jnp.concatenate([kkg[u], rg[u]], axis=0) for u in us]
        rhs = [jnp.concatenate([bd_ch(bh[u]), bd_ch(kh[u])], axis=0) for u in us]
        big = [dot(lhs[u], rhs[u], NT) for u in us]
        s_old = [sbd_ref[j, p] for j, p in units]
        from_s = [dot(lhs[u], s_old[u].astype(BF16), NT) for u in us]
        low = [jnp.where(strict, x[0:c, 0:w2], 0.0) for x in big]
        m_kv = [jnp.where(strict, x[0:c, w2:2 * w2], 0.0) for x in big]
        q_b = [jnp.where(tril, x[c:2 * c, 0:w2], 0.0) for x in big]
        p_kv = [jnp.where(tril, x[c:2 * c, w2:2 * w2], 0.0) for x in big]
        mp = [dot(jnp.concatenate([m_kv[u], p_kv[u]], axis=0).astype(BF16), bd_ch(v[u]))
              for u in us]
        nn = [-jnp.where(levels[0], x, 0.0) for x in low]
        for m in levels[1:]:
            off = [jnp.where(m, x, 0.0) for x in low]
            nnb = [x.astype(BF16) for x in nn]
            xc = [off[u] + dot(nnb[u], bd_tok(off[u].astype(BF16))) for u in us]
            nn = [nn[u] - (xc[u] + dot(xc[u].astype(BF16), bd_tok(nnb[u]))) for u in us]
        rhs_sa = [from_s[u][0:c] + mp[u][0:c] for u in us]
        sa = [rhs_sa[u] + dot(nn[u].astype(BF16), bd_ch(rhs_sa[u].astype(BF16))) for u in us]
        sab = [x.astype(BF16) for x in sa]
        qs = [dot(q_b[u].astype(BF16), bd_ch(sab[u])) for u in us]
        upd = [dot(jnp.concatenate([v[u], -sab[u]], axis=0),
                   jnp.concatenate([kt[u], bt[u]], axis=0), TN) for u in us]
        s_new = [s_old[u] * g_all[u] + jnp.where(same_head, upd[u], 0.0) for u in us]
        for u, (j, p) in enumerate(units):
            sbd_ref[j, p] = s_new[u]

        @pl.when(step == pl.num_programs(1) - 1)
        def _():
            for u, (j, p) in enumerate(units):
                s_ref[j, 2 * p] = s_new[u][0:n, 0:n]
                s_ref[j, 2 * p + 1] = s_new[u][n:2 * n, n:2 * n]

        y = [from_s[u][c:2 * c] + mp[u][c:2 * c] - qs[u] for u in us]
        yj = [jnp.concatenate(y[j * npair:(j + 1) * npair], axis=1) for j in seqs]
        yc = [yj[j] - seg_sum(yj[j]) * (1.0 / n) for j in seqs]
        var = [seg_sum(yc[j] * yc[j]) * (1.0 / n) for j in seqs]
        for j in seqs:
            yn = yc[j] * lax.rsqrt(var[j] + n * GN_EPS_PER_CH) * gnw_ref[...] + gnb_ref[...]
            o_ref[j] = ((yn + ops[j]["bonus"]) * ops[j]["sz"]).astype(BF16)

    ops = [dict() for _ in seqs]
    pre_steps(ops)
    stage_steps(ops)

    for j in seqs:
        prev_ref[j] = x_ref[j, c - SUBLANES:c, :]
        prevwa_ref[j] = wa_ref[j, c - SUBLANES:c, :]


def _rwkv_parts(p, s0, prev0, prevwa0, mu, muwa, w0, w2, a0, a2, kk, ka, rk, gnw, gnb,
                *, c, ns, t_valid, lay):
    nb, t, _ = p.shape
    nh, n = s0.shape[1:3]
    d = nh * n
    wx = lay["rkvz"][1]
    const2 = lambda b, i: (0, 0)
    vec = pl.BlockSpec((1, d), const2)
    assert 2 * n == LANES and c <= n and c & (c - 1) == 0, "two heads per lane group"
    return dict(
        operands=[p, p, s0, prev0, prevwa0, mu, muwa, w0, w2, a0, a2, kk, ka, rk, gnw, gnb],
        in_specs=[
            pl.BlockSpec((ns, c, wx), lambda b, i: (b, i, lay["rkvz"][0] // wx)),
            pl.BlockSpec((ns, c, LANES), lambda b, i: (b, i, lay["wa"][0] // LANES)),
            *_state_specs(s0, ns, [prev0, prevwa0]),
            pl.BlockSpec((1, wx), const2),
            pl.BlockSpec((1, LANES), const2),
            vec,
            pl.BlockSpec(w2.shape, const2),
            vec,
            pl.BlockSpec(a2.shape, const2),
            vec, vec, vec, vec, vec,
        ],
        out_shape=[jax.ShapeDtypeStruct((nb, t, d), BF16),
                   jax.ShapeDtypeStruct((nb, nh, n, n), F32)],
        out_specs=[pl.BlockSpec((ns, c, d), lambda b, i: (b, i, 0)),
                   pl.BlockSpec((ns, nh, n, n), lambda b, i: (b, 0, 0, 0))],
        scratch=[pltpu.VMEM((ns, SUBLANES, wx), F32), pltpu.VMEM((ns, SUBLANES, LANES), F32),
                 pltpu.VMEM((ns, nh // 2, LANES, LANES), F32)],
        statics=dict(c=c, ns=ns, nh=nh, n=n, t_valid=t_valid, t_total=t))


def _recurrence(body, parts, *, name):
    statics = parts["statics"]
    nb, t = parts["out_shape"][0].shape[:2]
    return pl.pallas_call(
        functools.partial(body, **statics),
        out_shape=tuple(parts["out_shape"]),
        grid=(nb // statics["ns"], t // statics["c"]),
        in_specs=parts["in_specs"],
        out_specs=tuple(parts["out_specs"]),
        scratch_shapes=parts["scratch"],
        compiler_params=pltpu.CompilerParams(
            dimension_semantics=("parallel", "arbitrary"), vmem_limit_bytes=VMEM_LIMIT),
        name=name,
    )(*parts["operands"])


def _merge_kernel(oa_ref, ob_ref, gate_ref, x_ref, woa_ref, wob_ref, wo_ref, lnf_ref, y_ref):
    d = x_ref.shape[1]
    ba = jnp.dot(oa_ref[...], woa_ref[...], preferred_element_type=F32)
    bb = jnp.dot(ob_ref[...], wob_ref[...], preferred_element_type=F32)
    gates = _sigmoid(gate_ref[...])
    merged = gates[:, :d] * ba + gates[:, d:] * bb
    xn = x_ref[...] + jnp.dot(merged.astype(BF16), wo_ref[...], preferred_element_type=F32)
    y_ref[...] = xn * lax.rsqrt(jnp.mean(xn * xn, axis=-1, keepdims=True) + EPS) * lnf_ref[...]


def _merge(oa, ob, p, x, woa, wob, wo, lnf, *, lay):
    m, d = x.shape
    tm = min(m, 512)
    wg = lay["gate"][1]
    const2 = lambda i: (0, 0)
    return pl.pallas_call(
        _merge_kernel,
        out_shape=jax.ShapeDtypeStruct((m, d), F32),
        grid=(m // tm,),
        in_specs=[
            pl.BlockSpec((tm, oa.shape[1]), lambda i: (i, 0)),
            pl.BlockSpec((tm, ob.shape[1]), lambda i: (i, 0)),
            pl.BlockSpec((tm, wg), lambda i: (i, lay["gate"][0] // wg)),
            pl.BlockSpec((tm, d), lambda i: (i, 0)),
            pl.BlockSpec(woa.shape, const2),
            pl.BlockSpec(wob.shape, const2),
            pl.BlockSpec(wo.shape, const2),
            pl.BlockSpec((1, d), const2),
        ],
        out_specs=pl.BlockSpec((tm, d), lambda i: (i, 0)),
        compiler_params=pltpu.CompilerParams(
            dimension_semantics=("parallel",), vmem_limit_bytes=VMEM_LIMIT),
        name="merge",
    )(oa, ob, p, x, woa, wob, wo, lnf)


def _pad_lanes(v, width):
    return jnp.pad(v, ((0, 0), (0, width - v.shape[1])))


def _token_major_t(ab, c):
    nb, t, k = ab.shape
    return jnp.transpose(ab.reshape(nb, t // c, c, k), (0, 1, 3, 2))


def kernel(x_prompt, x_sample, state_gdn, state_gdn_conv, state_rwkv, state_shift, meta_tokens,
           ln1_w, w_in, gdn_conv_w, gdn_a_log, gdn_dt_bias, gdn_norm_w, w_out_a, rwkv_mu, rwkv_w0,
           rwkv_w2, rwkv_a0, rwkv_a2, rwkv_k_k, rwkv_k_a, rwkv_r_k, rwkv_gn_w, rwkv_gn_b, w_out_b,
           w_out, lnf_w):
    assert ln1_w.shape[0] == 1, "single-layer trunk"
    bp, seq, d = x_prompt.shape
    bs, tseq, _ = x_sample.shape
    n_meta = meta_tokens.shape[0]
    _, _, nh_a, dk, dv = state_gdn.shape
    _, _, nh_b, n_b, _ = state_rwkv.shape
    kw, w_qkv = gdn_conv_w.shape[1:]
    lora_w = rwkv_w2.shape[1]
    lora_a = rwkv_a2.shape[1]
    d_a = nh_a * dv
    d_b = nh_b * n_b
    assert w_qkv == 2 * nh_a * dk + d_a and d_a == d and d_b == d and lora_w + lora_a == LANES
    assert n_meta % SUBLANES == 0 and n_meta % 16 == 0 and tseq >= kw - 1

    o_a = w_qkv
    o_b = o_a + nh_a
    o_z = o_b + nh_a
    o_r = o_z + d_a
    o_g = o_r + 3 * d_b + lora_w + lora_a + d_b
    w = w_in[0].T
    wr = w[o_r:o_g]
    mu = rwkv_mu
    lora0 = 3 * d_b
    rows = [w[:w_qkv], w[o_z:o_r], wr[:lora0], wr[lora0 + LANES:], w[o_g:],
            wr[lora0:lora0 + LANES], w[o_a:o_z]]
    used = sum(rw.shape[0] for rw in rows)
    n_pad = -(-(used + LANES - 2 * nh_a) // PROJ_TN) * PROJ_TN
    rows.append(jnp.zeros((n_pad - used, d), F32))
    w_all = jnp.concatenate([rw.astype(BF16) for rw in rows], axis=0)
    lay = {"qkv": (0, w_qkv), "za": (w_qkv, d_a), "rkvz": (w_qkv + d_a, 4 * d_b),
           "gate": (w_qkv + d_a + 4 * d_b, 2 * d)}
    lay["wa"] = (lay["gate"][0] + 2 * d, LANES)
    lay["ab"] = (lay["wa"][0] + LANES, LANES)
    for off, width in lay.values():
        assert off % width == 0
    mu_x = jnp.concatenate([mu[:, :lora0], mu[:, lora0 + LANES:]], axis=1)
    mu_wa = mu[:, lora0:lora0 + LANES]

    alr = _pad_lanes(gdn_a_log, LANES)
    dtr = _pad_lanes(gdn_dt_bias, LANES)
    alc = jnp.pad(gdn_a_log.reshape(nh_a, 1), ((0, nh_a), (0, 0)))
    dtc = jnp.pad(gdn_dt_bias.reshape(nh_a, 1), ((0, nh_a), (0, 0)))
    convw = gdn_conv_w[0]
    rk = rwkv_r_k.reshape(1, d_b)
    woa = w_out_a[0].astype(BF16)
    wob = w_out_b[0].astype(BF16)
    wo = w_out[0].astype(BF16)
    lnf = lnf_w.reshape(1, d)

    def branches(p, c, ns_a, ns_b, t_valid, s_gdn, conv_tail, s_rwkv, x_tail, wa_tail):
        abt = _token_major_t(p[:, :, lay["ab"][0]:lay["ab"][0] + 2 * nh_a], c)
        oa, sg = _recurrence(_gdn_kernel, _gdn_parts(
            p, abt, s_gdn, conv_tail, convw, alr, dtr, alc, dtc, gdn_norm_w,
            c=c, ns=ns_a, t_valid=t_valid, lay=lay), name="gdn")
        ob, sr = _recurrence(_rwkv_kernel, _rwkv_parts(
            p, s_rwkv, x_tail, wa_tail, mu_x, mu_wa, rwkv_w0, rwkv_w2[0], rwkv_a0, rwkv_a2[0],
            rwkv_k_k, rwkv_k_a, rk, rwkv_gn_w, rwkv_gn_b,
            c=c, ns=ns_b, t_valid=t_valid, lay=lay), name="rwkv")
        return oa, sg, ob, sr

    x0, wx = lay["rkvz"]
    a0_, _ = lay["wa"]

    p_m = _proj(meta_tokens, ln1_w, w_all, apply_norm=True)
    _, sg_m, _, sr_m = branches(
        p_m[None], n_meta, 1, 1, n_meta,
        jnp.zeros((1, nh_a, dk, dv), F32), jnp.zeros((1, 1, w_qkv), F32),
        jnp.zeros((1, nh_b, n_b, n_b), F32), jnp.zeros((1, 1, wx), F32),
        jnp.zeros((1, 1, LANES), F32))

    xp = x_prompt.reshape(bp * seq, d)
    p_p = _proj(xp, ln1_w, w_all, apply_norm=True)
    last_m = p_m[n_meta - 1:, None]
    oa_p, sg_p, ob_p, sr_p = branches(
        p_p.reshape(bp, seq, n_pad), 64, 4, 2, seq, sg_m,
        p_m[n_meta - (kw - 1):, None, :w_qkv], sr_m,
        last_m[:, :, x0:x0 + wx], last_m[:, :, a0_:a0_ + LANES])
    y_p = _merge(oa_p.reshape(bp * seq, d), ob_p.reshape(bp * seq, d), p_p, xp, woa, wob, wo, lnf,
                 lay=lay)

    tpad = 16
    xs = jnp.pad(x_sample, ((0, 0), (0, tpad - tseq), (0, 0))).reshape(bs * tpad, d)
    p_s = _proj(xs, ln1_w, w_all, apply_norm=True)
    p_first = _proj(state_shift[0], ln1_w, w_all, apply_norm=False)[None]
    oa_s, sg_s, ob_s, sr_s = branches(
        p_s.reshape(bs, tpad, n_pad), tpad, 8, 8, tseq, state_gdn[0],
        jnp.transpose(state_gdn_conv[0], (1, 0, 2)), state_rwkv[0],
        p_first[:, :, x0:x0 + wx], p_first[:, :, a0_:a0_ + LANES])
    y_s = _merge(oa_s.reshape(bs * tpad, d), ob_s.reshape(bs * tpad, d), p_s, xs, woa, wob, wo,
                 lnf, lay=lay)

    shift_p = _rmsnorm_rows(x_prompt[:, -1], ln1_w)
    shift_s = _rmsnorm_rows(x_sample[:, -1], ln1_w)
    conv_p = p_p.reshape(bp, seq, n_pad)[:, seq - (kw - 1):, :w_qkv]
    conv_s = p_s.reshape(bs, tpad, n_pad)[:, tseq - (kw - 1):tseq, :w_qkv]
    return (y_p.reshape(bp, seq, d), y_s.reshape(bs, tpad, d)[:, :tseq],
            sg_p[None], conv_p[None], sr_p[None], shift_p[None],
            sg_s[None], conv_s[None], sr_s[None], shift_s[None])
```

```python
import functools

import jax
import jax.numpy as jnp
from jax import lax
from jax.experimental import pallas as pl
from jax.experimental.pallas import tpu as pltpu

F32 = jnp.float32
BF16 = jnp.bfloat16

EPS = 1e-6
L2_EPS = 1e-6
GN_EPS_PER_CH = 1e-5

LANES = 128
SUBLANES = 8
VMEM_LIMIT = 48 * 1024 * 1024
PROJ_TM = 1024
PROJ_TN = 1536

NN = (((1,), (0,)), ((), ()))
NT = (((1,), (1,)), ((), ()))
TN = (((0,), (0,)), ((), ()))


def _mm(a, b, dims=NN):
    return lax.dot_general(a.astype(BF16), b.astype(BF16), dims, preferred_element_type=F32)


def _mm_f32(a, b, dims=NN):
    return lax.dot_general(a, b, dims, precision=lax.Precision.HIGHEST,
                           preferred_element_type=F32)


def _sigmoid(x):
    return 1.0 / (1.0 + jnp.exp(-x))


def _silu(x):
    return x * _sigmoid(x)


def _softplus(x):
    return jnp.maximum(x, 0.0) + jnp.log(1.0 + jnp.exp(-jnp.abs(x)))


def _shift_rows(cur, prev, k):
    rolled = pltpu.roll(cur, k, 0)
    fix = pltpu.roll(prev, k, 0)
    wrap = lax.broadcasted_iota(jnp.int32, fix.shape, 0) < k
    head = jnp.where(wrap, fix, rolled[0:SUBLANES])
    if cur.shape[0] == SUBLANES:
        return head
    return jnp.concatenate([head, rolled[SUBLANES:]], axis=0)


def _tri_masks(c):
    row = lax.broadcasted_iota(jnp.int32, (c, c), 0)
    col = lax.broadcasted_iota(jnp.int32, (c, c), 1)
    diff = row ^ col
    levels = [(diff >> 1) == 0]
    sh = 2
    while (1 << sh) <= c:
        levels.append((diff >> (sh - 1)) == 1)
        sh += 1
    return row, col, levels


def _tri_inv_many(lows, levels):
    nn = [-jnp.where(levels[0], low, 0.0) for low in lows]
    for m in levels[1:]:
        off = [jnp.where(m, low, 0.0) for low in lows]
        xc = [o + _mm(x, o) for x, o in zip(nn, off)]
        nn = [x - (y + _mm(y, x)) for x, y in zip(nn, xc)]
    return nn


def _proj_kernel(x_ref, lnw_ref, w_ref, o_ref, h_ref, *, apply_norm):
    @pl.when(pl.program_id(1) == 0)
    def _():
        x = x_ref[...]
        if apply_norm:
            x = x * lax.rsqrt(jnp.mean(x * x, axis=-1, keepdims=True) + EPS)
            x = x * lnw_ref[...]
        h_ref[...] = x.astype(BF16)

    o_ref[...] = lax.dot_general(h_ref[...], w_ref[...], NT, preferred_element_type=F32)


def _proj(x, lnw, w, *, apply_norm):
    m, d = x.shape
    n = w.shape[0]
    tm = min(m, PROJ_TM)
    tn = PROJ_TN
    assert m % tm == 0 and n % tn == 0
    return pl.pallas_call(
        functools.partial(_proj_kernel, apply_norm=apply_norm),
        out_shape=jax.ShapeDtypeStruct((m, n), F32),
        grid=(m // tm, n // tn),
        in_specs=[pl.BlockSpec((tm, d), lambda i, j: (i, 0)),
                  pl.BlockSpec((1, d), lambda i, j: (0, 0)),
                  pl.BlockSpec((tn, d), lambda i, j: (j, 0))],
        out_specs=pl.BlockSpec((tm, tn), lambda i, j: (i, j)),
        scratch_shapes=[pltpu.VMEM((tm, d), BF16)],
        compiler_params=pltpu.CompilerParams(
            dimension_semantics=("parallel", "arbitrary"), vmem_limit_bytes=VMEM_LIMIT),
        name="proj",
    )(x, lnw, w)


def _rmsnorm_kernel(x_ref, w_ref, o_ref):
    x = x_ref[...]
    o_ref[...] = x * lax.rsqrt(jnp.mean(x * x, axis=-1, keepdims=True) + EPS) * w_ref[...]


def _rmsnorm_rows(x, w):
    return pl.pallas_call(
        _rmsnorm_kernel,
        out_shape=jax.ShapeDtypeStruct(x.shape, F32),
        name="rmsnorm_rows",
    )(x, w)


def _gdn_kernel(qkv_ref, z_ref, ab_ref, abt_ref, s0_ref, prev0_ref, convw_ref,
                alr_ref, dtr_ref, alc_ref, dtc_ref, nw_ref,
                o_ref, s_ref, prev_ref, *, c, ns, nh, dk, dv, t_valid, t_total):
    ci = pl.program_id(1)
    shared = s0_ref.shape[0] == 1

    @pl.when(ci == 0)
    def _():
        for j in range(ns):
            s_ref[j] = s0_ref[0 if shared else j]
            _init_carry(prev_ref, prev0_ref, j, 0 if shared else j)

    row, col, levels = _tri_masks(c)
    tril = row >= col
    strict = row > col
    kw = convw_ref.shape[0]
    seqs = range(ns)

    ab = [ab_ref[j] for j in seqs]
    g_c = [-jnp.exp(alr_ref[...]) * _softplus(x + dtr_ref[...]) for x in ab]
    beta_c = [_sigmoid(x) for x in ab]
    abt = [abt_ref[j, 0] for j in seqs]
    g_r = [-jnp.exp(alc_ref[...]) * _softplus(x + dtc_ref[...]) for x in abt]
    if t_valid < t_total:
        tok_c = ci * c + lax.broadcasted_iota(jnp.int32, ab[0].shape, 0)
        tok_r = ci * c + lax.broadcasted_iota(jnp.int32, abt[0].shape, 1)
        g_c = [jnp.where(tok_c < t_valid, x, 0.0) for x in g_c]
        beta_c = [jnp.where(tok_c < t_valid, x, 0.0) for x in beta_c]
        g_r = [jnp.where(tok_r < t_valid, x, 0.0) for x in g_r]
    gc_c = _mm_f32(tril.astype(F32), jnp.concatenate(g_c, axis=1))
    gc_r = _mm_f32(jnp.concatenate(g_r, axis=0), (row <= col).astype(F32))

    def conv(j, col0, width):
        cur = qkv_ref[j, :, col0:col0 + width]
        prev = prev_ref[j, :, col0:col0 + width]
        acc = None
        for k in range(kw - 1, -1, -1):
            src = cur if k == 0 else _shift_rows(cur, prev, k)
            term = src * convw_ref[kw - 1 - k:kw - k, col0:col0 + width]
            acc = term if acc is None else acc + term
        return _silu(acc)

    units = [(j, h) for j in seqs for h in range(nh)]
    us = range(len(units))
    gcc = [gc_c[:, j * LANES + h:j * LANES + h + 1] for j, h in units]
    gcr = [gc_r[j * 2 * nh + h:j * 2 * nh + h + 1, :] for j, h in units]
    beta = [beta_c[j][:, nh + h:nh + h + 1] for j, h in units]
    glast = [x[c - 1:c, :] for x in gcc]
    decay = [jnp.where(tril, jnp.exp(jnp.where(tril, gcc[u] - gcr[u], 0.0)), 0.0) for u in us]
    q = [conv(j, h * dk, dk) for j, h in units]
    k = [conv(j, nh * dk + h * dk, dk) for j, h in units]
    v = [conv(j, 2 * nh * dk + h * dv, dv) for j, h in units]
    q = [x * lax.rsqrt(jnp.sum(x * x, axis=-1, keepdims=True) + L2_EPS) * (dk ** -0.5) for x in q]
    k = [x * lax.rsqrt(jnp.sum(x * x, axis=-1, keepdims=True) + L2_EPS) for x in k]
    kb = [k[u] * beta[u] for u in us]
    vb = [v[u] * beta[u] for u in us]
    kbg = [kb[u] * jnp.exp(gcc[u]) for u in us]
    qg = [q[u] * jnp.exp(gcc[u]) for u in us]
    k_tail = [k[u] * jnp.exp(glast[u] - gcc[u]) for u in us]
    lower = [jnp.where(strict, _mm(kb[u], k[u], NT) * decay[u], 0.0) for u in us]
    qk = [jnp.where(tril, _mm(q[u], k[u], NT) * decay[u], 0.0) for u in us]
    s_old = [s_ref[j, h] for j, h in units]
    qs = [_mm(qg[u], s_old[u]) for u in us]
    nn = _tri_inv_many(lower, levels)
    u_ = [vb[u] + _mm(nn[u], vb[u]) for u in us]
    w_ = [kbg[u] + _mm(nn[u], kbg[u]) for u in us]
    v_new = [u_[u] - _mm(w_[u], s_old[u]) for u in us]
    kv = [_mm(k_tail[u], v_new[u], TN) for u in us]
    for u, (j, h) in enumerate(units):
        s_ref[j, h] = s_old[u] * jnp.exp(glast[u]) + kv[u]
    o = [qs[u] + _mm(qk[u], v_new[u]) for u in us]
    for u, (j, h) in enumerate(units):
        oh = o[u] * lax.rsqrt(jnp.mean(o[u] * o[u], axis=-1, keepdims=True) + EPS) * nw_ref[...]
        oh = oh * _silu(z_ref[j, :, h * dv:(h + 1) * dv])
        o_ref[j, :, h * dv:(h + 1) * dv] = oh.astype(BF16)

    for j in seqs:
        prev_ref[j] = qkv_ref[j, c - SUBLANES:c, :]


def _state_specs(s0, ns, tails):
    shared = s0.shape[0] == 1
    lead = 1 if shared else ns
    idx4 = (lambda b, i: (0, 0, 0, 0)) if shared else (lambda b, i: (b, 0, 0, 0))
    idx3 = (lambda b, i: (0, 0, 0)) if shared else (lambda b, i: (0, b, 0))
    return ([pl.BlockSpec((lead,) + s0.shape[1:], idx4)]
            + [pl.BlockSpec((tl.shape[0], lead, tl.shape[2]), idx3) for tl in tails])


def _init_carry(prev_ref, tail_ref, j, jj):
    k = tail_ref.shape[0]
    if k < SUBLANES:
        prev_ref[j] = jnp.zeros(prev_ref.shape[1:], prev_ref.dtype)
    for r in range(k):
        row = SUBLANES - k + r
        prev_ref[j, row:row + 1, :] = tail_ref[r, jj:jj + 1, :]


def _gdn_parts(p, abt, s0, prev0, convw, alr, dtr, alc, dtc, nw, *, c, ns, t_valid, lay):
    nb, t, _ = p.shape
    nh, dk, dv = s0.shape[1:]
    wq = lay["qkv"][1]
    const2 = lambda b, i: (0, 0)
    return dict(
        operands=[p, p, p, abt, s0, prev0, convw, alr, dtr, alc, dtc, nw],
        in_specs=[
            pl.BlockSpec((ns, c, wq), lambda b, i: (b, i, lay["qkv"][0] // wq)),
            pl.BlockSpec((ns, c, nh * dv), lambda b, i: (b, i, lay["za"][0] // (nh * dv))),
            pl.BlockSpec((ns, c, LANES), lambda b, i: (b, i, lay["ab"][0] // LANES)),
            pl.BlockSpec((ns, 1, 2 * nh, c), lambda b, i: (b, i, 0, 0)),
            *_state_specs(s0, ns, [prev0]),
            pl.BlockSpec(convw.shape, const2),
            pl.BlockSpec(alr.shape, const2),
            pl.BlockSpec(dtr.shape, const2),
            pl.BlockSpec(alc.shape, const2),
            pl.BlockSpec(dtc.shape, const2),
            pl.BlockSpec(nw.shape, const2),
        ],
        out_shape=[jax.ShapeDtypeStruct((nb, t, nh * dv), BF16),
                   jax.ShapeDtypeStruct((nb, nh, dk, dv), F32)],
        out_specs=[pl.BlockSpec((ns, c, nh * dv), lambda b, i: (b, i, 0)),
                   pl.BlockSpec((ns, nh, dk, dv), lambda b, i: (b, 0, 0, 0))],
        scratch=[pltpu.VMEM((ns, SUBLANES, wq), F32)],
        statics=dict(c=c, ns=ns, nh=nh, dk=dk, dv=dv, t_valid=t_valid, t_total=t))


def _token_shift(x_ref, prev_ref, mu_ref, j, lo, hi):
    cur = x_ref[j, :, lo:hi]
    prv = _shift_rows(cur, prev_ref[j, :, lo:hi], 1)
    return cur + mu_ref[:, lo:hi] * (prv - cur)


_RWKV_OPERANDS = ("kkg", "rg", "bh", "kh", "kt", "bt", "v")


def _rwkv_kernel(x_ref, wa_ref, s0_ref, prev0_ref, prevwa0_ref, mu_ref, muwa_ref,
                 w0_ref, w2_ref, a0_ref, a2_ref, kk_ref, ka_ref, rk_ref, gnw_ref, gnb_ref,
                 o_ref, s_ref, prev_ref, prevwa_ref, sbd_ref,
                 *, c, ns, nh, n, t_valid, t_total):
    step = pl.program_id(1)
    d = nh * n
    npair = nh // 2
    ngrp = d // LANES
    shared = s0_ref.shape[0] == 1
    seqs = range(ns)

    @pl.when(step == 0)
    def _():
        zero = jnp.zeros((n, n), F32)
        for j in seqs:
            jj = 0 if shared else j
            for p in range(npair):
                top = jnp.concatenate([s0_ref[jj, 2 * p], zero], axis=1)
                bot = jnp.concatenate([zero, s0_ref[jj, 2 * p + 1]], axis=1)
                sbd_ref[j, p] = jnp.concatenate([top, bot], axis=0)
            _init_carry(prev_ref, prev0_ref, j, jj)
            _init_carry(prevwa_ref, prevwa0_ref, j, jj)

    w2 = 2 * c
    rowi = lax.broadcasted_iota(jnp.int32, (c, w2), 0)
    lane = lax.broadcasted_iota(jnp.int32, (c, w2), 1)
    lcol = lane & (c - 1)
    tril = rowi >= lcol
    strict = rowi > lcol
    diff = rowi ^ lcol
    levels = [(diff >> 1) == 0]
    sh = 2
    while (1 << sh) <= c:
        levels.append((diff >> (sh - 1)) == 1)
        sh += 1
    tok0 = (lane < c).astype(F32).astype(BF16)
    tok1 = (lane >= c).astype(F32).astype(BF16)
    chl = lax.broadcasted_iota(jnp.int32, (c, LANES), 1)
    ch0 = (chl < n).astype(F32).astype(BF16)
    ch1 = (chl >= n).astype(F32).astype(BF16)
    row2 = lax.broadcasted_iota(jnp.int32, (LANES, LANES), 0)
    lane2 = lax.broadcasted_iota(jnp.int32, (LANES, LANES), 1)
    log2n = n.bit_length() - 1
    same_head = (row2 >> log2n) == (lane2 >> log2n)
    seg_ones = same_head.astype(F32).astype(BF16)

    def bd_tok(xb):
        return jnp.concatenate([xb * tok0, xb * tok1], axis=0)

    def bd_ch(xb):
        return jnp.concatenate([xb * ch0, xb * ch1], axis=0)

    def dot(a, b, dims=NN):
        return lax.dot_general(a, b, dims, preferred_element_type=F32)

    def seg_sum(x):
        stk = jnp.concatenate([x[:, i * LANES:(i + 1) * LANES] for i in range(ngrp)], axis=0)
        acc = dot(stk.astype(BF16), seg_ones)
        return jnp.concatenate([acc[i * c:(i + 1) * c] for i in range(ngrp)], axis=1)

    lora = w2_ref.shape[0]
    row_c = lax.broadcasted_iota(jnp.int32, (c, c), 0)
    col_c = lax.broadcasted_iota(jnp.int32, (c, c), 1)
    tril_b = (row_c >= col_c).astype(BF16)

    def pre_steps(out):
        r = [_token_shift(x_ref, prev_ref, mu_ref, j, 0, d) for j in seqs]
        kb = [_token_shift(x_ref, prev_ref, mu_ref, j, d, 2 * d) for j in seqs]
        vb = [_token_shift(x_ref, prev_ref, mu_ref, j, 2 * d, 3 * d) for j in seqs]
        sz = [_silu(_token_shift(x_ref, prev_ref, mu_ref, j, 3 * d, 4 * d)) for j in seqs]
        wa = [wa_ref[j] for j in seqs]
        wa = [wa[j] + muwa_ref[...] * (_shift_rows(wa[j], prevwa_ref[j], 1) - wa[j])
              for j in seqs]
        tw = [jnp.tanh(x[:, 0:lora]) for x in wa]
        w_raw = [w0_ref[...] + _mm(tw[j], w2_ref[...]) for j in seqs]
        a = [_sigmoid(a0_ref[...] + _mm(wa[j][:, lora:], a2_ref[...])) for j in seqs]
        w_log = [-jnp.exp(-_softplus(-x) - 0.5) for x in w_raw]
        kku = [kb[j] * kk_ref[...] for j in seqs]
        k2 = [kb[j] * (1.0 + (a[j] - 1.0) * ka_ref[...]) for j in seqs]
        if t_valid < t_total:
            ok = (step * c + lax.broadcasted_iota(jnp.int32, (c, d), 0)) < t_valid
            w_log = [jnp.where(ok, x, 0.0) for x in w_log]
            kku = [jnp.where(ok, x, 0.0) for x in kku]
            vb = [jnp.where(ok, x, 0.0) for x in vb]
        w_hi = [x.astype(BF16) for x in w_log]
        rem = [w_log[j] - w_hi[j].astype(F32) for j in seqs]
        w_mid = [x.astype(BF16) for x in rem]
        w_lo = [(rem[j] - w_mid[j].astype(F32)).astype(BF16) for j in seqs]
        gcum = [dot(tril_b, w_hi[j]) + dot(tril_b, w_mid[j]) + dot(tril_b, w_lo[j]) for j in seqs]
        glast = [x[c - 1:c, :] for x in gcum]
        g_in = [jnp.exp(x) for x in gcum]
        g_ex = [jnp.exp(gcum[j] - w_log[j]) for j in seqs]
        g_inv = [jnp.exp(-x) for x in gcum]
        g_tail = [jnp.exp(glast[j] - gcum[j]) for j in seqs]
        ssq = [x * x for x in kku]
        bonus_in = [r[j] * k2[j] * rk_ref[...] for j in seqs]
        kk = [kku[j] * lax.rsqrt(seg_sum(ssq[j]) + L2_EPS) for j in seqs]
        bonus = [seg_sum(bonus_in[j]) * vb[j] for j in seqs]
        for j in seqs:
            b = kk[j] * a[j]
            out[j].update(
                kkg=(kk[j] * g_ex[j]).astype(BF16), rg=(r[j] * g_in[j]).astype(BF16),
                bh=(b * g_inv[j]).astype(BF16), kh=(k2[j] * g_inv[j]).astype(BF16),
                kt=(k2[j] * g_tail[j]).astype(BF16), bt=(b * g_tail[j]).astype(BF16),
                v=vb[j].astype(BF16), bonus=bonus[j], sz=sz[j], g_all=jnp.exp(glast[j]))

    units = [(j, p) for j in seqs for p in range(npair)]
    us = range(len(units))

    def stage_steps(ops):
        kkg, rg, bh, kh, kt, bt, v = (
            [ops[j][k][:, p * LANES:(p + 1) * LANES] for j, p in units] for k in _RWKV_OPERANDS)
        g_all = [ops[j]["g_all"][:, p * LANES:(p + 1) * LANES] for j, p in units]
        lhs = [jnp.concatenate([kkg[u], rg[u]], axis=0) for u in us]
        rhs = [jnp.concatenate([bd_ch(bh[u]), bd_ch(kh[u])], axis=0) for u in us]
        big = [dot(lhs[u], rhs[u], NT) for u in us]
        s_old = [sbd_ref[j, p] for j, p in units]
        from_s = [dot(lhs[u], s_old[u].astype(BF16), NT) for u in us]
        low = [jnp.where(strict, x[0:c, 0:w2], 0.0) for x in big]
        m_kv = [jnp.where(strict, x[0:c, w2:2 * w2], 0.0) for x in big]
        q_b = [jnp.where(tril, x[c:2 * c, 0:w2], 0.0) for x in big]
        p_kv = [jnp.where(tril, x[c:2 * c, w2:2 * w2], 0.0) for x in big]
        mp = [dot(jnp.concatenate([m_kv[u], p_kv[u]], axis=0).astype(BF16), bd_ch(v[u]))
              for u in us]
        nn = [-jnp.where(levels[0], x, 0.0) for x in low]
        for m in levels[1:]:
            off = [jnp.where(m, x, 0.0) for x in low]
            nnb = [x.astype(BF16) for x in nn]
            xc = [off[u] + dot(nnb[u], bd_tok(off[u].astype(BF16))) for u in us]
            nn = [nn[u] - (xc[u] + dot(xc[u].astype(BF16), bd_tok(nnb[u]))) for u in us]
        rhs_sa = [from_s[u][0:c] + mp[u][0:c] for u in us]
        sa = [rhs_sa[u] + dot(nn[u].astype(BF16), bd_ch(rhs_sa[u].astype(BF16))) for u in us]
        sab = [x.astype(BF16) for x in sa]
        qs = [dot(q_b[u].astype(BF16), bd_ch(sab[u])) for u in us]
        upd = [dot(jnp.concatenate([v[u], -sab[u]], axis=0),
                   jnp.concatenate([kt[u], bt[u]], axis=0), TN) for u in us]
        s_new = [s_old[u] * g_all[u] + jnp.where(same_head, upd[u], 0.0) for u in us]
        for u, (j, p) in enumerate(units):
            sbd_ref[j, p] = s_new[u]

        @pl.when(step == pl.num_programs(1) - 1)
        def _():
            for u, (j, p) in enumerate(units):
                s_ref[j, 2 * p] = s_new[u][0:n, 0:n]
                s_ref[j, 2 * p + 1] = s_new[u][n:2 * n, n:2 * n]

        y = [from_s[u][c:2 * c] + mp[u][c:2 * c] - qs[u] for u in us]
        yj = [jnp.concatenate(y[j * npair:(j + 1) * npair], axis=1) for j in seqs]
        yc = [yj[j] - seg_sum(yj[j]) * (1.0 / n) for j in seqs]
        var = [seg_sum(yc[j] * yc[j]) * (1.0 / n) for j in seqs]
        for j in seqs:
            yn = yc[j] * lax.rsqrt(var[j] + n * GN_EPS_PER_CH) * gnw_ref[...] + gnb_ref[...]
            o_ref[j] = ((yn + ops[j]["bonus"]) * ops[j]["sz"]).astype(BF16)

    ops = [dict() for _ in seqs]
    pre_steps(ops)
    stage_steps(ops)

    for j in seqs:
        prev_ref[j] = x_ref[j, c - SUBLANES:c, :]
        prevwa_ref[j] = wa_ref[j, c - SUBLANES:c, :]


def _rwkv_parts(p, s0, prev0, prevwa0, mu, muwa, w0, w2, a0, a2, kk, ka, rk, gnw, gnb,
                *, c, ns, t_valid, lay):
    nb, t, _ = p.shape
    nh, n = s0.shape[1:3]
    d = nh * n
    wx = lay["rkvz"][1]
    const2 = lambda b, i: (0, 0)
    vec = pl.BlockSpec((1, d), const2)
    assert 2 * n == LANES and c <= n and c & (c - 1) == 0, "two heads per lane group"
    return dict(
        operands=[p, p, s0, prev0, prevwa0, mu, muwa, w0, w2, a0, a2, kk, ka, rk, gnw, gnb],
        in_specs=[
            pl.BlockSpec((ns, c, wx), lambda b, i: (b, i, lay["rkvz"][0] // wx)),
            pl.BlockSpec((ns, c, LANES), lambda b, i: (b, i, lay["wa"][0] // LANES)),
            *_state_specs(s0, ns, [prev0, prevwa0]),
            pl.BlockSpec((1, wx), const2),
            pl.BlockSpec((1, LANES), const2),
            vec,
            pl.BlockSpec(w2.shape, const2),
            vec,
            pl.BlockSpec(a2.shape, const2),
            vec, vec, vec, vec, vec,
        ],
        out_shape=[jax.ShapeDtypeStruct((nb, t, d), BF16),
                   jax.ShapeDtypeStruct((nb, nh, n, n), F32)],
        out_specs=[pl.BlockSpec((ns, c, d), lambda b, i: (b, i, 0)),
                   pl.BlockSpec((ns, nh, n, n), lambda b, i: (b, 0, 0, 0))],
        scratch=[pltpu.VMEM((ns, SUBLANES, wx), F32), pltpu.VMEM((ns, SUBLANES, LANES), F32),
                 pltpu.VMEM((ns, nh // 2, LANES, LANES), F32)],
        statics=dict(c=c, ns=ns, nh=nh, n=n, t_valid=t_valid, t_total=t))


def _recurrence(body, parts, *, name):
    statics = parts["statics"]
    nb, t = parts["out_shape"][0].shape[:2]
    return pl.pallas_call(
        functools.partial(body, **statics),
        out_shape=tuple(parts["out_shape"]),
        grid=(nb // statics["ns"], t // statics["c"]),
        in_specs=parts["in_specs"],
        out_specs=tuple(parts["out_specs"]),
        scratch_shapes=parts["scratch"],
        compiler_params=pltpu.CompilerParams(
            dimension_semantics=("parallel", "arbitrary"), vmem_limit_bytes=VMEM_LIMIT),
        name=name,
    )(*parts["operands"])


def _merge_kernel(oa_ref, ob_ref, gate_ref, x_ref, woa_ref, wob_ref, wo_ref, lnf_ref, y_ref):
    d = x_ref.shape[1]
    ba = jnp.dot(oa_ref[...], woa_ref[...], preferred_element_type=F32)
    bb = jnp.dot(ob_ref[...], wob_ref[...], preferred_element_type=F32)
    gates = _sigmoid(gate_ref[...])
    merged = gates[:, :d] * ba + gates[:, d:] * bb
    xn = x_ref[...] + jnp.dot(merged.astype(BF16), wo_ref[...], preferred_element_type=F32)
    y_ref[...] = xn * lax.rsqrt(jnp.mean(xn * xn, axis=-1, keepdims=True) + EPS) * lnf_ref[...]


def _merge(oa, ob, p, x, woa, wob, wo, lnf, *, lay):
    m, d = x.shape
    tm = min(m, 512)
    wg = lay["gate"][1]
    const2 = lambda i: (0, 0)
    return pl.pallas_call(
        _merge_kernel,
        out_shape=jax.ShapeDtypeStruct((m, d), F32),
        grid=(m // tm,),
        in_specs=[
            pl.BlockSpec((tm, oa.shape[1]), lambda i: (i, 0)),
            pl.BlockSpec((tm, ob.shape[1]), lambda i: (i, 0)),
            pl.BlockSpec((tm, wg), lambda i: (i, lay["gate"][0] // wg)),
            pl.BlockSpec((tm, d), lambda i: (i, 0)),
            pl.BlockSpec(woa.shape, const2),
            pl.BlockSpec(wob.shape, const2),
            pl.BlockSpec(wo.shape, const2),
            pl.BlockSpec((1, d), const2),
        ],
        out_specs=pl.BlockSpec((tm, d), lambda i: (i, 0)),
        compiler_params=pltpu.CompilerParams(
            dimension_semantics=("parallel",), vmem_limit_bytes=VMEM_LIMIT),
        name="merge",
    )(oa, ob, p, x, woa, wob, wo, lnf)


def _pad_lanes(v, width):
    return jnp.pad(v, ((0, 0), (0, width - v.shape[1])))


def _token_major_t(ab, c):
    nb, t, k = ab.shape
    return jnp.transpose(ab.reshape(nb, t // c, c, k), (0, 1, 3, 2))


def kernel(x_prompt, x_sample, state_gdn, state_gdn_conv, state_rwkv, state_shift, meta_tokens,
           ln1_w, w_in, gdn_conv_w, gdn_a_log, gdn_dt_bias, gdn_norm_w, w_out_a, rwkv_mu, rwkv_w0,
           rwkv_w2, rwkv_a0, rwkv_a2, rwkv_k_k, rwkv_k_a, rwkv_r_k, rwkv_gn_w, rwkv_gn_b, w_out_b,
           w_out, lnf_w):
    assert ln1_w.shape[0] == 1, "single-layer trunk"
    bp, seq, d = x_prompt.shape
    bs, tseq, _ = x_sample.shape
    n_meta = meta_tokens.shape[0]
    _, _, nh_a, dk, dv = state_gdn.shape
    _, _, nh_b, n_b, _ = state_rwkv.shape
    kw, w_qkv = gdn_conv_w.shape[1:]
    lora_w = rwkv_w2.shape[1]
    lora_a = rwkv_a2.shape[1]
    d_a = nh_a * dv
    d_b = nh_b * n_b
    assert w_qkv == 2 * nh_a * dk + d_a and d_a == d and d_b == d and lora_w + lora_a == LANES
    assert n_meta % SUBLANES == 0 and n_meta % 16 == 0 and tseq >= kw - 1

    o_a = w_qkv
    o_b = o_a + nh_a
    o_z = o_b + nh_a
    o_r = o_z + d_a
    o_g = o_r + 3 * d_b + lora_w + lora_a + d_b
    w = w_in[0].T
    wr = w[o_r:o_g]
    mu = rwkv_mu
    lora0 = 3 * d_b
    rows = [w[:w_qkv], w[o_z:o_r], wr[:lora0], wr[lora0 + LANES:], w[o_g:],
            wr[lora0:lora0 + LANES], w[o_a:o_z]]
    used = sum(rw.shape[0] for rw in rows)
    n_pad = -(-(used + LANES - 2 * nh_a) // PROJ_TN) * PROJ_TN
    rows.append(jnp.zeros((n_pad - used, d), F32))
    w_all = jnp.concatenate([rw.astype(BF16) for rw in rows], axis=0)
    lay = {"qkv": (0, w_qkv), "za": (w_qkv, d_a), "rkvz": (w_qkv + d_a, 4 * d_b),
           "gate": (w_qkv + d_a + 4 * d_b, 2 * d)}
    lay["wa"] = (lay["gate"][0] + 2 * d, LANES)
    lay["ab"] = (lay["wa"][0] + LANES, LANES)
    for off, width in lay.values():
        assert off % width == 0
    mu_x = jnp.concatenate([mu[:, :lora0], mu[:, lora0 + LANES:]], axis=1)
    mu_wa = mu[:, lora0:lora0 + LANES]

    alr = _pad_lanes(gdn_a_log, LANES)
    dtr = _pad_lanes(gdn_dt_bias, LANES)
    alc = jnp.pad(gdn_a_log.reshape(nh_a, 1), ((0, nh_a), (0, 0)))
    dtc = jnp.pad(gdn_dt_bias.reshape(nh_a, 1), ((0, nh_a), (0, 0)))
    convw = gdn_conv_w[0]
    rk = rwkv_r_k.reshape(1, d_b)
    woa = w_out_a[0].astype(BF16)
    wob = w_out_b[0].astype(BF16)
    wo = w_out[0].astype(BF16)
    lnf = lnf_w.reshape(1, d)

    def branches(p, c, ns_a, ns_b, t_valid, s_gdn, conv_tail, s_rwkv, x_tail, wa_tail):
        abt = _token_major_t(p[:, :, lay["ab"][0]:lay["ab"][0] + 2 * nh_a], c)
        oa, sg = _recurrence(_gdn_kernel, _gdn_parts(
            p, abt, s_gdn, conv_tail, convw, alr, dtr, alc, dtc, gdn_norm_w,
            c=c, ns=ns_a, t_valid=t_valid, lay=lay), name="gdn")
        ob, sr = _recurrence(_rwkv_kernel, _rwkv_parts(
            p, s_rwkv, x_tail, wa_tail, mu_x, mu_wa, rwkv_w0, rwkv_w2[0], rwkv_a0, rwkv_a2[0],
            rwkv_k_k, rwkv_k_a, rk, rwkv_gn_w, rwkv_gn_b,
            c=c, ns=ns_b, t_valid=t_valid, lay=lay), name="rwkv")
        return oa, sg, ob, sr

    x0, wx = lay["rkvz"]
    a0_, _ = lay["wa"]

    tpad = SUBLANES
    assert n_meta <= bs and tseq < tpad
    rider = jnp.zeros((bs, tpad - tseq, d), F32).at[:n_meta, 0].set(meta_tokens)
    xs = jnp.concatenate([x_sample, rider], axis=1).reshape(bs * tpad, d)
    p_s = _proj(xs, ln1_w, w_all, apply_norm=True)
    p_m = p_s.reshape(bs, tpad, n_pad)[:n_meta, tseq]

    _, sg_m, _, sr_m = branches(
        p_m[None], n_meta, 1, 1, n_meta,
        jnp.zeros((1, nh_a, dk, dv), F32), jnp.zeros((1, 1, w_qkv), F32),
        jnp.zeros((1, nh_b, n_b, n_b), F32), jnp.zeros((1, 1, wx), F32),
        jnp.zeros((1, 1, LANES), F32))

    xp = x_prompt.reshape(bp * seq, d)
    p_p = _proj(xp, ln1_w, w_all, apply_norm=True)
    last_m = p_m[n_meta - 1:, None]
    oa_p, sg_p, ob_p, sr_p = branches(
        p_p.reshape(bp, seq, n_pad), 64, 4, 2, seq, sg_m,
        p_m[n_meta - (kw - 1):, None, :w_qkv], sr_m,
        last_m[:, :, x0:x0 + wx], last_m[:, :, a0_:a0_ + LANES])
    y_p = _merge(oa_p.reshape(bp * seq, d), ob_p.reshape(bp * seq, d), p_p, xp, woa, wob, wo, lnf,
                 lay=lay)

    p_first =_proj(state_shift[0], ln1_w, w_all, apply_norm=False)[None]
    oa_s, sg_s, ob_s, sr_s = branches(
        p_s.reshape(bs, tpad, n_pad), tpad, 8, 8, tseq, state_gdn[0],
        jnp.transpose(state_gdn_conv[0], (1, 0, 2)), state_rwkv[0],
        p_first[:, :, x0:x0 + wx], p_first[:, :, a0_:a0_ + LANES])
    y_s = _merge(oa_s.reshape(bs * tpad, d), ob_s.reshape(bs * tpad, d), p_s, xs, woa, wob, wo,
                 lnf, lay=lay)

    shift_p = _rmsnorm_rows(x_prompt[:, -1], ln1_w)
    shift_s = _rmsnorm_rows(x_sample[:, -1], ln1_w)
    conv_p = p_p.reshape(bp, seq, n_pad)[:, seq - (kw - 1):, :w_qkv]
    conv_s = p_s.reshape(bs, tpad, n_pad)[:, tseq - (kw - 1):tseq, :w_qkv]
    return (y_p.reshape(bp, seq, d), y_s.reshape(bs, tpad, d)[:, :tseq],
            sg_p[None], conv_p[None], sr_p[None], shift_p[None],
            sg_s[None], conv_s[None], sr_s[None], shift_s[None])
```

```python
import functools

import jax
import jax.numpy as jnp
from jax import lax
from jax.experimental import pallas as pl
from jax.experimental.pallas import tpu as pltpu

F32 = jnp.float32
BF16 = jnp.bfloat16

EPS = 1e-6
L2_EPS = 1e-6
GN_EPS_PER_CH = 1e-5

LANES = 128
SUBLANES = 8
VMEM_LIMIT = 48 * 1024 * 1024
PROJ_TM = 1024
PROJ_TN = 1536

NN = (((1,), (0,)), ((), ()))
NT = (((1,), (1,)), ((), ()))
TN = (((0,), (0,)), ((), ()))


def _mm(a, b, dims=NN):
    return lax.dot_general(a.astype(BF16), b.astype(BF16), dims, preferred_element_type=F32)


def _mm_f32(a, b, dims=NN):
    return lax.dot_general(a, b, dims, precision=lax.Precision.HIGHEST,
                           preferred_element_type=F32)


def _sigmoid(x):
    return 1.0 / (1.0 + jnp.exp(-x))


def _silu(x):
    return x * _sigmoid(x)


def _softplus(x):
    return jnp.maximum(x, 0.0) + jnp.log(1.0 + jnp.exp(-jnp.abs(x)))


def _shift_rows(cur, prev, k):
    rolled = pltpu.roll(cur, k, 0)
    fix = pltpu.roll(prev, k, 0)
    wrap = lax.broadcasted_iota(jnp.int32, fix.shape, 0) < k
    head = jnp.where(wrap, fix, rolled[0:SUBLANES])
    if cur.shape[0] == SUBLANES:
        return head
    return jnp.concatenate([head, rolled[SUBLANES:]], axis=0)


class _PairMasks:
    def __init__(self, c):
        row = lax.broadcasted_iota(jnp.int32, (c, 2 * c), 0)
        lane = lax.broadcasted_iota(jnp.int32, (c, 2 * c), 1)
        col = lane & (c - 1)
        self.tril = row >= col
        self.strict = row > col
        diff = row ^ col
        self.levels = [(diff >> 1) == 0]
        sh = 2
        while (1 << sh) <= c:
            self.levels.append((diff >> (sh - 1)) == 1)
            sh += 1
        self.head0 = lane < c
        self.tok0 = self.head0.astype(F32).astype(BF16)
        self.tok1 = (lane >= c).astype(F32).astype(BF16)

    def block_diag(self, xb):
        return jnp.concatenate([xb * self.tok0, xb * self.tok1], axis=0)


def _dotb(a, b, dims=NN):
    return lax.dot_general(a, b, dims, preferred_element_type=F32)


def _tri_inv_pairs(lows, pm):
    nn = [-jnp.where(pm.levels[0], x, 0.0) for x in lows]
    for m in pm.levels[1:]:
        off = [jnp.where(m, x, 0.0) for x in lows]
        nnb = [x.astype(BF16) for x in nn]
        xc = [o + _dotb(xb, pm.block_diag(o.astype(BF16))) for o, xb in zip(off, nnb)]
        nn = [x - (y + _dotb(y.astype(BF16), pm.block_diag(xb)))
              for x, y, xb in zip(nn, xc, nnb)]
    return nn


def _proj_kernel(x_ref, lnw_ref, w_ref, o_ref, h_ref, *, apply_norm):
    @pl.when(pl.program_id(1) == 0)
    def _():
        x = x_ref[...]
        if apply_norm:
            x = x * lax.rsqrt(jnp.mean(x * x, axis=-1, keepdims=True) + EPS)
            x = x * lnw_ref[...]
        h_ref[...] = x.astype(BF16)

    o_ref[...] = lax.dot_general(h_ref[...], w_ref[...], NT, preferred_element_type=F32)


def _proj(x, lnw, w, *, apply_norm):
    m, d = x.shape
    n = w.shape[0]
    tm = min(m, PROJ_TM)
    tn = PROJ_TN
    assert m % tm == 0 and n % tn == 0
    vmem = 2 * (tm * d * 4 + tn * d * 2 + tm * tn * 4) + tm * d * 2 + (4 << 20)
    return pl.pallas_call(
        functools.partial(_proj_kernel, apply_norm=apply_norm),
        out_shape=jax.ShapeDtypeStruct((m, n), F32),
        grid=(m // tm, n // tn),
        in_specs=[pl.BlockSpec((tm, d), lambda i, j: (i, 0)),
                  pl.BlockSpec((1, d), lambda i, j: (0, 0)),
                  pl.BlockSpec((tn, d), lambda i, j: (j, 0))],
        out_specs=pl.BlockSpec((tm, tn), lambda i, j: (i, j)),
        scratch_shapes=[pltpu.VMEM((tm, d), BF16)],
        compiler_params=pltpu.CompilerParams(
            dimension_semantics=("parallel", "arbitrary"), vmem_limit_bytes=vmem),
        name="proj",
    )(x, lnw, w)


def _rmsnorm_kernel(x_ref, w_ref, o_ref):
    x = x_ref[...]
    o_ref[...] = x * lax.rsqrt(jnp.mean(x * x, axis=-1, keepdims=True) + EPS) * w_ref[...]


def _rmsnorm_rows(x, w):
    return pl.pallas_call(
        _rmsnorm_kernel,
        out_shape=jax.ShapeDtypeStruct(x.shape, F32),
        name="rmsnorm_rows",
    )(x, w)


def _gdn_kernel(qkv_ref, z_ref, ab_ref, abt_ref, s0_ref, prev0_ref, convw_ref,
                alr_ref, dtr_ref, alc_ref, dtc_ref, nw_ref,
                o_ref, s_ref, prev_ref, *, c, ns, nh, dk, dv, t_valid, t_total):
    ci = pl.program_id(1)
    shared = s0_ref.shape[0] == 1

    @pl.when(ci == 0)
    def _():
        for j in range(ns):
            s_ref[j] = s0_ref[0 if shared else j]
            _init_carry(prev_ref, prev0_ref, j, 0 if shared else j)

    pm = _PairMasks(c)
    row = lax.broadcasted_iota(jnp.int32, (c, c), 0)
    col = lax.broadcasted_iota(jnp.int32, (c, c), 1)
    tril = row >= col
    kw = convw_ref.shape[0]
    seqs = range(ns)

    ab = [ab_ref[j] for j in seqs]
    g_c = [-jnp.exp(alr_ref[...]) * _softplus(x + dtr_ref[...]) for x in ab]
    beta_c = [_sigmoid(x) for x in ab]
    abt = [abt_ref[j, 0] for j in seqs]
    g_r = [-jnp.exp(alc_ref[...]) * _softplus(x + dtc_ref[...]) for x in abt]
    if t_valid < t_total:
        tok_c = ci * c + lax.broadcasted_iota(jnp.int32, ab[0].shape, 0)
        tok_r = ci * c + lax.broadcasted_iota(jnp.int32, abt[0].shape, 1)
        g_c = [jnp.where(tok_c < t_valid, x, 0.0) for x in g_c]
        beta_c = [jnp.where(tok_c < t_valid, x, 0.0) for x in beta_c]
        g_r = [jnp.where(tok_r < t_valid, x, 0.0) for x in g_r]
    gc_c = _mm_f32(tril.astype(F32), jnp.concatenate(g_c, axis=1))
    gc_r = _mm_f32(jnp.concatenate(g_r, axis=0), (row <= col).astype(F32))

    def conv(j, col0, width):
        cur = qkv_ref[j, :, col0:col0 + width]
        prev = prev_ref[j, :, col0:col0 + width]
        acc = None
        for k in range(kw - 1, -1, -1):
            src = cur if k == 0 else _shift_rows(cur, prev, k)
            term = src * convw_ref[kw - 1 - k:kw - k, col0:col0 + width]
            acc = term if acc is None else acc + term
        return _silu(acc)

    units = [(j, h) for j in seqs for h in range(nh)]
    us = range(len(units))
    gcc = [gc_c[:, j * LANES + h:j * LANES + h + 1] for j, h in units]
    gcr = [gc_r[j * 2 * nh + h:j * 2 * nh + h + 1, :] for j, h in units]
    beta = [beta_c[j][:, nh + h:nh + h + 1] for j, h in units]
    glast = [x[c - 1:c, :] for x in gcc]
    q = [conv(j, h * dk, dk) for j, h in units]
    k = [conv(j, nh * dk + h * dk, dk) for j, h in units]
    v = [conv(j, 2 * nh * dk + h * dv, dv) for j, h in units]
    q = [x * lax.rsqrt(jnp.sum(x * x, axis=-1, keepdims=True) + L2_EPS) * (dk ** -0.5) for x in q]
    k = [x * lax.rsqrt(jnp.sum(x * x, axis=-1, keepdims=True) + L2_EPS) for x in k]
    kb = [k[u] * beta[u] for u in us]
    vb = [v[u] * beta[u] for u in us]
    kbg = [kb[u] * jnp.exp(gcc[u]) for u in us]
    qg = [q[u] * jnp.exp(gcc[u]) for u in us]
    k_tail = [k[u] * jnp.exp(glast[u] - gcc[u]) for u in us]

    pairs = [(u, u + 1) for u in us if u % 2 == 0]

    def side_by_side(xs):
        return [jnp.concatenate([xs[a].astype(BF16), xs[b].astype(BF16)], axis=1) for a, b in pairs]

    def per_head_rows(xs):
        out = []
        for a, b in pairs:
            xa, xb = xs[a].astype(BF16), xs[b].astype(BF16)
            out.append(jnp.concatenate([jnp.concatenate([xa, jnp.zeros_like(xb)], axis=1),
                                        jnp.concatenate([jnp.zeros_like(xa), xb], axis=1)], axis=0))
        return out

    def split(xs, width):
        return [x[:, i * width:(i + 1) * width] for x in xs for i in range(2)]

    gcc_s = [jnp.where(pm.head0, gcc[a], gcc[b]) for a, b in pairs]
    gcr_s = [jnp.concatenate([gcr[a], gcr[b]], axis=1) for a, b in pairs]
    decay = [jnp.where(pm.tril, jnp.exp(jnp.where(pm.tril, x - y, 0.0)), 0.0)
             for x, y in zip(gcc_s, gcr_s)]
    k_rows = per_head_rows(k)
    lhs = [jnp.concatenate([x, y], axis=0) for x, y in zip(side_by_side(kb), side_by_side(q))]
    both = [_dotb(x, y, NT) for x, y in zip(lhs, k_rows)]
    lower = [jnp.where(pm.strict, x[0:c] * d_, 0.0) for x, d_ in zip(both, decay)]
    qk = [jnp.where(pm.tril, x[c:2 * c] * d_, 0.0) for x, d_ in zip(both, decay)]
    s_old = [s_ref[j, h] for j, h in units]
    qs = [_mm(qg[u], s_old[u]) for u in us]
    nn = [x.astype(BF16) for x in _tri_inv_pairs(lower, pm)]
    u_ = [vb[u] + x for u, x in enumerate(split(
        [_dotb(x, y) for x, y in zip(nn, per_head_rows(vb))], dv))]
    w_ = [kbg[u] + x for u, x in enumerate(split(
        [_dotb(x, y) for x, y in zip(nn, per_head_rows(kbg))], dk))]
    v_new = [u_[u] - _mm(w_[u], s_old[u]) for u in us]
    kv = [_mm(k_tail[u], v_new[u], TN) for u in us]
    for u, (j, h) in enumerate(units):
        s_ref[j, h] = s_old[u] * jnp.exp(glast[u]) + kv[u]
    o = [qs[u] + x for u, x in enumerate(split(
        [_dotb(x.astype(BF16), y) for x, y in zip(qk, per_head_rows(v_new))], dv))]
    for u, (j, h) in enumerate(units):
        oh = o[u] * lax.rsqrt(jnp.mean(o[u] * o[u], axis=-1, keepdims=True) + EPS) * nw_ref[...]
        oh = oh * _silu(z_ref[j, :, h * dv:(h + 1) * dv])
        o_ref[j, :, h * dv:(h + 1) * dv] = oh.astype(BF16)

    for j in seqs:
        prev_ref[j] = qkv_ref[j, c - SUBLANES:c, :]


def _state_specs(s0, ns, tails):
    shared = s0.shape[0] == 1
    lead = 1 if shared else ns
    idx4 = (lambda b, i: (0, 0, 0, 0)) if shared else (lambda b, i: (b, 0, 0, 0))
    idx3 = (lambda b, i: (0, 0, 0)) if shared else (lambda b, i: (0, b, 0))
    return ([pl.BlockSpec((lead,) + s0.shape[1:], idx4)]
            + [pl.BlockSpec((tl.shape[0], lead, tl.shape[2]), idx3) for tl in tails])


def _init_carry(prev_ref, tail_ref, j, jj):
    k = tail_ref.shape[0]
    if k < SUBLANES:
        prev_ref[j] = jnp.zeros(prev_ref.shape[1:], prev_ref.dtype)
    for r in range(k):
        row = SUBLANES - k + r
        prev_ref[j, row:row + 1, :] = tail_ref[r, jj:jj + 1, :]


def _gdn_parts(p, abt, s0, prev0, convw, alr, dtr, alc, dtc, nw, *, c, ns, t_valid, lay):
    nb, t, _ = p.shape
    nh, dk, dv = s0.shape[1:]
    wq = lay["qkv"][1]
    const2 = lambda b, i: (0, 0)
    return dict(
        operands=[p, p, p, abt, s0, prev0, convw, alr, dtr, alc, dtc, nw],
        in_specs=[
            pl.BlockSpec((ns, c, wq), lambda b, i: (b, i, lay["qkv"][0] // wq)),
            pl.BlockSpec((ns, c, nh * dv), lambda b, i: (b, i, lay["za"][0] // (nh * dv))),
            pl.BlockSpec((ns, c, LANES), lambda b, i: (b, i, lay["ab"][0] // LANES)),
            pl.BlockSpec((ns, 1, 2 * nh, c), lambda b, i: (b, i, 0, 0)),
            *_state_specs(s0, ns, [prev0]),
            pl.BlockSpec(convw.shape, const2),
            pl.BlockSpec(alr.shape, const2),
            pl.BlockSpec(dtr.shape, const2),
            pl.BlockSpec(alc.shape, const2),
            pl.BlockSpec(dtc.shape, const2),
            pl.BlockSpec(nw.shape, const2),
        ],
        out_shape=[jax.ShapeDtypeStruct((nb, t, nh * dv), BF16),
                   jax.ShapeDtypeStruct((nb, nh, dk, dv), F32)],
        out_specs=[pl.BlockSpec((ns, c, nh * dv), lambda b, i: (b, i, 0)),
                   pl.BlockSpec((ns, nh, dk, dv), lambda b, i: (b, 0, 0, 0))],
        scratch=[pltpu.VMEM((ns, SUBLANES, wq), F32)],
        statics=dict(c=c, ns=ns, nh=nh, dk=dk, dv=dv, t_valid=t_valid, t_total=t))


def _token_shift(x_ref, prev_ref, mu_ref, j, lo, hi):
    cur = x_ref[j, :, lo:hi]
    prv = _shift_rows(cur, prev_ref[j, :, lo:hi], 1)
    return cur + mu_ref[:, lo:hi] * (prv - cur)


_RWKV_OPERANDS = ("kkg", "rg", "bh", "kh", "kt", "bt", "v")


def _rwkv_kernel(x_ref, wa_ref, s0_ref, prev0_ref, prevwa0_ref, mu_ref, muwa_ref,
                 w0_ref, w2_ref, a0_ref, a2_ref, kk_ref, ka_ref, rk_ref, gnw_ref, gnb_ref,
                 o_ref, s_ref, prev_ref, prevwa_ref, sbd_ref,
                 *, c, ns, nh, n, t_valid, t_total):
    step = pl.program_id(1)
    d = nh * n
    npair = nh // 2
    ngrp = d // LANES
    shared = s0_ref.shape[0] == 1
    seqs = range(ns)

    @pl.when(step == 0)
    def _():
        zero = jnp.zeros((n, n), F32)
        for j in seqs:
            jj = 0 if shared else j
            for p in range(npair):
                top = jnp.concatenate([s0_ref[jj, 2 * p], zero], axis=1)
                bot = jnp.concatenate([zero, s0_ref[jj, 2 * p + 1]], axis=1)
                sbd_ref[j, p] = jnp.concatenate([top, bot], axis=0)
            _init_carry(prev_ref, prev0_ref, j, jj)
            _init_carry(prevwa_ref, prevwa0_ref, j, jj)

    w2 = 2 * c
    pm = _PairMasks(c)
    tril, strict = pm.tril, pm.strict
    chl = lax.broadcasted_iota(jnp.int32, (c, LANES), 1)
    ch0 = (chl < n).astype(F32).astype(BF16)
    ch1 = (chl >= n).astype(F32).astype(BF16)
    row2 = lax.broadcasted_iota(jnp.int32, (LANES, LANES), 0)
    lane2 = lax.broadcasted_iota(jnp.int32, (LANES, LANES), 1)
    log2n = n.bit_length() - 1
    same_head = (row2 >> log2n) == (lane2 >> log2n)
    seg_ones = same_head.astype(F32).astype(BF16)

    def bd_ch(xb):
        return jnp.concatenate([xb * ch0, xb * ch1], axis=0)

    dot = _dotb

    def seg_sum(x):
        stk = jnp.concatenate([x[:, i * LANES:(i + 1) * LANES] for i in range(ngrp)], axis=0)
        acc = dot(stk.astype(BF16), seg_ones)
        return jnp.concatenate([acc[i * c:(i + 1) * c] for i in range(ngrp)], axis=1)

    lora = w2_ref.shape[0]
    row_c = lax.broadcasted_iota(jnp.int32, (c, c), 0)
    col_c = lax.broadcasted_iota(jnp.int32, (c, c), 1)
    tril_b = (row_c >= col_c).astype(BF16)

    def pre_steps(out):
        r = [_token_shift(x_ref, prev_ref, mu_ref, j, 0, d) for j in seqs]
        kb = [_token_shift(x_ref, prev_ref, mu_ref, j, d, 2 * d) for j in seqs]
        vb = [_token_shift(x_ref, prev_ref, mu_ref, j, 2 * d, 3 * d) for j in seqs]
        sz = [_silu(_token_shift(x_ref, prev_ref, mu_ref, j, 3 * d, 4 * d)) for j in seqs]
        wa = [wa_ref[j] for j in seqs]
        wa = [wa[j] + muwa_ref[...] * (_shift_rows(wa[j], prevwa_ref[j], 1) - wa[j])
              for j in seqs]
        tw = [jnp.tanh(x[:, 0:lora]) for x in wa]
        w_raw = [w0_ref[...] + _mm(tw[j], w2_ref[...]) for j in seqs]
        a = [_sigmoid(a0_ref[...] + _mm(wa[j][:, lora:], a2_ref[...])) for j in seqs]
        w_log = [-jnp.exp(-_softplus(-x) - 0.5) for x in w_raw]
        kku = [kb[j] * kk_ref[...] for j in seqs]
        k2 = [kb[j] * (1.0 + (a[j] - 1.0) * ka_ref[...]) for j in seqs]
        if t_valid < t_total:
            ok = (step * c + lax.broadcasted_iota(jnp.int32, (c, d), 0)) < t_valid
            w_log = [jnp.where(ok, x, 0.0) for x in w_log]
            kku = [jnp.where(ok, x, 0.0) for x in kku]
            vb = [jnp.where(ok, x, 0.0) for x in vb]
        w_hi = [x.astype(BF16) for x in w_log]
        rem = [w_log[j] - w_hi[j].astype(F32) for j in seqs]
        w_mid = [x.astype(BF16) for x in rem]
        w_lo = [(rem[j] - w_mid[j].astype(F32)).astype(BF16) for j in seqs]
        gcum = [dot(tril_b, w_hi[j]) + dot(tril_b, w_mid[j]) + dot(tril_b, w_lo[j]) for j in seqs]
        glast = [x[c - 1:c, :] for x in gcum]
        g_in = [jnp.exp(x) for x in gcum]
        g_ex = [jnp.exp(gcum[j] - w_log[j]) for j in seqs]
        g_inv = [jnp.exp(-x) for x in gcum]
        g_tail = [jnp.exp(glast[j] - gcum[j]) for j in seqs]
        ssq = [x * x for x in kku]
        bonus_in = [r[j] * k2[j] * rk_ref[...] for j in seqs]
        kk = [kku[j] * lax.rsqrt(seg_sum(ssq[j]) + L2_EPS) for j in seqs]
        bonus = [seg_sum(bonus_in[j]) * vb[j] for j in seqs]
        for j in seqs:
            b = kk[j] * a[j]
            out[j].update(
                kkg=(kk[j] * g_ex[j]).astype(BF16), rg=(r[j] * g_in[j]).astype(BF16),
                bh=(b * g_inv[j]).astype(BF16), kh=(k2[j] * g_inv[j]).astype(BF16),
                kt=(k2[j] * g_tail[j]).astype(BF16), bt=(b * g_tail[j]).astype(BF16),
                v=vb[j].astype(BF16), bonus=bonus[j], sz=sz[j], g_all=jnp.exp(glast[j]))

    units = [(j, p) for j in seqs for p in range(npair)]
    us = range(len(units))

    def stage_steps(ops):
        kkg, rg, bh, kh, kt, bt, v = (
            [ops[j][k][:, p * LANES:(p + 1) * LANES] for j, p in units] for k in _RWKV_OPERANDS)
        g_all = [ops[j]["g_all"][:, p * LANES:(p + 1) * LANES] for j, p in units]
        lhs = [jnp.concatenate([kkg[u], rg[u]], axis=0) for u in us]
        rhs = [jnp.concatenate([bd_ch(bh[u]), bd_ch(kh[u])], axis=0) for u in us]
        big = [dot(lhs[u], rhs[u], NT) for u in us]
        s_old = [sbd_ref[j, p] for j, p in units]
        from_s = [dot(lhs[u], s_old[u].astype(BF16), NT) for u in us]
        low = [jnp.where(strict, x[0:c, 0:w2], 0.0) for x in big]
        m_kv = [jnp.where(strict, x[0:c, w2:2 * w2], 0.0) for x in big]
        q_b = [jnp.where(tril, x[c:2 * c, 0:w2], 0.0) for x in big]
        p_kv = [jnp.where(tril, x[c:2 * c, w2:2 * w2], 0.0) for x in big]
        mp = [dot(jnp.concatenate([m_kv[u], p_kv[u]], axis=0).astype(BF16), bd_ch(v[u]))
              for u in us]
        nn = _tri_inv_pairs(low, pm)
        rhs_sa = [from_s[u][0:c] + mp[u][0:c] for u in us]
        sa = [rhs_sa[u] + dot(nn[u].astype(BF16), bd_ch(rhs_sa[u].astype(BF16))) for u in us]
        sab = [x.astype(BF16) for x in sa]
        qs = [dot(q_b[u].astype(BF16), bd_ch(sab[u])) for u in us]
        upd = [dot(jnp.concatenate([v[u], -sab[u]], axis=0),
                   jnp.concatenate([kt[u], bt[u]], axis=0), TN) for u in us]
        s_new = [s_old[u] * g_all[u] + jnp.where(same_head, upd[u], 0.0) for u in us]
        for u, (j, p) in enumerate(units):
            sbd_ref[j, p] = s_new[u]

        @pl.when(step == pl.num_programs(1) - 1)
        def _():
            for u, (j, p) in enumerate(units):
                s_ref[j, 2 * p] = s_new[u][0:n, 0:n]
                s_ref[j, 2 * p + 1] = s_new[u][n:2 * n, n:2 * n]

        y = [from_s[u][c:2 * c] + mp[u][c:2 * c] - qs[u] for u in us]
        yj = [jnp.concatenate(y[j * npair:(j + 1) * npair], axis=1) for j in seqs]
        yc = [yj[j] - seg_sum(yj[j]) * (1.0 / n) for j in seqs]
        var = [seg_sum(yc[j] * yc[j]) * (1.0 / n) for j in seqs]
        for j in seqs:
            yn = yc[j] * lax.rsqrt(var[j] + n * GN_EPS_PER_CH) * gnw_ref[...] + gnb_ref[...]
            o_ref[j] = ((yn + ops[j]["bonus"]) * ops[j]["sz"]).astype(BF16)

    ops = [dict() for _ in seqs]
    pre_steps(ops)
    stage_steps(ops)

    for j in seqs:
        prev_ref[j] = x_ref[j, c - SUBLANES:c, :]
        prevwa_ref[j] = wa_ref[j, c - SUBLANES:c, :]


def _rwkv_parts(p, s0, prev0, prevwa0, mu, muwa, w0, w2, a0, a2, kk, ka, rk, gnw, gnb,
                *, c, ns, t_valid, lay):
    nb, t, _ = p.shape
    nh, n = s0.shape[1:3]
    d = nh * n
    wx = lay["rkvz"][1]
    const2 = lambda b, i: (0, 0)
    vec = pl.BlockSpec((1, d), const2)
    assert 2 * n == LANES and c <= n and c & (c - 1) == 0, "two heads per lane group"
    return dict(
        operands=[p, p, s0, prev0, prevwa0, mu, muwa, w0, w2, a0, a2, kk, ka, rk, gnw, gnb],
        in_specs=[
            pl.BlockSpec((ns, c, wx), lambda b, i: (b, i, lay["rkvz"][0] // wx)),
            pl.BlockSpec((ns, c, LANES), lambda b, i: (b, i, lay["wa"][0] // LANES)),
            *_state_specs(s0, ns, [prev0, prevwa0]),
            pl.BlockSpec((1, wx), const2),
            pl.BlockSpec((1, LANES), const2),
            vec,
            pl.BlockSpec(w2.shape, const2),
            vec,
            pl.BlockSpec(a2.shape, const2),
            vec, vec, vec, vec, vec,
        ],
        out_shape=[jax.ShapeDtypeStruct((nb, t, d), BF16),
                   jax.ShapeDtypeStruct((nb, nh, n, n), F32)],
        out_specs=[pl.BlockSpec((ns, c, d), lambda b, i: (b, i, 0)),
                   pl.BlockSpec((ns, nh, n, n), lambda b, i: (b, 0, 0, 0))],
        scratch=[pltpu.VMEM((ns, SUBLANES, wx), F32), pltpu.VMEM((ns, SUBLANES, LANES), F32),
                 pltpu.VMEM((ns, nh // 2, LANES, LANES), F32)],
        statics=dict(c=c, ns=ns, nh=nh, n=n, t_valid=t_valid, t_total=t))


def _recurrence(body, parts, *, name):
    statics = parts["statics"]
    nb, t = parts["out_shape"][0].shape[:2]
    return pl.pallas_call(
        functools.partial(body, **statics),
        out_shape=tuple(parts["out_shape"]),
        grid=(nb // statics["ns"], t // statics["c"]),
        in_specs=parts["in_specs"],
        out_specs=tuple(parts["out_specs"]),
        scratch_shapes=parts["scratch"],
        compiler_params=pltpu.CompilerParams(
            dimension_semantics=("parallel", "arbitrary"), vmem_limit_bytes=VMEM_LIMIT),
        name=name,
    )(*parts["operands"])


def _merge_kernel(oa_ref, ob_ref, gate_ref, x_ref, woa_ref, wob_ref, wo_ref, lnf_ref, y_ref):
    d = x_ref.shape[1]
    ba = jnp.dot(oa_ref[...], woa_ref[...], preferred_element_type=F32)
    bb = jnp.dot(ob_ref[...], wob_ref[...], preferred_element_type=F32)
    gates = _sigmoid(gate_ref[...])
    merged = gates[:, :d] * ba + gates[:, d:] * bb
    xn = x_ref[...] + jnp.dot(merged.astype(BF16), wo_ref[...], preferred_element_type=F32)
    y_ref[...] = xn * lax.rsqrt(jnp.mean(xn * xn, axis=-1, keepdims=True) + EPS) * lnf_ref[...]


def _merge(oa, ob, p, x, woa, wob, wo, lnf, *, lay):
    m, d = x.shape
    tm = min(m, 512)
    wg = lay["gate"][1]
    const2 = lambda i: (0, 0)
    return pl.pallas_call(
        _merge_kernel,
        out_shape=jax.ShapeDtypeStruct((m, d), F32),
        grid=(m // tm,),
        in_specs=[
            pl.BlockSpec((tm, oa.shape[1]), lambda i: (i, 0)),
            pl.BlockSpec((tm, ob.shape[1]), lambda i: (i, 0)),
            pl.BlockSpec((tm, wg), lambda i: (i, lay["gate"][0] // wg)),
            pl.BlockSpec((tm, d), lambda i: (i, 0)),
            pl.BlockSpec(woa.shape, const2),
            pl.BlockSpec(wob.shape, const2),
            pl.BlockSpec(wo.shape, const2),
            pl.BlockSpec((1, d), const2),
        ],
        out_specs=pl.BlockSpec((tm, d), lambda i: (i, 0)),
        compiler_params=pltpu.CompilerParams(
            dimension_semantics=("parallel",), vmem_limit_bytes=VMEM_LIMIT),
        name="merge",
    )(oa, ob, p, x, woa, wob, wo, lnf)


def _pad_lanes(v, width):
    return jnp.pad(v, ((0, 0), (0, width - v.shape[1])))


def _token_major_t(ab, c):
    nb, t, k = ab.shape
    return jnp.transpose(ab.reshape(nb, t // c, c, k), (0, 1, 3, 2))


def kernel(x_prompt, x_sample, state_gdn, state_gdn_conv, state_rwkv, state_shift, meta_tokens,
           ln1_w, w_in, gdn_conv_w, gdn_a_log, gdn_dt_bias, gdn_norm_w, w_out_a, rwkv_mu, rwkv_w0,
           rwkv_w2, rwkv_a0, rwkv_a2, rwkv_k_k, rwkv_k_a, rwkv_r_k, rwkv_gn_w, rwkv_gn_b, w_out_b,
           w_out, lnf_w):
    assert ln1_w.shape[0] == 1, "single-layer trunk"
    bp, seq, d = x_prompt.shape
    bs, tseq, _ = x_sample.shape
    n_meta = meta_tokens.shape[0]
    _, _, nh_a, dk, dv = state_gdn.shape
    _, _, nh_b, n_b, _ = state_rwkv.shape
    kw, w_qkv = gdn_conv_w.shape[1:]
    lora_w = rwkv_w2.shape[1]
    lora_a = rwkv_a2.shape[1]
    d_a = nh_a * dv
    d_b = nh_b * n_b
    assert w_qkv == 2 * nh_a * dk + d_a and d_a == d and d_b == d and lora_w + lora_a == LANES
    assert n_meta % SUBLANES == 0 and n_meta % 16 == 0 and tseq >= kw - 1

    o_a = w_qkv
    o_b = o_a + nh_a
    o_z = o_b + nh_a
    o_r = o_z + d_a
    o_g = o_r + 3 * d_b + lora_w + lora_a + d_b
    w = w_in[0].T
    wr = w[o_r:o_g]
    mu = rwkv_mu
    lora0 = 3 * d_b
    rows = [w[:w_qkv], w[o_z:o_r], wr[:lora0], wr[lora0 + LANES:], w[o_g:],
            wr[lora0:lora0 + LANES], w[o_a:o_z]]
    used = sum(rw.shape[0] for rw in rows)
    n_pad = -(-(used + LANES - 2 * nh_a) // PROJ_TN) * PROJ_TN
    rows.append(jnp.zeros((n_pad - used, d), F32))
    w_all = jnp.concatenate([rw.astype(BF16) for rw in rows], axis=0)
    lay = {"qkv": (0, w_qkv), "za": (w_qkv, d_a), "rkvz": (w_qkv + d_a, 4 * d_b),
           "gate": (w_qkv + d_a + 4 * d_b, 2 * d)}
    lay["wa"] = (lay["gate"][0] + 2 * d, LANES)
    lay["ab"] = (lay["wa"][0] + LANES, LANES)
    for off, width in lay.values():
        assert off % width == 0
    mu_x = jnp.concatenate([mu[:, :lora0], mu[:, lora0 + LANES:]], axis=1)
    mu_wa = mu[:, lora0:lora0 + LANES]

    alr = _pad_lanes(gdn_a_log, LANES)
    dtr = _pad_lanes(gdn_dt_bias, LANES)
    alc = jnp.pad(gdn_a_log.reshape(nh_a, 1), ((0, nh_a), (0, 0)))
    dtc = jnp.pad(gdn_dt_bias.reshape(nh_a, 1), ((0, nh_a), (0, 0)))
    convw = gdn_conv_w[0]
    rk = rwkv_r_k.reshape(1, d_b)
    woa = w_out_a[0].astype(BF16)
    wob = w_out_b[0].astype(BF16)
    wo = w_out[0].astype(BF16)
    lnf = lnf_w.reshape(1, d)

    def branches(p, c, ns_a, ns_b, t_valid, s_gdn, conv_tail, s_rwkv, x_tail, wa_tail):
        abt = _token_major_t(p[:, :, lay["ab"][0]:lay["ab"][0] + 2 * nh_a], c)
        oa, sg = _recurrence(_gdn_kernel, _gdn_parts(
            p, abt, s_gdn, conv_tail, convw, alr, dtr, alc, dtc, gdn_norm_w,
            c=c, ns=ns_a, t_valid=t_valid, lay=lay), name="gdn")
        ob, sr = _recurrence(_rwkv_kernel, _rwkv_parts(
            p, s_rwkv, x_tail, wa_tail, mu_x, mu_wa, rwkv_w0, rwkv_w2[0], rwkv_a0, rwkv_a2[0],
            rwkv_k_k, rwkv_k_a, rk, rwkv_gn_w, rwkv_gn_b,
            c=c, ns=ns_b, t_valid=t_valid, lay=lay), name="rwkv")
        return oa, sg, ob, sr

    x0, wx = lay["rkvz"]
    a0_, _ = lay["wa"]

    tpad = SUBLANES
    assert n_meta <= bs and tseq < tpad
    rider = jnp.zeros((bs, tpad - tseq, d), F32).at[:n_meta, 0].set(meta_tokens)
    xs = jnp.concatenate([x_sample, rider], axis=1).reshape(bs * tpad, d)
    p_s = _proj(xs, ln1_w, w_all, apply_norm=True)
    p_m = p_s.reshape(bs, tpad, n_pad)[:n_meta, tseq]

    _, sg_m, _, sr_m = branches(
        p_m[None], n_meta, 1, 1, n_meta,
        jnp.zeros((1, nh_a, dk, dv), F32), jnp.zeros((1, 1, w_qkv), F32),
        jnp.zeros((1, nh_b, n_b, n_b), F32), jnp.zeros((1, 1, wx), F32),
        jnp.zeros((1, 1, LANES), F32))

    xp = x_prompt.reshape(bp * seq, d)
    p_p = _proj(xp, ln1_w, w_all, apply_norm=True)
    last_m = p_m[n_meta - 1:, None]
    oa_p, sg_p, ob_p, sr_p = branches(
        p_p.reshape(bp, seq, n_pad), 64, 4, 2, seq, sg_m,
        p_m[n_meta - (kw - 1):, None, :w_qkv], sr_m,
        last_m[:, :, x0:x0 + wx], last_m[:, :, a0_:a0_ + LANES])
    y_p = _merge(oa_p.reshape(bp * seq, d), ob_p.reshape(bp * seq, d), p_p, xp, woa, wob, wo, lnf,
                 lay=lay)

    p_first =_proj(state_shift[0], ln1_w, w_all, apply_norm=False)[None]
    oa_s, sg_s, ob_s, sr_s = branches(
        p_s.reshape(bs, tpad, n_pad), tpad, 8, 8, tseq, state_gdn[0],
        jnp.transpose(state_gdn_conv[0], (1, 0, 2)), state_rwkv[0],
        p_first[:, :, x0:x0 + wx], p_first[:, :, a0_:a0_ + LANES])
    y_s = _merge(oa_s.reshape(bs * tpad, d), ob_s.reshape(bs * tpad, d), p_s, xs, woa, wob, wo,
                 lnf, lay=lay)

    shift_p = _rmsnorm_rows(x_prompt[:, -1], ln1_w)
    shift_s = _rmsnorm_rows(x_sample[:, -1], ln1_w)
    conv_p = p_p.reshape(bp, seq, n_pad)[:, seq - (kw - 1):, :w_qkv]
    conv_s = p_s.reshape(bs, tpad, n_pad)[:, tseq - (kw - 1):tseq, :w_qkv]
    return (y_p.reshape(bp, seq, d), y_s.reshape(bs, tpad, d)[:, :tseq],
            sg_p[None], conv_p[None], sr_p[None], shift_p[None],
            sg_s[None], conv_s[None], sr_s[None], shift_s[None])
```

```python
import functools

import jax
import jax.numpy as jnp
from jax import lax
from jax.experimental import pallas as pl
from jax.experimental.pallas import tpu as pltpu

F32 = jnp.float32
BF16 = jnp.bfloat16

EPS = 1e-6
L2_EPS = 1e-6
GN_EPS_PER_CH = 1e-5

LANES = 128
SUBLANES = 8
VMEM_LIMIT = 48 * 1024 * 1024
PROJ_TM = 1024
PROJ_TN = 1536

NN = (((1,), (0,)), ((), ()))
NT = (((1,), (1,)), ((), ()))
TN = (((0,), (0,)), ((), ()))


def _mm(a, b, dims=NN):
    return lax.dot_general(a.astype(BF16), b.astype(BF16), dims, preferred_element_type=F32)


def _dotb(a, b, dims=NN):
    return lax.dot_general(a, b, dims, preferred_element_type=F32)


def _split3(x):
    hi = x.astype(BF16)
    rem = x - hi.astype(F32)
    mid = rem.astype(BF16)
    return hi, mid, (rem - mid.astype(F32)).astype(BF16)


def _cumsum_rows(tril01, x):
    hi, mid, lo = _split3(x)
    return _dotb(tril01, hi) + _dotb(tril01, mid) + _dotb(tril01, lo)


def _cumsum_lanes(x, triu01):
    hi, mid, lo = _split3(x)
    return _dotb(hi, triu01) + _dotb(mid, triu01) + _dotb(lo, triu01)


def _sigmoid(x):
    return 1.0 / (1.0 + jnp.exp(-x))


def _silu(x):
    return x * _sigmoid(x)


def _softplus(x):
    return jnp.maximum(x, 0.0) + jnp.log(1.0 + jnp.exp(-jnp.abs(x)))


def _shift_rows(cur, prev, k):
    rolled = pltpu.roll(cur, k, 0)
    fix = pltpu.roll(prev, k, 0)
    wrap = lax.broadcasted_iota(jnp.int32, fix.shape, 0) < k
    head = jnp.where(wrap, fix, rolled[0:SUBLANES])
    if cur.shape[0] == SUBLANES:
        return head
    return jnp.concatenate([head, rolled[SUBLANES:]], axis=0)


class _PairMasks:
    def __init__(self, c):
        row = lax.broadcasted_iota(jnp.int32, (c, 2 * c), 0)
        lane = lax.broadcasted_iota(jnp.int32, (c, 2 * c), 1)
        col = lane & (c - 1)
        self.tril = row >= col
        self.strict = row > col
        diff = row ^ col
        self.levels = [(diff >> 1) == 0]
        sh = 2
        while (1 << sh) <= c:
            self.levels.append((diff >> (sh - 1)) == 1)
            sh += 1
        self.head0 = lane < c
        self.tok0 = self.head0.astype(F32).astype(BF16)
        self.tok1 = (lane >= c).astype(F32).astype(BF16)

    def block_diag(self, xb):
        return jnp.concatenate([xb * self.tok0, xb * self.tok1], axis=0)


def _tri_inv_pairs(lows, pm):
    nn = [-jnp.where(pm.levels[0], x, 0.0) for x in lows]
    for m in pm.levels[1:]:
        off = [jnp.where(m, x, 0.0) for x in lows]
        nnb = [x.astype(BF16) for x in nn]
        xc = [o + _dotb(xb, pm.block_diag(o.astype(BF16))) for o, xb in zip(off, nnb)]
        nn = [x - (y + _dotb(y.astype(BF16), pm.block_diag(xb)))
              for x, y, xb in zip(nn, xc, nnb)]
    return nn


def _proj_kernel(x_ref, lnw_ref, w_ref, o_ref, h_ref, *, apply_norm):
    @pl.when(pl.program_id(1) == 0)
    def _():
        x = x_ref[...]
        if apply_norm:
            x = x * lax.rsqrt(jnp.mean(x * x, axis=-1, keepdims=True) + EPS)
            x = x * lnw_ref[...]
        h_ref[...] = x.astype(BF16)

    o_ref[...] = lax.dot_general(h_ref[...], w_ref[...], NT, preferred_element_type=F32)


def _proj(x, lnw, w, *, apply_norm):
    m, d = x.shape
    n = w.shape[0]
    tm = min(m, PROJ_TM)
    tn = PROJ_TN
    assert m % tm == 0 and n % tn == 0
    vmem = 2 * (tm * d * 4 + tn * d * 2 + tm * tn * 4) + tm * d * 2 + (4 << 20)
    return pl.pallas_call(
        functools.partial(_proj_kernel, apply_norm=apply_norm),
        out_shape=jax.ShapeDtypeStruct((m, n), F32),
        grid=(m // tm, n // tn),
        in_specs=[pl.BlockSpec((tm, d), lambda i, j: (i, 0)),
                  pl.BlockSpec((1, d), lambda i, j: (0, 0)),
                  pl.BlockSpec((tn, d), lambda i, j: (j, 0))],
        out_specs=pl.BlockSpec((tm, tn), lambda i, j: (i, j)),
        scratch_shapes=[pltpu.VMEM((tm, d), BF16)],
        compiler_params=pltpu.CompilerParams(
            dimension_semantics=("parallel", "arbitrary"), vmem_limit_bytes=vmem),
        name="proj",
    )(x, lnw, w)


def _rmsnorm_kernel(x_ref, w_ref, o_ref):
    x = x_ref[...]
    o_ref[...] = x * lax.rsqrt(jnp.mean(x * x, axis=-1, keepdims=True) + EPS) * w_ref[...]


def _rmsnorm_rows(x, w):
    return pl.pallas_call(
        _rmsnorm_kernel,
        out_shape=jax.ShapeDtypeStruct(x.shape, F32),
        name="rmsnorm_rows",
    )(x, w)


def _gdn_kernel(qkv_ref, z_ref, ab_ref, abt_ref, s0_ref, prev0_ref, convw_ref,
                alr_ref, dtr_ref, alc_ref, dtc_ref, nw_ref,
                o_ref, s_ref, prev_ref, *, c, ns, nh, dk, dv, t_valid, t_total):
    ci = pl.program_id(1)
    shared = s0_ref.shape[0] == 1

    @pl.when(ci == 0)
    def _():
        for j in range(ns):
            s_ref[j] = s0_ref[0 if shared else j]
            _init_carry(prev_ref, prev0_ref, j, 0 if shared else j)

    pm = _PairMasks(c)
    row = lax.broadcasted_iota(jnp.int32, (c, c), 0)
    col = lax.broadcasted_iota(jnp.int32, (c, c), 1)
    tril = row >= col
    kw = convw_ref.shape[0]
    seqs = range(ns)

    ab = [ab_ref[j] for j in seqs]
    g_c = [-jnp.exp(alr_ref[...]) * _softplus(x + dtr_ref[...]) for x in ab]
    beta_c = [_sigmoid(x) for x in ab]
    abt = [abt_ref[j, 0] for j in seqs]
    g_r = [-jnp.exp(alc_ref[...]) * _softplus(x + dtc_ref[...]) for x in abt]
    if t_valid < t_total:
        tok_c = ci * c + lax.broadcasted_iota(jnp.int32, ab[0].shape, 0)
        tok_r = ci * c + lax.broadcasted_iota(jnp.int32, abt[0].shape, 1)
        g_c = [jnp.where(tok_c < t_valid, x, 0.0) for x in g_c]
        beta_c = [jnp.where(tok_c < t_valid, x, 0.0) for x in beta_c]
        g_r = [jnp.where(tok_r < t_valid, x, 0.0) for x in g_r]
    gc_c = _cumsum_rows(tril.astype(BF16), jnp.concatenate(g_c, axis=1))
    gc_r = _cumsum_lanes(jnp.concatenate(g_r, axis=0), (row <= col).astype(BF16))

    def conv(j, col0, width):
        cur = qkv_ref[j, :, col0:col0 + width]
        prev = prev_ref[j, :, col0:col0 + width]
        acc = None
        for k in range(kw - 1, -1, -1):
            src = cur if k == 0 else _shift_rows(cur, prev, k)
            term = src * convw_ref[kw - 1 - k:kw - k, col0:col0 + width]
            acc = term if acc is None else acc + term
        return _silu(acc)

    units = [(j, h) for j in seqs for h in range(nh)]
    us = range(len(units))
    gcc = [gc_c[:, j * LANES + h:j * LANES + h + 1] for j, h in units]
    gcr = [gc_r[j * 2 * nh + h:j * 2 * nh + h + 1, :] for j, h in units]
    beta = [beta_c[j][:, nh + h:nh + h + 1] for j, h in units]
    glast = [x[c - 1:c, :] for x in gcc]
    q = [conv(j, h * dk, dk) for j, h in units]
    k = [conv(j, nh * dk + h * dk, dk) for j, h in units]
    v = [conv(j, 2 * nh * dk + h * dv, dv) for j, h in units]
    q = [x * lax.rsqrt(jnp.sum(x * x, axis=-1, keepdims=True) + L2_EPS) * (dk ** -0.5) for x in q]
    k = [x * lax.rsqrt(jnp.sum(x * x, axis=-1, keepdims=True) + L2_EPS) for x in k]
    kb = [k[u] * beta[u] for u in us]
    vb = [v[u] * beta[u] for u in us]
    kbg = [kb[u] * jnp.exp(gcc[u]) for u in us]
    qg = [q[u] * jnp.exp(gcc[u]) for u in us]
    k_tail = [k[u] * jnp.exp(glast[u] - gcc[u]) for u in us]

    pairs = [(u, u + 1) for u in us if u % 2 == 0]

    def side_by_side(xs):
        return [jnp.concatenate([xs[a].astype(BF16), xs[b].astype(BF16)], axis=1) for a, b in pairs]

    def per_head_rows(xs):
        out = []
        for a, b in pairs:
            xa, xb = xs[a].astype(BF16), xs[b].astype(BF16)
            out.append(jnp.concatenate([jnp.concatenate([xa, jnp.zeros_like(xb)], axis=1),
                                        jnp.concatenate([jnp.zeros_like(xa), xb], axis=1)], axis=0))
        return out

    def split(xs, width):
        return [x[:, i * width:(i + 1) * width] for x in xs for i in range(2)]

    gcc_s = [jnp.where(pm.head0, gcc[a], gcc[b]) for a, b in pairs]
    gcr_s = [jnp.concatenate([gcr[a], gcr[b]], axis=1) for a, b in pairs]
    decay = [jnp.where(pm.tril, jnp.exp(jnp.where(pm.tril, x - y, 0.0)), 0.0)
             for x, y in zip(gcc_s, gcr_s)]
    k_rows = per_head_rows(k)
    lhs = [jnp.concatenate([x, y], axis=0) for x, y in zip(side_by_side(kb), side_by_side(q))]
    both = [_dotb(x, y, NT) for x, y in zip(lhs, k_rows)]
    lower = [jnp.where(pm.strict, x[0:c] * d_, 0.0) for x, d_ in zip(both, decay)]
    qk = [jnp.where(pm.tril, x[c:2 * c] * d_, 0.0) for x, d_ in zip(both, decay)]
    s_old = [s_ref[j, h] for j, h in units]
    qs = [_mm(qg[u], s_old[u]) for u in us]
    nn = [x.astype(BF16) for x in _tri_inv_pairs(lower, pm)]
    u_ = [vb[u] + x for u, x in enumerate(split(
        [_dotb(x, y) for x, y in zip(nn, per_head_rows(vb))], dv))]
    w_ = [kbg[u] + x for u, x in enumerate(split(
        [_dotb(x, y) for x, y in zip(nn, per_head_rows(kbg))], dk))]
    v_new = [u_[u] - _mm(w_[u], s_old[u]) for u in us]
    kv = [_mm(k_tail[u], v_new[u], TN) for u in us]
    for u, (j, h) in enumerate(units):
        s_ref[j, h] = s_old[u] * jnp.exp(glast[u]) + kv[u]
    o = [qs[u] + x for u, x in enumerate(split(
        [_dotb(x.astype(BF16), y) for x, y in zip(qk, per_head_rows(v_new))], dv))]
    for u, (j, h) in enumerate(units):
        oh = o[u] * lax.rsqrt(jnp.mean(o[u] * o[u], axis=-1, keepdims=True) + EPS) * nw_ref[...]
        oh = oh * _silu(z_ref[j, :, h * dv:(h + 1) * dv])
        o_ref[j, :, h * dv:(h + 1) * dv] = oh.astype(BF16)

    for j in seqs:
        prev_ref[j] = qkv_ref[j, c - SUBLANES:c, :]


def _state_specs(s0, ns, tails):
    shared = s0.shape[0] == 1
    lead = 1 if shared else ns
    idx4 = (lambda b, i: (0, 0, 0, 0)) if shared else (lambda b, i: (b, 0, 0, 0))
    idx3 = (lambda b, i: (0, 0, 0)) if shared else (lambda b, i: (0, b, 0))
    return ([pl.BlockSpec((lead,) + s0.shape[1:], idx4)]
            + [pl.BlockSpec((tl.shape[0], lead, tl.shape[2]), idx3) for tl in tails])


def _init_carry(prev_ref, tail_ref, j, jj):
    k = tail_ref.shape[0]
    if k < SUBLANES:
        prev_ref[j] = jnp.zeros(prev_ref.shape[1:], prev_ref.dtype)
    for r in range(k):
        row = SUBLANES - k + r
        prev_ref[j, row:row + 1, :] = tail_ref[r, jj:jj + 1, :]


def _gdn_parts(p, abt, s0, prev0, convw, alr, dtr, alc, dtc, nw, *, c, ns, t_valid, lay):
    nb, t, _ = p.shape
    nh, dk, dv = s0.shape[1:]
    wq = lay["qkv"][1]
    const2 = lambda b, i: (0, 0)
    return dict(
        operands=[p, p, p, abt, s0, prev0, convw, alr, dtr, alc, dtc, nw],
        in_specs=[
            pl.BlockSpec((ns, c, wq), lambda b, i: (b, i, lay["qkv"][0] // wq)),
            pl.BlockSpec((ns, c, nh * dv), lambda b, i: (b, i, lay["za"][0] // (nh * dv))),
            pl.BlockSpec((ns, c, LANES), lambda b, i: (b, i, lay["ab"][0] // LANES)),
            pl.BlockSpec((ns, 1, 2 * nh, c), lambda b, i: (b, i, 0, 0)),
            *_state_specs(s0, ns, [prev0]),
            pl.BlockSpec(convw.shape, const2),
            pl.BlockSpec(alr.shape, const2),
            pl.BlockSpec(dtr.shape, const2),
            pl.BlockSpec(alc.shape, const2),
            pl.BlockSpec(dtc.shape, const2),
            pl.BlockSpec(nw.shape, const2),
        ],
        out_shape=[jax.ShapeDtypeStruct((nb, t, nh * dv), BF16),
                   jax.ShapeDtypeStruct((nb, nh, dk, dv), F32)],
        out_specs=[pl.BlockSpec((ns, c, nh * dv), lambda b, i: (b, i, 0)),
                   pl.BlockSpec((ns, nh, dk, dv), lambda b, i: (b, 0, 0, 0))],
        scratch=[pltpu.VMEM((ns, SUBLANES, wq), F32)],
        statics=dict(c=c, ns=ns, nh=nh, dk=dk, dv=dv, t_valid=t_valid, t_total=t))


def _token_shift(x_ref, prev_ref, mu_ref, j, lo, hi):
    cur = x_ref[j, :, lo:hi]
    prv = _shift_rows(cur, prev_ref[j, :, lo:hi], 1)
    return cur + mu_ref[:, lo:hi] * (prv - cur)


_RWKV_OPERANDS = ("kkg", "rg", "bh", "kh", "kt", "bt", "v")


def _rwkv_kernel(x_ref, wa_ref, s0_ref, prev0_ref, prevwa0_ref, mu_ref, muwa_ref,
                 w0_ref, w2_ref, a0_ref, a2_ref, kk_ref, ka_ref, rk_ref, gnw_ref, gnb_ref,
                 o_ref, s_ref, prev_ref, prevwa_ref, sbd_ref,
                 *, c, ns, nh, n, t_valid, t_total):
    step = pl.program_id(1)
    d = nh * n
    npair = nh // 2
    ngrp = d // LANES
    shared = s0_ref.shape[0] == 1
    seqs = range(ns)

    @pl.when(step == 0)
    def _():
        zero = jnp.zeros((n, n), F32)
        for j in seqs:
            jj = 0 if shared else j
            for p in range(npair):
                top = jnp.concatenate([s0_ref[jj, 2 * p], zero], axis=1)
                bot = jnp.concatenate([zero, s0_ref[jj, 2 * p + 1]], axis=1)
                sbd_ref[j, p] = jnp.concatenate([top, bot], axis=0)
            _init_carry(prev_ref, prev0_ref, j, jj)
            _init_carry(prevwa_ref, prevwa0_ref, j, jj)

    w2 = 2 * c
    pm = _PairMasks(c)
    tril, strict = pm.tril, pm.strict
    chl = lax.broadcasted_iota(jnp.int32, (c, LANES), 1)
    ch0 = (chl < n).astype(F32).astype(BF16)
    ch1 = (chl >= n).astype(F32).astype(BF16)
    row2 = lax.broadcasted_iota(jnp.int32, (LANES, LANES), 0)
    lane2 = lax.broadcasted_iota(jnp.int32, (LANES, LANES), 1)
    log2n = n.bit_length() - 1
    same_head = (row2 >> log2n) == (lane2 >> log2n)
    seg_ones = same_head.astype(F32).astype(BF16)

    def bd_ch(xb):
        return jnp.concatenate([xb * ch0, xb * ch1], axis=0)

    dot = _dotb

    def seg_sum(x):
        stk = jnp.concatenate([x[:, i * LANES:(i + 1) * LANES] for i in range(ngrp)], axis=0)
        acc = dot(stk.astype(BF16), seg_ones)
        return jnp.concatenate([acc[i * c:(i + 1) * c] for i in range(ngrp)], axis=1)

    lora = w2_ref.shape[0]
    row_c = lax.broadcasted_iota(jnp.int32, (c, c), 0)
    col_c = lax.broadcasted_iota(jnp.int32, (c, c), 1)
    tril_b = (row_c >= col_c).astype(BF16)

    def pre_steps(out):
        r = [_token_shift(x_ref, prev_ref, mu_ref, j, 0, d) for j in seqs]
        kb = [_token_shift(x_ref, prev_ref, mu_ref, j, d, 2 * d) for j in seqs]
        vb = [_token_shift(x_ref, prev_ref, mu_ref, j, 2 * d, 3 * d) for j in seqs]
        sz = [_silu(_token_shift(x_ref, prev_ref, mu_ref, j, 3 * d, 4 * d)) for j in seqs]
        wa = [wa_ref[j] for j in seqs]
        wa = [wa[j] + muwa_ref[...] * (_shift_rows(wa[j], prevwa_ref[j], 1) - wa[j])
              for j in seqs]
        tw = [jnp.tanh(x[:, 0:lora]) for x in wa]
        w_raw = [w0_ref[...] + _mm(tw[j], w2_ref[...]) for j in seqs]
        a = [_sigmoid(a0_ref[...] + _mm(wa[j][:, lora:], a2_ref[...])) for j in seqs]
        w_log = [-jnp.exp(-_softplus(-x) - 0.5) for x in w_raw]
        kku = [kb[j] * kk_ref[...] for j in seqs]
        k2 = [kb[j] * (1.0 + (a[j] - 1.0) * ka_ref[...]) for j in seqs]
        if t_valid < t_total:
            ok = (step * c + lax.broadcasted_iota(jnp.int32, (c, d), 0)) < t_valid
            w_log = [jnp.where(ok, x, 0.0) for x in w_log]
            kku = [jnp.where(ok, x, 0.0) for x in kku]
            vb = [jnp.where(ok, x, 0.0) for x in vb]
        gcum = [_cumsum_rows(tril_b, x) for x in w_log]
        glast = [x[c - 1:c, :] for x in gcum]
        g_in = [jnp.exp(x) for x in gcum]
        g_ex = [jnp.exp(gcum[j] - w_log[j]) for j in seqs]
        g_inv = [jnp.exp(-x) for x in gcum]
        g_tail = [jnp.exp(glast[j] - gcum[j]) for j in seqs]
        ssq = [x * x for x in kku]
        bonus_in = [r[j] * k2[j] * rk_ref[...] for j in seqs]
        kk = [kku[j] * lax.rsqrt(seg_sum(ssq[j]) + L2_EPS) for j in seqs]
        bonus = [seg_sum(bonus_in[j]) * vb[j] for j in seqs]
        for j in seqs:
            b = kk[j] * a[j]
            out[j].update(
                kkg=(kk[j] * g_ex[j]).astype(BF16), rg=(r[j] * g_in[j]).astype(BF16),
                bh=(b * g_inv[j]).astype(BF16), kh=(k2[j] * g_inv[j]).astype(BF16),
                kt=(k2[j] * g_tail[j]).astype(BF16), bt=(b * g_tail[j]).astype(BF16),
                v=vb[j].astype(BF16), bonus=bonus[j], sz=sz[j], g_all=jnp.exp(glast[j]))

    units = [(j, p) for j in seqs for p in range(npair)]
    us = range(len(units))

    def stage_steps(ops):
        kkg, rg, bh, kh, kt, bt, v = (
            [ops[j][k][:, p * LANES:(p + 1) * LANES] for j, p in units] for k in _RWKV_OPERANDS)
        g_all = [ops[j]["g_all"][:, p * LANES:(p + 1) * LANES] for j, p in units]
        lhs = [jnp.concatenate([kkg[u], rg[u]], axis=0) for u in us]
        rhs = [jnp.concatenate([bd_ch(bh[u]), bd_ch(kh[u])], axis=0) for u in us]
        big = [dot(lhs[u], rhs[u], NT) for u in us]
        s_old = [sbd_ref[j, p] for j, p in units]
        from_s = [dot(lhs[u], s_old[u].astype(BF16), NT) for u in us]
        low = [jnp.where(strict, x[0:c, 0:w2], 0.0) for x in big]
        m_kv = [jnp.where(strict, x[0:c, w2:2 * w2], 0.0) for x in big]
        q_b = [jnp.where(tril, x[c:2 * c, 0:w2], 0.0) for x in big]
        p_kv = [jnp.where(tril, x[c:2 * c, w2:2 * w2], 0.0) for x in big]
        mp = [dot(jnp.concatenate([m_kv[u], p_kv[u]], axis=0).astype(BF16), bd_ch(v[u]))
              for u in us]
        nn = _tri_inv_pairs(low, pm)
        rhs_sa = [from_s[u][0:c] + mp[u][0:c] for u in us]
        sa = [rhs_sa[u] + dot(nn[u].astype(BF16), bd_ch(rhs_sa[u].astype(BF16))) for u in us]
        sab = [x.astype(BF16) for x in sa]
        qs = [dot(q_b[u].astype(BF16), bd_ch(sab[u])) for u in us]
        upd = [dot(jnp.concatenate([v[u], -sab[u]], axis=0),
                   jnp.concatenate([kt[u], bt[u]], axis=0), TN) for u in us]
        s_new = [s_old[u] * g_all[u] + jnp.where(same_head, upd[u], 0.0) for u in us]
        for u, (j, p) in enumerate(units):
            sbd_ref[j, p] = s_new[u]

        @pl.when(step == pl.num_programs(1) - 1)
        def _():
            for u, (j, p) in enumerate(units):
                s_ref[j, 2 * p] = s_new[u][0:n, 0:n]
                s_ref[j, 2 * p + 1] = s_new[u][n:2 * n, n:2 * n]

        y = [from_s[u][c:2 * c] + mp[u][c:2 * c] - qs[u] for u in us]
        yj = [jnp.concatenate(y[j * npair:(j + 1) * npair], axis=1) for j in seqs]
        yc = [yj[j] - seg_sum(yj[j]) * (1.0 / n) for j in seqs]
        var = [seg_sum(yc[j] * yc[j]) * (1.0 / n) for j in seqs]
        for j in seqs:
            yn = yc[j] * lax.rsqrt(var[j] + n * GN_EPS_PER_CH) * gnw_ref[...] + gnb_ref[...]
            o_ref[j] = ((yn + ops[j]["bonus"]) * ops[j]["sz"]).astype(BF16)

    ops = [dict() for _ in seqs]
    pre_steps(ops)
    stage_steps(ops)

    for j in seqs:
        prev_ref[j] = x_ref[j, c - SUBLANES:c, :]
        prevwa_ref[j] = wa_ref[j, c - SUBLANES:c, :]


def _rwkv_parts(p, s0, prev0, prevwa0, mu, muwa, w0, w2, a0, a2, kk, ka, rk, gnw, gnb,
                *, c, ns, t_valid, lay):
    nb, t, _ = p.shape
    nh, n = s0.shape[1:3]
    d = nh * n
    wx = lay["rkvz"][1]
    const2 = lambda b, i: (0, 0)
    vec = pl.BlockSpec((1, d), const2)
    assert 2 * n == LANES and c <= n and c & (c - 1) == 0, "two heads per lane group"
    return dict(
        operands=[p, p, s0, prev0, prevwa0, mu, muwa, w0, w2, a0, a2, kk, ka, rk, gnw, gnb],
        in_specs=[
            pl.BlockSpec((ns, c, wx), lambda b, i: (b, i, lay["rkvz"][0] // wx)),
            pl.BlockSpec((ns, c, LANES), lambda b, i: (b, i, lay["wa"][0] // LANES)),
            *_state_specs(s0, ns, [prev0, prevwa0]),
            pl.BlockSpec((1, wx), const2),
            pl.BlockSpec((1, LANES), const2),
            vec,
            pl.BlockSpec(w2.shape, const2),
            vec,
            pl.BlockSpec(a2.shape, const2),
            vec, vec, vec, vec, vec,
        ],
        out_shape=[jax.ShapeDtypeStruct((nb, t, d), BF16),
                   jax.ShapeDtypeStruct((nb, nh, n, n), F32)],
        out_specs=[pl.BlockSpec((ns, c, d), lambda b, i: (b, i, 0)),
                   pl.BlockSpec((ns, nh, n, n), lambda b, i: (b, 0, 0, 0))],
        scratch=[pltpu.VMEM((ns, SUBLANES, wx), F32), pltpu.VMEM((ns, SUBLANES, LANES), F32),
                 pltpu.VMEM((ns, nh // 2, LANES, LANES), F32)],
        statics=dict(c=c, ns=ns, nh=nh, n=n, t_valid=t_valid, t_total=t))


def _recurrence(body, parts, *, name):
    statics = parts["statics"]
    nb, t = parts["out_shape"][0].shape[:2]
    return pl.pallas_call(
        functools.partial(body, **statics),
        out_shape=tuple(parts["out_shape"]),
        grid=(nb // statics["ns"], t // statics["c"]),
        in_specs=parts["in_specs"],
        out_specs=tuple(parts["out_specs"]),
        scratch_shapes=parts["scratch"],
        compiler_params=pltpu.CompilerParams(
            dimension_semantics=("parallel", "arbitrary"), vmem_limit_bytes=VMEM_LIMIT),
        name=name,
    )(*parts["operands"])


def _merge_kernel(oa_ref, ob_ref, gate_ref, x_ref, woa_ref, wob_ref, wo_ref, lnf_ref, y_ref):
    d = x_ref.shape[1]
    ba = jnp.dot(oa_ref[...], woa_ref[...], preferred_element_type=F32)
    bb = jnp.dot(ob_ref[...], wob_ref[...], preferred_element_type=F32)
    gates = _sigmoid(gate_ref[...])
    merged = gates[:, :d] * ba + gates[:, d:] * bb
    xn = x_ref[...] + jnp.dot(merged.astype(BF16), wo_ref[...], preferred_element_type=F32)
    y_ref[...] = xn * lax.rsqrt(jnp.mean(xn * xn, axis=-1, keepdims=True) + EPS) * lnf_ref[...]


def _merge(oa, ob, p, x, woa, wob, wo, lnf, *, lay):
    m, d = x.shape
    tm = min(m, 512)
    wg = lay["gate"][1]
    const2 = lambda i: (0, 0)
    return pl.pallas_call(
        _merge_kernel,
        out_shape=jax.ShapeDtypeStruct((m, d), F32),
        grid=(m // tm,),
        in_specs=[
            pl.BlockSpec((tm, oa.shape[1]), lambda i: (i, 0)),
            pl.BlockSpec((tm, ob.shape[1]), lambda i: (i, 0)),
            pl.BlockSpec((tm, wg), lambda i: (i, lay["gate"][0] // wg)),
            pl.BlockSpec((tm, d), lambda i: (i, 0)),
            pl.BlockSpec(woa.shape, const2),
            pl.BlockSpec(wob.shape, const2),
            pl.BlockSpec(wo.shape, const2),
            pl.BlockSpec((1, d), const2),
        ],
        out_specs=pl.BlockSpec((tm, d), lambda i: (i, 0)),
        compiler_params=pltpu.CompilerParams(
            dimension_semantics=("parallel",), vmem_limit_bytes=VMEM_LIMIT),
        name="merge",
    )(oa, ob, p, x, woa, wob, wo, lnf)


def _pad_lanes(v, width):
    return jnp.pad(v, ((0, 0), (0, width - v.shape[1])))


def _token_major_t(ab, c):
    nb, t, k = ab.shape
    return jnp.transpose(ab.reshape(nb, t // c, c, k), (0, 1, 3, 2))


def kernel(x_prompt, x_sample, state_gdn, state_gdn_conv, state_rwkv, state_shift, meta_tokens,
           ln1_w, w_in, gdn_conv_w, gdn_a_log, gdn_dt_bias, gdn_norm_w, w_out_a, rwkv_mu, rwkv_w0,
           rwkv_w2, rwkv_a0, rwkv_a2, rwkv_k_k, rwkv_k_a, rwkv_r_k, rwkv_gn_w, rwkv_gn_b, w_out_b,
           w_out, lnf_w):
    assert ln1_w.shape[0] == 1, "single-layer trunk"
    bp, seq, d = x_prompt.shape
    bs, tseq, _ = x_sample.shape
    n_meta = meta_tokens.shape[0]
    _, _, nh_a, dk, dv = state_gdn.shape
    _, _, nh_b, n_b, _ = state_rwkv.shape
    kw, w_qkv = gdn_conv_w.shape[1:]
    lora_w = rwkv_w2.shape[1]
    lora_a = rwkv_a2.shape[1]
    d_a = nh_a * dv
    d_b = nh_b * n_b
    assert w_qkv == 2 * nh_a * dk + d_a and d_a == d and d_b == d and lora_w + lora_a == LANES
    assert n_meta % SUBLANES == 0 and n_meta % 16 == 0 and tseq >= kw - 1

    o_a = w_qkv
    o_b = o_a + nh_a
    o_z = o_b + nh_a
    o_r = o_z + d_a
    o_g = o_r + 3 * d_b + lora_w + lora_a + d_b
    w = w_in[0].T
    wr = w[o_r:o_g]
    mu = rwkv_mu
    lora0 = 3 * d_b
    rows = [w[:w_qkv], w[o_z:o_r], wr[:lora0], wr[lora0 + LANES:], w[o_g:],
            wr[lora0:lora0 + LANES], w[o_a:o_z]]
    used = sum(rw.shape[0] for rw in rows)
    n_pad = -(-(used + LANES - 2 * nh_a) // PROJ_TN) * PROJ_TN
    rows.append(jnp.zeros((n_pad - used, d), F32))
    w_all = jnp.concatenate([rw.astype(BF16) for rw in rows], axis=0)
    lay = {"qkv": (0, w_qkv), "za": (w_qkv, d_a), "rkvz": (w_qkv + d_a, 4 * d_b),
           "gate": (w_qkv + d_a + 4 * d_b, 2 * d)}
    lay["wa"] = (lay["gate"][0] + 2 * d, LANES)
    lay["ab"] = (lay["wa"][0] + LANES, LANES)
    for off, width in lay.values():
        assert off % width == 0
    mu_x = jnp.concatenate([mu[:, :lora0], mu[:, lora0 + LANES:]], axis=1)
    mu_wa = mu[:, lora0:lora0 + LANES]

    alr = _pad_lanes(gdn_a_log, LANES)
    dtr = _pad_lanes(gdn_dt_bias, LANES)
    alc = jnp.pad(gdn_a_log.reshape(nh_a, 1), ((0, nh_a), (0, 0)))
    dtc = jnp.pad(gdn_dt_bias.reshape(nh_a, 1), ((0, nh_a), (0, 0)))
    convw = gdn_conv_w[0]
    rk = rwkv_r_k.reshape(1, d_b)
    woa = w_out_a[0].astype(BF16)
    wob = w_out_b[0].astype(BF16)
    wo = w_out[0].astype(BF16)
    lnf = lnf_w.reshape(1, d)

    def branches(p, c, ns_a, ns_b, t_valid, s_gdn, conv_tail, s_rwkv, x_tail, wa_tail):
        abt = _token_major_t(p[:, :, lay["ab"][0]:lay["ab"][0] + 2 * nh_a], c)
        oa, sg = _recurrence(_gdn_kernel, _gdn_parts(
            p, abt, s_gdn, conv_tail, convw, alr, dtr, alc, dtc, gdn_norm_w,
            c=c, ns=ns_a, t_valid=t_valid, lay=lay), name="gdn")
        ob, sr = _recurrence(_rwkv_kernel, _rwkv_parts(
            p, s_rwkv, x_tail, wa_tail, mu_x, mu_wa, rwkv_w0, rwkv_w2[0], rwkv_a0, rwkv_a2[0],
            rwkv_k_k, rwkv_k_a, rk, rwkv_gn_w, rwkv_gn_b,
            c=c, ns=ns_b, t_valid=t_valid, lay=lay), name="rwkv")
        return oa, sg, ob, sr

    x0, wx = lay["rkvz"]
    a0_, _ = lay["wa"]

    tpad = SUBLANES
    assert n_meta <= bs and tseq < tpad
    rider = jnp.zeros((bs, tpad - tseq, d), F32).at[:n_meta, 0].set(meta_tokens)
    xs = jnp.concatenate([x_sample, rider], axis=1).reshape(bs * tpad, d)
    p_s = _proj(xs, ln1_w, w_all, apply_norm=True)
    p_m = p_s.reshape(bs, tpad, n_pad)[:n_meta, tseq]

    _, sg_m, _, sr_m = branches(
        p_m[None], n_meta, 1, 1, n_meta,
        jnp.zeros((1, nh_a, dk, dv), F32), jnp.zeros((1, 1, w_qkv), F32),
        jnp.zeros((1, nh_b, n_b, n_b), F32), jnp.zeros((1, 1, wx), F32),
        jnp.zeros((1, 1, LANES), F32))

    xp = x_prompt.reshape(bp * seq, d)
    p_p = _proj(xp, ln1_w, w_all, apply_norm=True)
    last_m = p_m[n_meta - 1:, None]
    oa_p, sg_p, ob_p, sr_p = branches(
        p_p.reshape(bp, seq, n_pad), 64, 4, 2, seq, sg_m,
        p_m[n_meta - (kw - 1):, None, :w_qkv], sr_m,
        last_m[:, :, x0:x0 + wx], last_m[:, :, a0_:a0_ + LANES])
    y_p = _merge(oa_p.reshape(bp * seq, d), ob_p.reshape(bp * seq, d), p_p, xp, woa, wob, wo, lnf,
                 lay=lay)

    p_first =_proj(state_shift[0], ln1_w, w_all, apply_norm=False)[None]
    oa_s, sg_s, ob_s, sr_s = branches(
        p_s.reshape(bs, tpad, n_pad), tpad, 8, 8, tseq, state_gdn[0],
        jnp.transpose(state_gdn_conv[0], (1, 0, 2)), state_rwkv[0],
        p_first[:, :, x0:x0 + wx], p_first[:, :, a0_:a0_ + LANES])
    y_s = _merge(oa_s.reshape(bs * tpad, d), ob_s.reshape(bs * tpad, d), p_s, xs, woa, wob, wo,
                 lnf, lay=lay)

    shift_p = _rmsnorm_rows(x_prompt[:, -1], ln1_w)
    shift_s = _rmsnorm_rows(x_sample[:, -1], ln1_w)
    conv_p = p_p.reshape(bp, seq, n_pad)[:, seq - (kw - 1):, :w_qkv]
    conv_s = p_s.reshape(bs, tpad, n_pad)[:, tseq - (kw - 1):tseq, :w_qkv]
    return (y_p.reshape(bp, seq, d), y_s.reshape(bs, tpad, d)[:, :tseq],
            sg_p[None], conv_p[None], sr_p[None], shift_p[None],
            sg_s[None], conv_s[None], sr_s[None], shift_s[None])
```

```python
import functools

import jax
import jax.numpy as jnp
from jax import lax
from jax.experimental import pallas as pl
from jax.experimental.pallas import tpu as pltpu

F32 = jnp.float32
BF16 = jnp.bfloat16

EPS = 1e-6
L2_EPS = 1e-6
GN_EPS_PER_CH = 1e-5

LANES = 128
SUBLANES = 8
VMEM_LIMIT = 32 * 1024 * 1024
PROJ_TM = 1024
PROJ_TN = 1536

NN = (((1,), (0,)), ((), ()))
NT = (((1,), (1,)), ((), ()))
TN = (((0,), (0,)), ((), ()))


def _mm(a, b, dims=NN):
    return lax.dot_general(a.astype(BF16), b.astype(BF16), dims, preferred_element_type=F32)


def _dotb(a, b, dims=NN):
    return lax.dot_general(a, b, dims, preferred_element_type=F32)


def _split3(x):
    hi = x.astype(BF16)
    rem = x - hi.astype(F32)
    mid = rem.astype(BF16)
    return hi, mid, (rem - mid.astype(F32)).astype(BF16)


def _cumsum_rows(tril01, x):
    hi, mid, lo = _split3(x)
    return _dotb(tril01, hi) + _dotb(tril01, mid) + _dotb(tril01, lo)


def _cumsum_lanes(x, triu01):
    hi, mid, lo = _split3(x)
    return _dotb(hi, triu01) + _dotb(mid, triu01) + _dotb(lo, triu01)


def _sigmoid(x):
    return 1.0 / (1.0 + jnp.exp(-x))


def _silu(x):
    return x * _sigmoid(x)


def _softplus(x):
    return jnp.maximum(x, 0.0) + jnp.log(1.0 + jnp.exp(-jnp.abs(x)))


def _shift_rows(cur, prev, k):
    rolled = pltpu.roll(cur, k, 0)
    fix = pltpu.roll(prev, k, 0)
    wrap = lax.broadcasted_iota(jnp.int32, fix.shape, 0) < k
    head = jnp.where(wrap, fix, rolled[0:SUBLANES])
    if cur.shape[0] == SUBLANES:
        return head
    return jnp.concatenate([head, rolled[SUBLANES:]], axis=0)


class _PairMasks:
    def __init__(self, c):
        row = lax.broadcasted_iota(jnp.int32, (c, 2 * c), 0)
        lane = lax.broadcasted_iota(jnp.int32, (c, 2 * c), 1)
        col = lane & (c - 1)
        self.tril = row >= col
        self.strict = row > col
        diff = row ^ col
        self.levels = [(diff >> 1) == 0]
        sh = 2
        while (1 << sh) <= c:
            self.levels.append((diff >> (sh - 1)) == 1)
            sh += 1
        self.head0 = lane < c
        self.tok0 = self.head0.astype(F32).astype(BF16)
        self.tok1 = (lane >= c).astype(F32).astype(BF16)

    def block_diag(self, xb):
        return jnp.concatenate([xb * self.tok0, xb * self.tok1], axis=0)


def _tri_inv_pairs(lows, pm):
    nn = [-jnp.where(pm.levels[0], x, 0.0) for x in lows]
    for m in pm.levels[1:]:
        off = [jnp.where(m, x, 0.0) for x in lows]
        nnb = [x.astype(BF16) for x in nn]
        xc = [o + _dotb(xb, pm.block_diag(o.astype(BF16))) for o, xb in zip(off, nnb)]
        nn = [x - (y + _dotb(y.astype(BF16), pm.block_diag(xb)))
              for x, y, xb in zip(nn, xc, nnb)]
    return nn


def _proj_kernel(x_ref, lnw_ref, w_ref, o_ref, h_ref, *, apply_norm):
    @pl.when(pl.program_id(1) == 0)
    def _():
        x = x_ref[...]
        if apply_norm:
            x = x * lax.rsqrt(jnp.mean(x * x, axis=-1, keepdims=True) + EPS)
            x = x * lnw_ref[...]
        h_ref[...] = x.astype(BF16)

    o_ref[...] = lax.dot_general(h_ref[...], w_ref[...], NT, preferred_element_type=F32)


def _proj(x, lnw, w, *, apply_norm):
    m, d = x.shape
    n = w.shape[0]
    tm = min(m, PROJ_TM)
    tn = PROJ_TN
    assert m % tm == 0 and n % tn == 0
    vmem = 2 * (tm * d * 4 + tn * d * 2 + tm * tn * 4) + tm * d * 2 + (4 << 20)
    return pl.pallas_call(
        functools.partial(_proj_kernel, apply_norm=apply_norm),
        out_shape=jax.ShapeDtypeStruct((m, n), F32),
        grid=(m // tm, n // tn),
        in_specs=[pl.BlockSpec((tm, d), lambda i, j: (i, 0)),
                  pl.BlockSpec((1, d), lambda i, j: (0, 0)),
                  pl.BlockSpec((tn, d), lambda i, j: (j, 0))],
        out_specs=pl.BlockSpec((tm, tn), lambda i, j: (i, j)),
        scratch_shapes=[pltpu.VMEM((tm, d), BF16)],
        compiler_params=pltpu.CompilerParams(
            dimension_semantics=("parallel", "arbitrary"), vmem_limit_bytes=vmem),
        name="proj",
    )(x, lnw, w)


def _rmsnorm_kernel(x_ref, w_ref, o_ref):
    x = x_ref[...]
    o_ref[...] = x * lax.rsqrt(jnp.mean(x * x, axis=-1, keepdims=True) + EPS) * w_ref[...]


def _rmsnorm_rows(x, w):
    return pl.pallas_call(
        _rmsnorm_kernel,
        out_shape=jax.ShapeDtypeStruct(x.shape, F32),
        name="rmsnorm_rows",
    )(x, w)


def _gdn_kernel(qkv_ref, z_ref, ab_ref, abt_ref, s0_ref, prev0_ref, convw_ref,
                alr_ref, dtr_ref, alc_ref, dtc_ref, nw_ref,
                o_ref, s_ref, prev_ref, *, c, ns, nh, dk, dv, t_valid, t_total):
    ci = pl.program_id(1)
    shared = s0_ref.shape[0] == 1

    @pl.when(ci == 0)
    def _():
        for j in range(ns):
            s_ref[j] = s0_ref[0 if shared else j]
            _init_carry(prev_ref, prev0_ref, j, 0 if shared else j)

    pm = _PairMasks(c)
    row = lax.broadcasted_iota(jnp.int32, (c, c), 0)
    col = lax.broadcasted_iota(jnp.int32, (c, c), 1)
    tril = row >= col
    kw = convw_ref.shape[0]
    seqs = range(ns)

    ab = [ab_ref[j] for j in seqs]
    g_c = [-jnp.exp(alr_ref[...]) * _softplus(x + dtr_ref[...]) for x in ab]
    beta_c = [_sigmoid(x) for x in ab]
    abt = [abt_ref[j, 0] for j in seqs]
    g_r = [-jnp.exp(alc_ref[...]) * _softplus(x + dtc_ref[...]) for x in abt]
    if t_valid < t_total:
        tok_c = ci * c + lax.broadcasted_iota(jnp.int32, ab[0].shape, 0)
        tok_r = ci * c + lax.broadcasted_iota(jnp.int32, abt[0].shape, 1)
        g_c = [jnp.where(tok_c < t_valid, x, 0.0) for x in g_c]
        beta_c = [jnp.where(tok_c < t_valid, x, 0.0) for x in beta_c]
        g_r = [jnp.where(tok_r < t_valid, x, 0.0) for x in g_r]
    gc_c = _cumsum_rows(tril.astype(BF16), jnp.concatenate(g_c, axis=1))
    gc_r = _cumsum_lanes(jnp.concatenate(g_r, axis=0), (row <= col).astype(BF16))

    def conv(j, col0, width):
        cur = qkv_ref[j, :, col0:col0 + width]
        prev = prev_ref[j, :, col0:col0 + width]
        acc = None
        for k in range(kw - 1, -1, -1):
            src = cur if k == 0 else _shift_rows(cur, prev, k)
            term = src * convw_ref[kw - 1 - k:kw - k, col0:col0 + width]
            acc = term if acc is None else acc + term
        return _silu(acc)

    units = [(j, h) for j in seqs for h in range(nh)]
    us = range(len(units))
    gcc = [gc_c[:, j * LANES + h:j * LANES + h + 1] for j, h in units]
    gcr = [gc_r[j * 2 * nh + h:j * 2 * nh + h + 1, :] for j, h in units]
    beta = [beta_c[j][:, nh + h:nh + h + 1] for j, h in units]
    glast = [x[c - 1:c, :] for x in gcc]
    q = [conv(j, h * dk, dk) for j, h in units]
    k = [conv(j, nh * dk + h * dk, dk) for j, h in units]
    v = [conv(j, 2 * nh * dk + h * dv, dv) for j, h in units]
    q = [x * lax.rsqrt(jnp.sum(x * x, axis=-1, keepdims=True) + L2_EPS) * (dk ** -0.5) for x in q]
    k = [x * lax.rsqrt(jnp.sum(x * x, axis=-1, keepdims=True) + L2_EPS) for x in k]
    kb = [k[u] * beta[u] for u in us]
    vb = [v[u] * beta[u] for u in us]
    kbg = [kb[u] * jnp.exp(gcc[u]) for u in us]
    qg = [q[u] * jnp.exp(gcc[u]) for u in us]
    k_tail = [k[u] * jnp.exp(glast[u] - gcc[u]) for u in us]

    pairs = [(u, u + 1) for u in us if u % 2 == 0]

    def side_by_side(xs):
        return [jnp.concatenate([xs[a].astype(BF16), xs[b].astype(BF16)], axis=1) for a, b in pairs]

    def per_head_rows(xs):
        out = []
        for a, b in pairs:
            xa, xb = xs[a].astype(BF16), xs[b].astype(BF16)
            out.append(jnp.concatenate([jnp.concatenate([xa, jnp.zeros_like(xb)], axis=1),
                                        jnp.concatenate([jnp.zeros_like(xa), xb], axis=1)], axis=0))
        return out

    def split(xs, width):
        return [x[:, i * width:(i + 1) * width] for x in xs for i in range(2)]

    gcc_s = [jnp.where(pm.head0, gcc[a], gcc[b]) for a, b in pairs]
    gcr_s = [jnp.concatenate([gcr[a], gcr[b]], axis=1) for a, b in pairs]
    decay = [jnp.where(pm.tril, jnp.exp(jnp.where(pm.tril, x - y, 0.0)), 0.0)
             for x, y in zip(gcc_s, gcr_s)]
    k_rows = per_head_rows(k)
    lhs = [jnp.concatenate([x, y], axis=0) for x, y in zip(side_by_side(kb), side_by_side(q))]
    both = [_dotb(x, y, NT) for x, y in zip(lhs, k_rows)]
    lower = [jnp.where(pm.strict, x[0:c] * d_, 0.0) for x, d_ in zip(both, decay)]
    qk = [jnp.where(pm.tril, x[c:2 * c] * d_, 0.0) for x, d_ in zip(both, decay)]
    s_old = [s_ref[j, h] for j, h in units]
    qs = [_mm(qg[u], s_old[u]) for u in us]
    nn = [x.astype(BF16) for x in _tri_inv_pairs(lower, pm)]
    u_ = [vb[u] + x for u, x in enumerate(split(
        [_dotb(x, y) for x, y in zip(nn, per_head_rows(vb))], dv))]
    w_ = [kbg[u] + x for u, x in enumerate(split(
        [_dotb(x, y) for x, y in zip(nn, per_head_rows(kbg))], dk))]
    v_new = [u_[u] - _mm(w_[u], s_old[u]) for u in us]
    kv = [_mm(k_tail[u], v_new[u], TN) for u in us]
    for u, (j, h) in enumerate(units):
        s_ref[j, h] = s_old[u] * jnp.exp(glast[u]) + kv[u]
    o = [qs[u] + x for u, x in enumerate(split(
        [_dotb(x.astype(BF16), y) for x, y in zip(qk, per_head_rows(v_new))], dv))]
    for u, (j, h) in enumerate(units):
        oh = o[u] * lax.rsqrt(jnp.mean(o[u] * o[u], axis=-1, keepdims=True) + EPS) * nw_ref[...]
        oh = oh * _silu(z_ref[j, :, h * dv:(h + 1) * dv])
        o_ref[j, :, h * dv:(h + 1) * dv] = oh.astype(BF16)

    for j in seqs:
        prev_ref[j] = qkv_ref[j, c - SUBLANES:c, :]


def _state_specs(s0, ns, tails):
    shared = s0.shape[0] == 1
    lead = 1 if shared else ns
    idx4 = (lambda b, i: (0, 0, 0, 0)) if shared else (lambda b, i: (b, 0, 0, 0))
    idx3 = (lambda b, i: (0, 0, 0)) if shared else (lambda b, i: (0, b, 0))
    return ([pl.BlockSpec((lead,) + s0.shape[1:], idx4)]
            + [pl.BlockSpec((tl.shape[0], lead, tl.shape[2]), idx3) for tl in tails])


def _init_carry(prev_ref, tail_ref, j, jj):
    k = tail_ref.shape[0]
    if k < SUBLANES:
        prev_ref[j] = jnp.zeros(prev_ref.shape[1:], prev_ref.dtype)
    for r in range(k):
        row = SUBLANES - k + r
        prev_ref[j, row:row + 1, :] = tail_ref[r, jj:jj + 1, :]


def _gdn_parts(p, abt, s0, prev0, convw, alr, dtr, alc, dtc, nw, *, c, ns, t_valid, lay):
    nb, t, _ = p.shape
    nh, dk, dv = s0.shape[1:]
    wq = lay["qkv"][1]
    const2 = lambda b, i: (0, 0)
    return dict(
        operands=[p, p, p, abt, s0, prev0, convw, alr, dtr, alc, dtc, nw],
        in_specs=[
            pl.BlockSpec((ns, c, wq), lambda b, i: (b, i, lay["qkv"][0] // wq)),
            pl.BlockSpec((ns, c, nh * dv), lambda b, i: (b, i, lay["za"][0] // (nh * dv))),
            pl.BlockSpec((ns, c, LANES), lambda b, i: (b, i, lay["ab"][0] // LANES)),
            pl.BlockSpec((ns, 1, 2 * nh, c), lambda b, i: (b, i, 0, 0)),
            *_state_specs(s0, ns, [prev0]),
            pl.BlockSpec(convw.shape, const2),
            pl.BlockSpec(alr.shape, const2),
            pl.BlockSpec(dtr.shape, const2),
            pl.BlockSpec(alc.shape, const2),
            pl.BlockSpec(dtc.shape, const2),
            pl.BlockSpec(nw.shape, const2),
        ],
        out_shape=[jax.ShapeDtypeStruct((nb, t, nh * dv), BF16),
                   jax.ShapeDtypeStruct((nb, nh, dk, dv), F32)],
        out_specs=[pl.BlockSpec((ns, c, nh * dv), lambda b, i: (b, i, 0)),
                   pl.BlockSpec((ns, nh, dk, dv), lambda b, i: (b, 0, 0, 0))],
        scratch=[pltpu.VMEM((ns, SUBLANES, wq), F32)],
        statics=dict(c=c, ns=ns, nh=nh, dk=dk, dv=dv, t_valid=t_valid, t_total=t))


def _token_shift(x_ref, prev_ref, mu_ref, j, lo, hi):
    cur = x_ref[j, :, lo:hi]
    prv = _shift_rows(cur, prev_ref[j, :, lo:hi], 1)
    return cur + mu_ref[:, lo:hi] * (prv - cur)


_RWKV_OPERANDS = ("kkg", "rg", "bh", "kh", "kt", "bt", "v")


def _rwkv_kernel(x_ref, wa_ref, s0_ref, prev0_ref, prevwa0_ref, mu_ref, muwa_ref,
                 w0_ref, w2_ref, a0_ref, a2_ref, kk_ref, ka_ref, rk_ref, gnw_ref, gnb_ref,
                 o_ref, s_ref, prev_ref, prevwa_ref, sbd_ref,
                 *, c, ns, nh, n, t_valid, t_total):
    step = pl.program_id(1)
    d = nh * n
    npair = nh // 2
    ngrp = d // LANES
    shared = s0_ref.shape[0] == 1
    seqs = range(ns)

    @pl.when(step == 0)
    def _():
        zero = jnp.zeros((n, n), F32)
        for j in seqs:
            jj = 0 if shared else j
            for p in range(npair):
                top = jnp.concatenate([s0_ref[jj, 2 * p], zero], axis=1)
                bot = jnp.concatenate([zero, s0_ref[jj, 2 * p + 1]], axis=1)
                sbd_ref[j, p] = jnp.concatenate([top, bot], axis=0)
            _init_carry(prev_ref, prev0_ref, j, jj)
            _init_carry(prevwa_ref, prevwa0_ref, j, jj)

    w2 = 2 * c
    pm = _PairMasks(c)
    tril, strict = pm.tril, pm.strict
    chl = lax.broadcasted_iota(jnp.int32, (c, LANES), 1)
    ch0 = (chl < n).astype(F32).astype(BF16)
    ch1 = (chl >= n).astype(F32).astype(BF16)
    row2 = lax.broadcasted_iota(jnp.int32, (LANES, LANES), 0)
    lane2 = lax.broadcasted_iota(jnp.int32, (LANES, LANES), 1)
    log2n = n.bit_length() - 1
    same_head = (row2 >> log2n) == (lane2 >> log2n)
    seg_ones = same_head.astype(F32).astype(BF16)

    def bd_ch(xb):
        return jnp.concatenate([xb * ch0, xb * ch1], axis=0)

    dot = _dotb

    def seg_sum(x):
        stk = jnp.concatenate([x[:, i * LANES:(i + 1) * LANES] for i in range(ngrp)], axis=0)
        acc = dot(stk.astype(BF16), seg_ones)
        return jnp.concatenate([acc[i * c:(i + 1) * c] for i in range(ngrp)], axis=1)

    lora = w2_ref.shape[0]
    row_c = lax.broadcasted_iota(jnp.int32, (c, c), 0)
    col_c = lax.broadcasted_iota(jnp.int32, (c, c), 1)
    tril_b = (row_c >= col_c).astype(BF16)

    def pre_steps(out):
        r = [_token_shift(x_ref, prev_ref, mu_ref, j, 0, d) for j in seqs]
        kb = [_token_shift(x_ref, prev_ref, mu_ref, j, d, 2 * d) for j in seqs]
        vb = [_token_shift(x_ref, prev_ref, mu_ref, j, 2 * d, 3 * d) for j in seqs]
        sz = [_silu(_token_shift(x_ref, prev_ref, mu_ref, j, 3 * d, 4 * d)) for j in seqs]
        wa = [wa_ref[j] for j in seqs]
        wa = [wa[j] + muwa_ref[...] * (_shift_rows(wa[j], prevwa_ref[j], 1) - wa[j])
              for j in seqs]
        tw = [jnp.tanh(x[:, 0:lora]) for x in wa]
        w_raw = [w0_ref[...] + _mm(tw[j], w2_ref[...]) for j in seqs]
        a = [_sigmoid(a0_ref[...] + _mm(wa[j][:, lora:], a2_ref[...])) for j in seqs]
        w_log = [-jnp.exp(-_softplus(-x) - 0.5) for x in w_raw]
        kku = [kb[j] * kk_ref[...] for j in seqs]
        k2 = [kb[j] * (1.0 + (a[j] - 1.0) * ka_ref[...]) for j in seqs]
        if t_valid < t_total:
            ok = (step * c + lax.broadcasted_iota(jnp.int32, (c, d), 0)) < t_valid
            w_log = [jnp.where(ok, x, 0.0) for x in w_log]
            kku = [jnp.where(ok, x, 0.0) for x in kku]
            vb = [jnp.where(ok, x, 0.0) for x in vb]
        gcum = [_cumsum_rows(tril_b, x) for x in w_log]
        glast = [x[c - 1:c, :] for x in gcum]
        g_in = [jnp.exp(x) for x in gcum]
        g_ex = [jnp.exp(gcum[j] - w_log[j]) for j in seqs]
        g_inv = [jnp.exp(-x) for x in gcum]
        g_tail = [jnp.exp(glast[j] - gcum[j]) for j in seqs]
        ssq = [x * x for x in kku]
        bonus_in = [r[j] * k2[j] * rk_ref[...] for j in seqs]
        kk = [kku[j] * lax.rsqrt(seg_sum(ssq[j]) + L2_EPS) for j in seqs]
        bonus = [seg_sum(bonus_in[j]) * vb[j] for j in seqs]
        for j in seqs:
            b = kk[j] * a[j]
            out[j].update(
                kkg=(kk[j] * g_ex[j]).astype(BF16), rg=(r[j] * g_in[j]).astype(BF16),
                bh=(b * g_inv[j]).astype(BF16), kh=(k2[j] * g_inv[j]).astype(BF16),
                kt=(k2[j] * g_tail[j]).astype(BF16), bt=(b * g_tail[j]).astype(BF16),
                v=vb[j].astype(BF16), bonus=bonus[j], sz=sz[j], g_all=jnp.exp(glast[j]))

    units = [(j, p) for j in seqs for p in range(npair)]
    us = range(len(units))

    def stage_steps(ops):
        kkg, rg, bh, kh, kt, bt, v = (
            [ops[j][k][:, p * LANES:(p + 1) * LANES] for j, p in units] for k in _RWKV_OPERANDS)
        g_all = [ops[j]["g_all"][:, p * LANES:(p + 1) * LANES] for j, p in units]
        lhs = [jnp.concatenate([kkg[u], rg[u]], axis=0) for u in us]
        rhs = [jnp.concatenate([bd_ch(bh[u]), bd_ch(kh[u])], axis=0) for u in us]
        big = [dot(lhs[u], rhs[u], NT) for u in us]
        s_old = [sbd_ref[j, p] for j, p in units]
        from_s = [dot(lhs[u], s_old[u].astype(BF16), NT) for u in us]
        low = [jnp.where(strict, x[0:c, 0:w2], 0.0) for x in big]
        m_kv = [jnp.where(strict, x[0:c, w2:2 * w2], 0.0) for x in big]
        q_b = [jnp.where(tril, x[c:2 * c, 0:w2], 0.0) for x in big]
        p_kv = [jnp.where(tril, x[c:2 * c, w2:2 * w2], 0.0) for x in big]
        mp = [dot(jnp.concatenate([m_kv[u], p_kv[u]], axis=0).astype(BF16), bd_ch(v[u]))
              for u in us]
        nn = _tri_inv_pairs(low, pm)
        rhs_sa = [from_s[u][0:c] + mp[u][0:c] for u in us]
        sa = [rhs_sa[u] + dot(nn[u].astype(BF16), bd_ch(rhs_sa[u].astype(BF16))) for u in us]
        sab = [x.astype(BF16) for x in sa]
        qs = [dot(q_b[u].astype(BF16), bd_ch(sab[u])) for u in us]
        upd = [dot(jnp.concatenate([v[u], -sab[u]], axis=0),
                   jnp.concatenate([kt[u], bt[u]], axis=0), TN) for u in us]
        s_new = [s_old[u] * g_all[u] + jnp.where(same_head, upd[u], 0.0) for u in us]
        for u, (j, p) in enumerate(units):
            sbd_ref[j, p] = s_new[u]

        @pl.when(step == pl.num_programs(1) - 1)
        def _():
            for u, (j, p) in enumerate(units):
                s_ref[j, 2 * p] = s_new[u][0:n, 0:n]
                s_ref[j, 2 * p + 1] = s_new[u][n:2 * n, n:2 * n]

        y = [from_s[u][c:2 * c] + mp[u][c:2 * c] - qs[u] for u in us]
        yj = [jnp.concatenate(y[j * npair:(j + 1) * npair], axis=1) for j in seqs]
        yc = [yj[j] - seg_sum(yj[j]) * (1.0 / n) for j in seqs]
        var = [seg_sum(yc[j] * yc[j]) * (1.0 / n) for j in seqs]
        for j in seqs:
            yn = yc[j] * lax.rsqrt(var[j] + n * GN_EPS_PER_CH) * gnw_ref[...] + gnb_ref[...]
            o_ref[j] = ((yn + ops[j]["bonus"]) * ops[j]["sz"]).astype(BF16)

    ops = [dict() for _ in seqs]
    pre_steps(ops)
    stage_steps(ops)

    for j in seqs:
        prev_ref[j] = x_ref[j, c - SUBLANES:c, :]
        prevwa_ref[j] = wa_ref[j, c - SUBLANES:c, :]


def _rwkv_parts(p, s0, prev0, prevwa0, mu, muwa, w0, w2, a0, a2, kk, ka, rk, gnw, gnb,
                *, c, ns, t_valid, lay):
    nb, t, _ = p.shape
    nh, n = s0.shape[1:3]
    d = nh * n
    wx = lay["rkvz"][1]
    const2 = lambda b, i: (0, 0)
    vec = pl.BlockSpec((1, d), const2)
    assert 2 * n == LANES and c <= n and c & (c - 1) == 0, "two heads per lane group"
    return dict(
        operands=[p, p, s0, prev0, prevwa0, mu, muwa, w0, w2, a0, a2, kk, ka, rk, gnw, gnb],
        in_specs=[
            pl.BlockSpec((ns, c, wx), lambda b, i: (b, i, lay["rkvz"][0] // wx)),
            pl.BlockSpec((ns, c, LANES), lambda b, i: (b, i, lay["wa"][0] // LANES)),
            *_state_specs(s0, ns, [prev0, prevwa0]),
            pl.BlockSpec((1, wx), const2),
            pl.BlockSpec((1, LANES), const2),
            vec,
            pl.BlockSpec(w2.shape, const2),
            vec,
            pl.BlockSpec(a2.shape, const2),
            vec, vec, vec, vec, vec,
        ],
        out_shape=[jax.ShapeDtypeStruct((nb, t, d), BF16),
                   jax.ShapeDtypeStruct((nb, nh, n, n), F32)],
        out_specs=[pl.BlockSpec((ns, c, d), lambda b, i: (b, i, 0)),
                   pl.BlockSpec((ns, nh, n, n), lambda b, i: (b, 0, 0, 0))],
        scratch=[pltpu.VMEM((ns, SUBLANES, wx), F32), pltpu.VMEM((ns, SUBLANES, LANES), F32),
                 pltpu.VMEM((ns, nh // 2, LANES, LANES), F32)],
        statics=dict(c=c, ns=ns, nh=nh, n=n, t_valid=t_valid, t_total=t))


def _recurrence(body, parts, *, name):
    statics = parts["statics"]
    nb, t = parts["out_shape"][0].shape[:2]
    return pl.pallas_call(
        functools.partial(body, **statics),
        out_shape=tuple(parts["out_shape"]),
        grid=(nb // statics["ns"], t // statics["c"]),
        in_specs=parts["in_specs"],
        out_specs=tuple(parts["out_specs"]),
        scratch_shapes=parts["scratch"],
        compiler_params=pltpu.CompilerParams(
            dimension_semantics=("parallel", "arbitrary"), vmem_limit_bytes=VMEM_LIMIT),
        name=name,
    )(*parts["operands"])


def _merge_kernel(oa_ref, ob_ref, gate_ref, x_ref, woa_ref, wob_ref, wo_ref, lnf_ref, y_ref):
    d = x_ref.shape[1]
    ba = jnp.dot(oa_ref[...], woa_ref[...], preferred_element_type=F32)
    bb = jnp.dot(ob_ref[...], wob_ref[...], preferred_element_type=F32)
    gates = _sigmoid(gate_ref[...])
    merged = gates[:, :d] * ba + gates[:, d:] * bb
    xn = x_ref[...] + jnp.dot(merged.astype(BF16), wo_ref[...], preferred_element_type=F32)
    y_ref[...] = xn * lax.rsqrt(jnp.mean(xn * xn, axis=-1, keepdims=True) + EPS) * lnf_ref[...]


def _merge(oa, ob, p, x, woa, wob, wo, lnf, *, lay):
    m, d = x.shape
    tm = min(m, 512)
    wg = lay["gate"][1]
    const2 = lambda i: (0, 0)
    return pl.pallas_call(
        _merge_kernel,
        out_shape=jax.ShapeDtypeStruct((m, d), F32),
        grid=(m // tm,),
        in_specs=[
            pl.BlockSpec((tm, oa.shape[1]), lambda i: (i, 0)),
            pl.BlockSpec((tm, ob.shape[1]), lambda i: (i, 0)),
            pl.BlockSpec((tm, wg), lambda i: (i, lay["gate"][0] // wg)),
            pl.BlockSpec((tm, d), lambda i: (i, 0)),
            pl.BlockSpec(woa.shape, const2),
            pl.BlockSpec(wob.shape, const2),
            pl.BlockSpec(wo.shape, const2),
            pl.BlockSpec((1, d), const2),
        ],
        out_specs=pl.BlockSpec((tm, d), lambda i: (i, 0)),
        compiler_params=pltpu.CompilerParams(
            dimension_semantics=("parallel",), vmem_limit_bytes=VMEM_LIMIT),
        name="merge",
    )(oa, ob, p, x, woa, wob, wo, lnf)


def _pad_lanes(v, width):
    return jnp.pad(v, ((0, 0), (0, width - v.shape[1])))


def _token_major_t(ab, c):
    nb, t, k = ab.shape
    return jnp.transpose(ab.reshape(nb, t // c, c, k), (0, 1, 3, 2))


def kernel(x_prompt, x_sample, state_gdn, state_gdn_conv, state_rwkv, state_shift, meta_tokens,
           ln1_w, w_in, gdn_conv_w, gdn_a_log, gdn_dt_bias, gdn_norm_w, w_out_a, rwkv_mu, rwkv_w0,
           rwkv_w2, rwkv_a0, rwkv_a2, rwkv_k_k, rwkv_k_a, rwkv_r_k, rwkv_gn_w, rwkv_gn_b, w_out_b,
           w_out, lnf_w):
    assert ln1_w.shape[0] == 1, "single-layer trunk"
    bp, seq, d = x_prompt.shape
    bs, tseq, _ = x_sample.shape
    n_meta = meta_tokens.shape[0]
    _, _, nh_a, dk, dv = state_gdn.shape
    _, _, nh_b, n_b, _ = state_rwkv.shape
    kw, w_qkv = gdn_conv_w.shape[1:]
    lora_w = rwkv_w2.shape[1]
    lora_a = rwkv_a2.shape[1]
    d_a = nh_a * dv
    d_b = nh_b * n_b
    assert w_qkv == 2 * nh_a * dk + d_a and d_a == d and d_b == d and lora_w + lora_a == LANES
    assert n_meta % SUBLANES == 0 and n_meta % 16 == 0 and tseq >= kw - 1

    o_a = w_qkv
    o_b = o_a + nh_a
    o_z = o_b + nh_a
    o_r = o_z + d_a
    o_g = o_r + 3 * d_b + lora_w + lora_a + d_b
    w = w_in[0].T
    wr = w[o_r:o_g]
    mu = rwkv_mu
    lora0 = 3 * d_b
    rows = [w[:w_qkv], w[o_z:o_r], wr[:lora0], wr[lora0 + LANES:], w[o_g:],
            wr[lora0:lora0 + LANES], w[o_a:o_z]]
    used = sum(rw.shape[0] for rw in rows)
    n_pad = -(-(used + LANES - 2 * nh_a) // PROJ_TN) * PROJ_TN
    rows.append(jnp.zeros((n_pad - used, d), F32))
    w_all = jnp.concatenate([rw.astype(BF16) for rw in rows], axis=0)
    lay = {"qkv": (0, w_qkv), "za": (w_qkv, d_a), "rkvz": (w_qkv + d_a, 4 * d_b),
           "gate": (w_qkv + d_a + 4 * d_b, 2 * d)}
    lay["wa"] = (lay["gate"][0] + 2 * d, LANES)
    lay["ab"] = (lay["wa"][0] + LANES, LANES)
    for off, width in lay.values():
        assert off % width == 0
    mu_x = jnp.concatenate([mu[:, :lora0], mu[:, lora0 + LANES:]], axis=1)
    mu_wa = mu[:, lora0:lora0 + LANES]

    alr = _pad_lanes(gdn_a_log, LANES)
    dtr = _pad_lanes(gdn_dt_bias, LANES)
    alc = jnp.pad(gdn_a_log.reshape(nh_a, 1), ((0, nh_a), (0, 0)))
    dtc = jnp.pad(gdn_dt_bias.reshape(nh_a, 1), ((0, nh_a), (0, 0)))
    convw = gdn_conv_w[0]
    rk = rwkv_r_k.reshape(1, d_b)
    woa = w_out_a[0].astype(BF16)
    wob = w_out_b[0].astype(BF16)
    wo = w_out[0].astype(BF16)
    lnf = lnf_w.reshape(1, d)

    def branches(p, c, ns_a, ns_b, t_valid, s_gdn, conv_tail, s_rwkv, x_tail, wa_tail):
        abt = _token_major_t(p[:, :, lay["ab"][0]:lay["ab"][0] + 2 * nh_a], c)
        oa, sg = _recurrence(_gdn_kernel, _gdn_parts(
            p, abt, s_gdn, conv_tail, convw, alr, dtr, alc, dtc, gdn_norm_w,
            c=c, ns=ns_a, t_valid=t_valid, lay=lay), name="gdn")
        ob, sr = _recurrence(_rwkv_kernel, _rwkv_parts(
            p, s_rwkv, x_tail, wa_tail, mu_x, mu_wa, rwkv_w0, rwkv_w2[0], rwkv_a0, rwkv_a2[0],
            rwkv_k_k, rwkv_k_a, rk, rwkv_gn_w, rwkv_gn_b,
            c=c, ns=ns_b, t_valid=t_valid, lay=lay), name="rwkv")
        return oa, sg, ob, sr

    x0, wx = lay["rkvz"]
    a0_, _ = lay["wa"]

    tpad = SUBLANES
    assert n_meta <= bs and tseq < tpad
    rider = jnp.zeros((bs, tpad - tseq, d), F32).at[:n_meta, 0].set(meta_tokens)
    xs = jnp.concatenate([x_sample, rider], axis=1).reshape(bs * tpad, d)
    p_s = _proj(xs, ln1_w, w_all, apply_norm=True)
    p_m = p_s.reshape(bs, tpad, n_pad)[:n_meta, tseq]

    _, sg_m, _, sr_m = branches(
        p_m[None], n_meta, 1, 1, n_meta,
        jnp.zeros((1, nh_a, dk, dv), F32), jnp.zeros((1, 1, w_qkv), F32),
        jnp.zeros((1, nh_b, n_b, n_b), F32), jnp.zeros((1, 1, wx), F32),
        jnp.zeros((1, 1, LANES), F32))

    xp = x_prompt.reshape(bp * seq, d)
    p_p = _proj(xp, ln1_w, w_all, apply_norm=True)
    last_m = p_m[n_meta - 1:, None]
    oa_p, sg_p, ob_p, sr_p = branches(
        p_p.reshape(bp, seq, n_pad), 64, 4, 2, seq, sg_m,
        p_m[n_meta - (kw - 1):, None, :w_qkv], sr_m,
        last_m[:, :, x0:x0 + wx], last_m[:, :, a0_:a0_ + LANES])
    y_p = _merge(oa_p.reshape(bp * seq, d), ob_p.reshape(bp * seq, d), p_p, xp, woa, wob, wo, lnf,
                 lay=lay)

    p_first =_proj(state_shift[0], ln1_w, w_all, apply_norm=False)[None]
    oa_s, sg_s, ob_s, sr_s = branches(
        p_s.reshape(bs, tpad, n_pad), tpad, 8, 8, tseq, state_gdn[0],
        jnp.transpose(state_gdn_conv[0], (1, 0, 2)), state_rwkv[0],
        p_first[:, :, x0:x0 + wx], p_first[:, :, a0_:a0_ + LANES])
    y_s = _merge(oa_s.reshape(bs * tpad, d), ob_s.reshape(bs * tpad, d), p_s, xs, woa, wob, wo,
                 lnf, lay=lay)

    shift_p = _rmsnorm_rows(x_prompt[:, -1], ln1_w)
    shift_s = _rmsnorm_rows(x_sample[:, -1], ln1_w)
    conv_p = p_p.reshape(bp, seq, n_pad)[:, seq - (kw - 1):, :w_qkv]
    conv_s = p_s.reshape(bs, tpad, n_pad)[:, tseq - (kw - 1):tseq, :w_qkv]
    return (y_p.reshape(bp, seq, d), y_s.reshape(bs, tpad, d)[:, :tseq],
            sg_p[None], conv_p[None], sr_p[None], shift_p[None],
            sg_s[None], conv_s[None], sr_s[None], shift_s[None])
```

```python
import functools

import jax
import jax.numpy as jnp
from jax import lax
from jax.experimental import pallas as pl
from jax.experimental.pallas import tpu as pltpu

F32 = jnp.float32
BF16 = jnp.bfloat16

EPS = 1e-6
L2_EPS = 1e-6
GN_EPS_PER_CH = 1e-5

LANES = 128
SUBLANES = 8
VMEM_LIMIT = 32 * 1024 * 1024
PROJ_TM = 1024
PROJ_TN = 1536
PROMPT_CHUNK = 64
PROMPT_SEQS = (4, 2)
SAMPLE_SEQS = (8, 8)

NN = (((1,), (0,)), ((), ()))
NT = (((1,), (1,)), ((), ()))
TN = (((0,), (0,)), ((), ()))


def _mm(a, b, dims=NN):
    return lax.dot_general(a.astype(BF16), b.astype(BF16), dims, preferred_element_type=F32)


def _dotb(a, b, dims=NN):
    return lax.dot_general(a, b, dims, preferred_element_type=F32)


def _split3(x):
    hi = x.astype(BF16)
    rem = x - hi.astype(F32)
    mid = rem.astype(BF16)
    return hi, mid, (rem - mid.astype(F32)).astype(BF16)


def _cumsum_rows(tril01, x):
    hi, mid, lo = _split3(x)
    return _dotb(tril01, hi) + _dotb(tril01, mid) + _dotb(tril01, lo)


def _cumsum_lanes(x, triu01):
    hi, mid, lo = _split3(x)
    return _dotb(hi, triu01) + _dotb(mid, triu01) + _dotb(lo, triu01)


def _sigmoid(x):
    return 1.0 / (1.0 + jnp.exp(-x))


def _silu(x):
    return x * _sigmoid(x)


def _softplus(x):
    return jnp.maximum(x, 0.0) + jnp.log(1.0 + jnp.exp(-jnp.abs(x)))


def _shift_rows(cur, prev, k):
    rolled = pltpu.roll(cur, k, 0)
    fix = pltpu.roll(prev, k, 0)
    wrap = lax.broadcasted_iota(jnp.int32, fix.shape, 0) < k
    head = jnp.where(wrap, fix, rolled[0:SUBLANES])
    if cur.shape[0] == SUBLANES:
        return head
    return jnp.concatenate([head, rolled[SUBLANES:]], axis=0)


class _PairMasks:
    def __init__(self, c):
        row = lax.broadcasted_iota(jnp.int32, (c, 2 * c), 0)
        lane = lax.broadcasted_iota(jnp.int32, (c, 2 * c), 1)
        col = lane & (c - 1)
        self.tril = row >= col
        self.strict = row > col
        diff = row ^ col
        self.levels = [(diff >> 1) == 0]
        sh = 2
        while (1 << sh) <= c:
            self.levels.append((diff >> (sh - 1)) == 1)
            sh += 1
        self.head0 = lane < c
        self.tok0 = self.head0.astype(F32).astype(BF16)
        self.tok1 = (lane >= c).astype(F32).astype(BF16)

    def block_diag(self, xb):
        return jnp.concatenate([xb * self.tok0, xb * self.tok1], axis=0)


def _tri_inv_pairs(lows, pm):
    nn = [-jnp.where(pm.levels[0], x, 0.0) for x in lows]
    for m in pm.levels[1:]:
        off = [jnp.where(m, x, 0.0) for x in lows]
        nnb = [x.astype(BF16) for x in nn]
        xc = [o + _dotb(xb, pm.block_diag(o.astype(BF16))) for o, xb in zip(off, nnb)]
        nn = [x - (y + _dotb(y.astype(BF16), pm.block_diag(xb)))
              for x, y, xb in zip(nn, xc, nnb)]
    return nn


def _proj_kernel(x_ref, lnw_ref, w_ref, o_ref, h_ref, *, apply_norm):
    @pl.when(pl.program_id(1) == 0)
    def _():
        x = x_ref[...]
        if apply_norm:
            x = x * lax.rsqrt(jnp.mean(x * x, axis=-1, keepdims=True) + EPS)
            x = x * lnw_ref[...]
        h_ref[...] = x.astype(BF16)

    o_ref[...] = lax.dot_general(h_ref[...], w_ref[...], NT, preferred_element_type=F32)


def _proj(x, lnw, w, *, apply_norm):
    m, d = x.shape
    n = w.shape[0]
    tm = min(m, PROJ_TM)
    tn = PROJ_TN
    assert m % tm == 0 and n % tn == 0
    vmem = 2 * (tm * d * 4 + tn * d * 2 + tm * tn * 4) + tm * d * 2 + (4 << 20)
    return pl.pallas_call(
        functools.partial(_proj_kernel, apply_norm=apply_norm),
        out_shape=jax.ShapeDtypeStruct((m, n), F32),
        grid=(m // tm, n // tn),
        in_specs=[pl.BlockSpec((tm, d), lambda i, j: (i, 0)),
                  pl.BlockSpec((1, d), lambda i, j: (0, 0)),
                  pl.BlockSpec((tn, d), lambda i, j: (j, 0))],
        out_specs=pl.BlockSpec((tm, tn), lambda i, j: (i, j)),
        scratch_shapes=[pltpu.VMEM((tm, d), BF16)],
        compiler_params=pltpu.CompilerParams(
            dimension_semantics=("parallel", "arbitrary"), vmem_limit_bytes=vmem),
        name="proj",
    )(x, lnw, w)


def _rmsnorm_kernel(x_ref, w_ref, o_ref):
    x = x_ref[...]
    o_ref[...] = x * lax.rsqrt(jnp.mean(x * x, axis=-1, keepdims=True) + EPS) * w_ref[...]


def _rmsnorm_rows(x, w):
    return pl.pallas_call(
        _rmsnorm_kernel,
        out_shape=jax.ShapeDtypeStruct(x.shape, F32),
        name="rmsnorm_rows",
    )(x, w)


def _gdn_kernel(qkv_ref, z_ref, ab_ref, abt_ref, s0_ref, prev0_ref, convw_ref,
                alr_ref, dtr_ref, alc_ref, dtc_ref, nw_ref,
                o_ref, s_ref, prev_ref, *, c, ns, nh, dk, dv, t_valid, t_total):
    ci = pl.program_id(1)
    shared = s0_ref.shape[0] == 1

    @pl.when(ci == 0)
    def _():
        for j in range(ns):
            s_ref[j] = s0_ref[0 if shared else j]
            _init_carry(prev_ref, prev0_ref, j, 0 if shared else j)

    pm = _PairMasks(c)
    row = lax.broadcasted_iota(jnp.int32, (c, c), 0)
    col = lax.broadcasted_iota(jnp.int32, (c, c), 1)
    tril = row >= col
    kw = convw_ref.shape[0]
    seqs = range(ns)

    ab = [ab_ref[j] for j in seqs]
    g_c = [-jnp.exp(alr_ref[...]) * _softplus(x + dtr_ref[...]) for x in ab]
    beta_c = [_sigmoid(x) for x in ab]
    abt = [abt_ref[j, 0] for j in seqs]
    g_r = [-jnp.exp(alc_ref[...]) * _softplus(x + dtc_ref[...]) for x in abt]
    if t_valid < t_total:
        tok_c = ci * c + lax.broadcasted_iota(jnp.int32, ab[0].shape, 0)
        tok_r = ci * c + lax.broadcasted_iota(jnp.int32, abt[0].shape, 1)
        g_c = [jnp.where(tok_c < t_valid, x, 0.0) for x in g_c]
        beta_c = [jnp.where(tok_c < t_valid, x, 0.0) for x in beta_c]
        g_r = [jnp.where(tok_r < t_valid, x, 0.0) for x in g_r]
    gc_c = _cumsum_rows(tril.astype(BF16), jnp.concatenate(g_c, axis=1))
    gc_r = _cumsum_lanes(jnp.concatenate(g_r, axis=0), (row <= col).astype(BF16))

    def conv(j, col0, width):
        cur = qkv_ref[j, :, col0:col0 + width]
        prev = prev_ref[j, :, col0:col0 + width]
        acc = None
        for k in range(kw - 1, -1, -1):
            src = cur if k == 0 else _shift_rows(cur, prev, k)
            term = src * convw_ref[kw - 1 - k:kw - k, col0:col0 + width]
            acc = term if acc is None else acc + term
        return _silu(acc)

    units = [(j, h) for j in seqs for h in range(nh)]
    us = range(len(units))
    gcc = [gc_c[:, j * LANES + h:j * LANES + h + 1] for j, h in units]
    gcr = [gc_r[j * 2 * nh + h:j * 2 * nh + h + 1, :] for j, h in units]
    beta = [beta_c[j][:, nh + h:nh + h + 1] for j, h in units]
    glast = [x[c - 1:c, :] for x in gcc]
    q = [conv(j, h * dk, dk) for j, h in units]
    k = [conv(j, nh * dk + h * dk, dk) for j, h in units]
    v = [conv(j, 2 * nh * dk + h * dv, dv) for j, h in units]
    q = [x * lax.rsqrt(jnp.sum(x * x, axis=-1, keepdims=True) + L2_EPS) * (dk ** -0.5) for x in q]
    k = [x * lax.rsqrt(jnp.sum(x * x, axis=-1, keepdims=True) + L2_EPS) for x in k]
    kb = [k[u] * beta[u] for u in us]
    vb = [v[u] * beta[u] for u in us]
    kbg = [kb[u] * jnp.exp(gcc[u]) for u in us]
    qg = [q[u] * jnp.exp(gcc[u]) for u in us]
    k_tail = [k[u] * jnp.exp(glast[u] - gcc[u]) for u in us]

    pairs = [(u, u + 1) for u in us if u % 2 == 0]

    def side_by_side(xs):
        return [jnp.concatenate([xs[a].astype(BF16), xs[b].astype(BF16)], axis=1) for a, b in pairs]

    def per_head_rows(xs):
        out = []
        for a, b in pairs:
            xa, xb = xs[a].astype(BF16), xs[b].astype(BF16)
            out.append(jnp.concatenate([jnp.concatenate([xa, jnp.zeros_like(xb)], axis=1),
                                        jnp.concatenate([jnp.zeros_like(xa), xb], axis=1)], axis=0))
        return out

    def split(xs, width):
        return [x[:, i * width:(i + 1) * width] for x in xs for i in range(2)]

    gcc_s = [jnp.where(pm.head0, gcc[a], gcc[b]) for a, b in pairs]
    gcr_s = [jnp.concatenate([gcr[a], gcr[b]], axis=1) for a, b in pairs]
    decay = [jnp.where(pm.tril, jnp.exp(jnp.where(pm.tril, x - y, 0.0)), 0.0)
             for x, y in zip(gcc_s, gcr_s)]
    k_rows = per_head_rows(k)
    lhs = [jnp.concatenate([x, y], axis=0) for x, y in zip(side_by_side(kb), side_by_side(q))]
    both = [_dotb(x, y, NT) for x, y in zip(lhs, k_rows)]
    lower = [jnp.where(pm.strict, x[0:c] * d_, 0.0) for x, d_ in zip(both, decay)]
    qk = [jnp.where(pm.tril, x[c:2 * c] * d_, 0.0) for x, d_ in zip(both, decay)]
    s_old = [s_ref[j, h] for j, h in units]
    qs = [_mm(qg[u], s_old[u]) for u in us]
    nn = [x.astype(BF16) for x in _tri_inv_pairs(lower, pm)]
    u_ = [vb[u] + x for u, x in enumerate(split(
        [_dotb(x, y) for x, y in zip(nn, per_head_rows(vb))], dv))]
    w_ = [kbg[u] + x for u, x in enumerate(split(
        [_dotb(x, y) for x, y in zip(nn, per_head_rows(kbg))], dk))]
    v_new = [u_[u] - _mm(w_[u], s_old[u]) for u in us]
    kv = [_mm(k_tail[u], v_new[u], TN) for u in us]
    for u, (j, h) in enumerate(units):
        s_ref[j, h] = s_old[u] * jnp.exp(glast[u]) + kv[u]
    o = [qs[u] + x for u, x in enumerate(split(
        [_dotb(x.astype(BF16), y) for x, y in zip(qk, per_head_rows(v_new))], dv))]
    for u, (j, h) in enumerate(units):
        oh = o[u] * lax.rsqrt(jnp.mean(o[u] * o[u], axis=-1, keepdims=True) + EPS) * nw_ref[...]
        oh = oh * _silu(z_ref[j, :, h * dv:(h + 1) * dv])
        o_ref[j, :, h * dv:(h + 1) * dv] = oh.astype(BF16)

    for j in seqs:
        prev_ref[j] = qkv_ref[j, c - SUBLANES:c, :]


def _state_specs(s0, ns, tails):
    shared = s0.shape[0] == 1
    lead = 1 if shared else ns
    idx4 = (lambda b, i: (0, 0, 0, 0)) if shared else (lambda b, i: (b, 0, 0, 0))
    idx3 = (lambda b, i: (0, 0, 0)) if shared else (lambda b, i: (0, b, 0))
    return ([pl.BlockSpec((lead,) + s0.shape[1:], idx4)]
            + [pl.BlockSpec((tl.shape[0], lead, tl.shape[2]), idx3) for tl in tails])


def _init_carry(prev_ref, tail_ref, j, jj):
    k = tail_ref.shape[0]
    if k < SUBLANES:
        prev_ref[j] = jnp.zeros(prev_ref.shape[1:], prev_ref.dtype)
    for r in range(k):
        row = SUBLANES - k + r
        prev_ref[j, row:row + 1, :] = tail_ref[r, jj:jj + 1, :]


def _gdn_parts(p, abt, s0, prev0, convw, alr, dtr, alc, dtc, nw, *, c, ns, t_valid, lay):
    nb, t, _ = p.shape
    nh, dk, dv = s0.shape[1:]
    wq = lay["qkv"][1]
    const2 = lambda b, i: (0, 0)
    return dict(
        operands=[p, p, p, abt, s0, prev0, convw, alr, dtr, alc, dtc, nw],
        in_specs=[
            pl.BlockSpec((ns, c, wq), lambda b, i: (b, i, lay["qkv"][0] // wq)),
            pl.BlockSpec((ns, c, nh * dv), lambda b, i: (b, i, lay["za"][0] // (nh * dv))),
            pl.BlockSpec((ns, c, LANES), lambda b, i: (b, i, lay["ab"][0] // LANES)),
            pl.BlockSpec((ns, 1, 2 * nh, c), lambda b, i: (b, i, 0, 0)),
            *_state_specs(s0, ns, [prev0]),
            pl.BlockSpec(convw.shape, const2),
            pl.BlockSpec(alr.shape, const2),
            pl.BlockSpec(dtr.shape, const2),
            pl.BlockSpec(alc.shape, const2),
            pl.BlockSpec(dtc.shape, const2),
            pl.BlockSpec(nw.shape, const2),
        ],
        out_shape=[jax.ShapeDtypeStruct((nb, t, nh * dv), BF16),
                   jax.ShapeDtypeStruct((nb, nh, dk, dv), F32)],
        out_specs=[pl.BlockSpec((ns, c, nh * dv), lambda b, i: (b, i, 0)),
                   pl.BlockSpec((ns, nh, dk, dv), lambda b, i: (b, 0, 0, 0))],
        scratch=[pltpu.VMEM((ns, SUBLANES, wq), F32)],
        statics=dict(c=c, ns=ns, nh=nh, dk=dk, dv=dv, t_valid=t_valid, t_total=t))


def _token_shift(x_ref, prev_ref, mu_ref, j, lo, hi):
    cur = x_ref[j, :, lo:hi]
    prv = _shift_rows(cur, prev_ref[j, :, lo:hi], 1)
    return cur + mu_ref[:, lo:hi] * (prv - cur)


_RWKV_OPERANDS = ("kkg", "rg", "bh", "kh", "kt", "bt", "v")


def _rwkv_kernel(x_ref, wa_ref, s0_ref, prev0_ref, prevwa0_ref, mu_ref, muwa_ref,
                 w0_ref, w2_ref, a0_ref, a2_ref, kk_ref, ka_ref, rk_ref, gnw_ref, gnb_ref,
                 o_ref, s_ref, prev_ref, prevwa_ref, sbd_ref,
                 *, c, ns, nh, n, t_valid, t_total):
    step = pl.program_id(1)
    d = nh * n
    npair = nh // 2
    ngrp = d // LANES
    shared = s0_ref.shape[0] == 1
    seqs = range(ns)

    @pl.when(step == 0)
    def _():
        zero = jnp.zeros((n, n), F32)
        for j in seqs:
            jj = 0 if shared else j
            for p in range(npair):
                top = jnp.concatenate([s0_ref[jj, 2 * p], zero], axis=1)
                bot = jnp.concatenate([zero, s0_ref[jj, 2 * p + 1]], axis=1)
                sbd_ref[j, p] = jnp.concatenate([top, bot], axis=0)
            _init_carry(prev_ref, prev0_ref, j, jj)
            _init_carry(prevwa_ref, prevwa0_ref, j, jj)

    w2 = 2 * c
    pm = _PairMasks(c)
    tril, strict = pm.tril, pm.strict
    chl = lax.broadcasted_iota(jnp.int32, (c, LANES), 1)
    ch0 = (chl < n).astype(F32).astype(BF16)
    ch1 = (chl >= n).astype(F32).astype(BF16)
    row2 = lax.broadcasted_iota(jnp.int32, (LANES, LANES), 0)
    lane2 = lax.broadcasted_iota(jnp.int32, (LANES, LANES), 1)
    log2n = n.bit_length() - 1
    same_head = (row2 >> log2n) == (lane2 >> log2n)
    seg_ones = same_head.astype(F32).astype(BF16)

    def bd_ch(xb):
        return jnp.concatenate([xb * ch0, xb * ch1], axis=0)

    dot = _dotb

    def seg_sum(x):
        stk = jnp.concatenate([x[:, i * LANES:(i + 1) * LANES] for i in range(ngrp)], axis=0)
        acc = dot(stk.astype(BF16), seg_ones)
        return jnp.concatenate([acc[i * c:(i + 1) * c] for i in range(ngrp)], axis=1)

    lora = w2_ref.shape[0]
    row_c = lax.broadcasted_iota(jnp.int32, (c, c), 0)
    col_c = lax.broadcasted_iota(jnp.int32, (c, c), 1)
    tril_b = (row_c >= col_c).astype(BF16)

    def pre_steps(out):
        r = [_token_shift(x_ref, prev_ref, mu_ref, j, 0, d) for j in seqs]
        kb = [_token_shift(x_ref, prev_ref, mu_ref, j, d, 2 * d) for j in seqs]
        vb = [_token_shift(x_ref, prev_ref, mu_ref, j, 2 * d, 3 * d) for j in seqs]
        sz = [_silu(_token_shift(x_ref, prev_ref, mu_ref, j, 3 * d, 4 * d)) for j in seqs]
        wa = [wa_ref[j] for j in seqs]
        wa = [wa[j] + muwa_ref[...] * (_shift_rows(wa[j], prevwa_ref[j], 1) - wa[j])
              for j in seqs]
        tw = [jnp.tanh(x[:, 0:lora]) for x in wa]
        w_raw = [w0_ref[...] + _mm(tw[j], w2_ref[...]) for j in seqs]
        a = [_sigmoid(a0_ref[...] + _mm(wa[j][:, lora:], a2_ref[...])) for j in seqs]
        w_log = [-jnp.exp(-_softplus(-x) - 0.5) for x in w_raw]
        kku = [kb[j] * kk_ref[...] for j in seqs]
        k2 = [kb[j] * (1.0 + (a[j] - 1.0) * ka_ref[...]) for j in seqs]
        if t_valid < t_total:
            ok = (step * c + lax.broadcasted_iota(jnp.int32, (c, d), 0)) < t_valid
            w_log = [jnp.where(ok, x, 0.0) for x in w_log]
            kku = [jnp.where(ok, x, 0.0) for x in kku]
            vb = [jnp.where(ok, x, 0.0) for x in vb]
        gcum = [_cumsum_rows(tril_b, x) for x in w_log]
        glast = [x[c - 1:c, :] for x in gcum]
        g_in = [jnp.exp(x) for x in gcum]
        g_ex = [jnp.exp(gcum[j] - w_log[j]) for j in seqs]
        g_inv = [jnp.exp(-x) for x in gcum]
        g_tail = [jnp.exp(glast[j] - gcum[j]) for j in seqs]
        ssq = [x * x for x in kku]
        bonus_in = [r[j] * k2[j] * rk_ref[...] for j in seqs]
        kk = [kku[j] * lax.rsqrt(seg_sum(ssq[j]) + L2_EPS) for j in seqs]
        bonus = [seg_sum(bonus_in[j]) * vb[j] for j in seqs]
        for j in seqs:
            b = kk[j] * a[j]
            out[j].update(
                kkg=(kk[j] * g_ex[j]).astype(BF16), rg=(r[j] * g_in[j]).astype(BF16),
                bh=(b * g_inv[j]).astype(BF16), kh=(k2[j] * g_inv[j]).astype(BF16),
                kt=(k2[j] * g_tail[j]).astype(BF16), bt=(b * g_tail[j]).astype(BF16),
                v=vb[j].astype(BF16), bonus=bonus[j], sz=sz[j], g_all=jnp.exp(glast[j]))

    units = [(j, p) for j in seqs for p in range(npair)]
    us = range(len(units))

    def stage_steps(ops):
        kkg, rg, bh, kh, kt, bt, v = (
            [ops[j][k][:, p * LANES:(p + 1) * LANES] for j, p in units] for k in _RWKV_OPERANDS)
        g_all = [ops[j]["g_all"][:, p * LANES:(p + 1) * LANES] for j, p in units]
        lhs = [jnp.concatenate([kkg[u], rg[u]], axis=0) for u in us]
        rhs = [jnp.concatenate([bd_ch(bh[u]), bd_ch(kh[u])], axis=0) for u in us]
        big = [dot(lhs[u], rhs[u], NT) for u in us]
        s_old = [sbd_ref[j, p] for j, p in units]
        from_s = [dot(lhs[u], s_old[u].astype(BF16), NT) for u in us]
        low = [jnp.where(strict, x[0:c, 0:w2], 0.0) for x in big]
        m_kv = [jnp.where(strict, x[0:c, w2:2 * w2], 0.0) for x in big]
        q_b = [jnp.where(tril, x[c:2 * c, 0:w2], 0.0) for x in big]
        p_kv = [jnp.where(tril, x[c:2 * c, w2:2 * w2], 0.0) for x in big]
        mp = [dot(jnp.concatenate([m_kv[u], p_kv[u]], axis=0).astype(BF16), bd_ch(v[u]))
              for u in us]
        nn = _tri_inv_pairs(low, pm)
        rhs_sa = [from_s[u][0:c] + mp[u][0:c] for u in us]
        sa = [rhs_sa[u] + dot(nn[u].astype(BF16), bd_ch(rhs_sa[u].astype(BF16))) for u in us]
        sab = [x.astype(BF16) for x in sa]
        qs = [dot(q_b[u].astype(BF16), bd_ch(sab[u])) for u in us]
        upd = [dot(jnp.concatenate([v[u], -sab[u]], axis=0),
                   jnp.concatenate([kt[u], bt[u]], axis=0), TN) for u in us]
        s_new = [s_old[u] * g_all[u] + jnp.where(same_head, upd[u], 0.0) for u in us]
        for u, (j, p) in enumerate(units):
            sbd_ref[j, p] = s_new[u]

        @pl.when(step == pl.num_programs(1) - 1)
        def _():
            for u, (j, p) in enumerate(units):
                s_ref[j, 2 * p] = s_new[u][0:n, 0:n]
                s_ref[j, 2 * p + 1] = s_new[u][n:2 * n, n:2 * n]

        y = [from_s[u][c:2 * c] + mp[u][c:2 * c] - qs[u] for u in us]
        yj = [jnp.concatenate(y[j * npair:(j + 1) * npair], axis=1) for j in seqs]
        yc = [yj[j] - seg_sum(yj[j]) * (1.0 / n) for j in seqs]
        var = [seg_sum(yc[j] * yc[j]) * (1.0 / n) for j in seqs]
        for j in seqs:
            yn = yc[j] * lax.rsqrt(var[j] + n * GN_EPS_PER_CH) * gnw_ref[...] + gnb_ref[...]
            o_ref[j] = ((yn + ops[j]["bonus"]) * ops[j]["sz"]).astype(BF16)

    ops = [dict() for _ in seqs]
    pre_steps(ops)
    stage_steps(ops)

    for j in seqs:
        prev_ref[j] = x_ref[j, c - SUBLANES:c, :]
        prevwa_ref[j] = wa_ref[j, c - SUBLANES:c, :]


def _rwkv_parts(p, s0, prev0, prevwa0, mu, muwa, w0, w2, a0, a2, kk, ka, rk, gnw, gnb,
                *, c, ns, t_valid, lay):
    nb, t, _ = p.shape
    nh, n = s0.shape[1:3]
    d = nh * n
    wx = lay["rkvz"][1]
    const2 = lambda b, i: (0, 0)
    vec = pl.BlockSpec((1, d), const2)
    assert 2 * n == LANES and c <= n and c & (c - 1) == 0, "two heads per lane group"
    return dict(
        operands=[p, p, s0, prev0, prevwa0, mu, muwa, w0, w2, a0, a2, kk, ka, rk, gnw, gnb],
        in_specs=[
            pl.BlockSpec((ns, c, wx), lambda b, i: (b, i, lay["rkvz"][0] // wx)),
            pl.BlockSpec((ns, c, LANES), lambda b, i: (b, i, lay["wa"][0] // LANES)),
            *_state_specs(s0, ns, [prev0, prevwa0]),
            pl.BlockSpec((1, wx), const2),
            pl.BlockSpec((1, LANES), const2),
            vec,
            pl.BlockSpec(w2.shape, const2),
            vec,
            pl.BlockSpec(a2.shape, const2),
            vec, vec, vec, vec, vec,
        ],
        out_shape=[jax.ShapeDtypeStruct((nb, t, d), BF16),
                   jax.ShapeDtypeStruct((nb, nh, n, n), F32)],
        out_specs=[pl.BlockSpec((ns, c, d), lambda b, i: (b, i, 0)),
                   pl.BlockSpec((ns, nh, n, n), lambda b, i: (b, 0, 0, 0))],
        scratch=[pltpu.VMEM((ns, SUBLANES, wx), F32), pltpu.VMEM((ns, SUBLANES, LANES), F32),
                 pltpu.VMEM((ns, nh // 2, LANES, LANES), F32)],
        statics=dict(c=c, ns=ns, nh=nh, n=n, t_valid=t_valid, t_total=t))


def _recurrence(body, parts, *, name):
    statics = parts["statics"]
    nb, t = parts["out_shape"][0].shape[:2]
    return pl.pallas_call(
        functools.partial(body, **statics),
        out_shape=tuple(parts["out_shape"]),
        grid=(nb // statics["ns"], t // statics["c"]),
        in_specs=parts["in_specs"],
        out_specs=tuple(parts["out_specs"]),
        scratch_shapes=parts["scratch"],
        compiler_params=pltpu.CompilerParams(
            dimension_semantics=("parallel", "arbitrary"), vmem_limit_bytes=VMEM_LIMIT),
        name=name,
    )(*parts["operands"])


def _merge_kernel(oa_ref, ob_ref, gate_ref, x_ref, woa_ref, wob_ref, wo_ref, lnf_ref, y_ref):
    d = x_ref.shape[1]
    ba = jnp.dot(oa_ref[...], woa_ref[...], preferred_element_type=F32)
    bb = jnp.dot(ob_ref[...], wob_ref[...], preferred_element_type=F32)
    gates = _sigmoid(gate_ref[...])
    merged = gates[:, :d] * ba + gates[:, d:] * bb
    xn = x_ref[...] + jnp.dot(merged.astype(BF16), wo_ref[...], preferred_element_type=F32)
    y_ref[...] = xn * lax.rsqrt(jnp.mean(xn * xn, axis=-1, keepdims=True) + EPS) * lnf_ref[...]


def _merge(oa, ob, p, x, woa, wob, wo, lnf, *, lay):
    m, d = x.shape
    tm = min(m, 512)
    wg = lay["gate"][1]
    const2 = lambda i: (0, 0)
    return pl.pallas_call(
        _merge_kernel,
        out_shape=jax.ShapeDtypeStruct((m, d), F32),
        grid=(m // tm,),
        in_specs=[
            pl.BlockSpec((tm, oa.shape[1]), lambda i: (i, 0)),
            pl.BlockSpec((tm, ob.shape[1]), lambda i: (i, 0)),
            pl.BlockSpec((tm, wg), lambda i: (i, lay["gate"][0] // wg)),
            pl.BlockSpec((tm, d), lambda i: (i, 0)),
            pl.BlockSpec(woa.shape, const2),
            pl.BlockSpec(wob.shape, const2),
            pl.BlockSpec(wo.shape, const2),
            pl.BlockSpec((1, d), const2),
        ],
        out_specs=pl.BlockSpec((tm, d), lambda i: (i, 0)),
        compiler_params=pltpu.CompilerParams(
            dimension_semantics=("parallel",), vmem_limit_bytes=VMEM_LIMIT),
        name="merge",
    )(oa, ob, p, x, woa, wob, wo, lnf)


def _pad_lanes(v, width):
    return jnp.pad(v, ((0, 0), (0, width - v.shape[1])))


def _token_major_t(ab, c):
    nb, t, k = ab.shape
    return jnp.transpose(ab.reshape(nb, t // c, c, k), (0, 1, 3, 2))


def kernel(x_prompt, x_sample, state_gdn, state_gdn_conv, state_rwkv, state_shift, meta_tokens,
           ln1_w, w_in, gdn_conv_w, gdn_a_log, gdn_dt_bias, gdn_norm_w, w_out_a, rwkv_mu, rwkv_w0,
           rwkv_w2, rwkv_a0, rwkv_a2, rwkv_k_k, rwkv_k_a, rwkv_r_k, rwkv_gn_w, rwkv_gn_b, w_out_b,
           w_out, lnf_w):
    assert ln1_w.shape[0] == 1, "single-layer trunk"
    bp, seq, d = x_prompt.shape
    bs, tseq, _ = x_sample.shape
    n_meta = meta_tokens.shape[0]
    _, _, nh_a, dk, dv = state_gdn.shape
    _, _, nh_b, n_b, _ = state_rwkv.shape
    kw, w_qkv = gdn_conv_w.shape[1:]
    lora_w = rwkv_w2.shape[1]
    lora_a = rwkv_a2.shape[1]
    d_a = nh_a * dv
    d_b = nh_b * n_b
    assert w_qkv == 2 * nh_a * dk + d_a and d_a == d and d_b == d and lora_w + lora_a == LANES
    assert n_meta & (n_meta - 1) == 0 and n_meta >= SUBLANES and kw - 1 <= tseq
    assert seq % PROMPT_CHUNK == 0 and all(bp % k == 0 for k in PROMPT_SEQS)
    assert all(bs % k == 0 for k in SAMPLE_SEQS)

    o_a = w_qkv
    o_b = o_a + nh_a
    o_z = o_b + nh_a
    o_r = o_z + d_a
    o_g = o_r + 3 * d_b + lora_w + lora_a + d_b
    w = w_in[0].T
    wr = w[o_r:o_g]
    mu = rwkv_mu
    lora0 = 3 * d_b
    rows = [w[:w_qkv], w[o_z:o_r], wr[:lora0], wr[lora0 + LANES:], w[o_g:],
            wr[lora0:lora0 + LANES], w[o_a:o_z]]
    used = sum(rw.shape[0] for rw in rows)
    n_pad = -(-(used + LANES - 2 * nh_a) // PROJ_TN) * PROJ_TN
    rows.append(jnp.zeros((n_pad - used, d), F32))
    w_all = jnp.concatenate([rw.astype(BF16) for rw in rows], axis=0)
    lay = {"qkv": (0, w_qkv), "za": (w_qkv, d_a), "rkvz": (w_qkv + d_a, 4 * d_b),
           "gate": (w_qkv + d_a + 4 * d_b, 2 * d)}
    lay["wa"] = (lay["gate"][0] + 2 * d, LANES)
    lay["ab"] = (lay["wa"][0] + LANES, LANES)
    for off, width in lay.values():
        assert off % width == 0
    mu_x = jnp.concatenate([mu[:, :lora0], mu[:, lora0 + LANES:]], axis=1)
    mu_wa = mu[:, lora0:lora0 + LANES]

    alr = _pad_lanes(gdn_a_log, LANES)
    dtr = _pad_lanes(gdn_dt_bias, LANES)
    alc = jnp.pad(gdn_a_log.reshape(nh_a, 1), ((0, nh_a), (0, 0)))
    dtc = jnp.pad(gdn_dt_bias.reshape(nh_a, 1), ((0, nh_a), (0, 0)))
    convw = gdn_conv_w[0]
    rk = rwkv_r_k.reshape(1, d_b)
    woa = w_out_a[0].astype(BF16)
    wob = w_out_b[0].astype(BF16)
    wo = w_out[0].astype(BF16)
    lnf = lnf_w.reshape(1, d)

    def branches(p, c, ns_a, ns_b, t_valid, s_gdn, conv_tail, s_rwkv, x_tail, wa_tail):
        abt = _token_major_t(p[:, :, lay["ab"][0]:lay["ab"][0] + 2 * nh_a], c)
        oa, sg = _recurrence(_gdn_kernel, _gdn_parts(
            p, abt, s_gdn, conv_tail, convw, alr, dtr, alc, dtc, gdn_norm_w,
            c=c, ns=ns_a, t_valid=t_valid, lay=lay), name="gdn")
        ob, sr = _recurrence(_rwkv_kernel, _rwkv_parts(
            p, s_rwkv, x_tail, wa_tail, mu_x, mu_wa, rwkv_w0, rwkv_w2[0], rwkv_a0, rwkv_a2[0],
            rwkv_k_k, rwkv_k_a, rk, rwkv_gn_w, rwkv_gn_b,
            c=c, ns=ns_b, t_valid=t_valid, lay=lay), name="rwkv")
        return oa, sg, ob, sr

    x0, wx = lay["rkvz"]
    a0_, _ = lay["wa"]

    tpad = SUBLANES
    assert n_meta <= bs and tseq < tpad
    rider = jnp.zeros((bs, tpad - tseq, d), F32).at[:n_meta, 0].set(meta_tokens)
    xs = jnp.concatenate([x_sample, rider], axis=1).reshape(bs * tpad, d)
    p_s = _proj(xs, ln1_w, w_all, apply_norm=True)
    p_m = p_s.reshape(bs, tpad, n_pad)[:n_meta, tseq]

    _, sg_m, _, sr_m = branches(
        p_m[None], n_meta, 1, 1, n_meta,
        jnp.zeros((1, nh_a, dk, dv), F32), jnp.zeros((1, 1, w_qkv), F32),
        jnp.zeros((1, nh_b, n_b, n_b), F32), jnp.zeros((1, 1, wx), F32),
        jnp.zeros((1, 1, LANES), F32))

    xp = x_prompt.reshape(bp * seq, d)
    p_p = _proj(xp, ln1_w, w_all, apply_norm=True)
    last_m = p_m[n_meta - 1:, None]
    oa_p, sg_p, ob_p, sr_p = branches(
        p_p.reshape(bp, seq, n_pad), PROMPT_CHUNK, *PROMPT_SEQS, seq, sg_m,
        p_m[n_meta - (kw - 1):, None, :w_qkv], sr_m,
        last_m[:, :, x0:x0 + wx], last_m[:, :, a0_:a0_ + LANES])
    y_p = _merge(oa_p.reshape(bp * seq, d), ob_p.reshape(bp * seq, d), p_p, xp, woa, wob, wo, lnf,
                 lay=lay)

    p_first =_proj(state_shift[0], ln1_w, w_all, apply_norm=False)[None]
    oa_s, sg_s, ob_s, sr_s = branches(
        p_s.reshape(bs, tpad, n_pad), tpad, *SAMPLE_SEQS, tseq, state_gdn[0],
        jnp.transpose(state_gdn_conv[0], (1, 0, 2)), state_rwkv[0],
        p_first[:, :, x0:x0 + wx], p_first[:, :, a0_:a0_ + LANES])
    y_s = _merge(oa_s.reshape(bs * tpad, d), ob_s.reshape(bs * tpad, d), p_s, xs, woa, wob, wo,
                 lnf, lay=lay)

    shift_p = _rmsnorm_rows(x_prompt[:, -1], ln1_w)
    shift_s = _rmsnorm_rows(x_sample[:, -1], ln1_w)
    conv_p = p_p.reshape(bp, seq, n_pad)[:, seq - (kw - 1):, :w_qkv]
    conv_s = p_s.reshape(bs, tpad, n_pad)[:, tseq - (kw - 1):tseq, :w_qkv]
    return (y_p.reshape(bp, seq, d), y_s.reshape(bs, tpad, d)[:, :tseq],
            sg_p[None], conv_p[None], sr_p[None], shift_p[None],
            sg_s[None], conv_s[None], sr_s[None], shift_s[None])
```

```python
import functools

import jax
import jax.numpy as jnp
from jax import lax
from jax.experimental import pallas as pl
from jax.experimental.pallas import tpu as pltpu

F32 = jnp.float32
BF16 = jnp.bfloat16

EPS = 1e-6
L2_EPS = 1e-6
GN_EPS_PER_CH = 1e-5

LANES = 128
SUBLANES = 8
VMEM_LIMIT = 32 * 1024 * 1024
PROJ_TM = 1024
PROJ_TN = 1536
PROMPT_CHUNK = 64
PROMPT_SEQS = (4, 4)
SAMPLE_SEQS = (8, 8)

NN = (((1,), (0,)), ((), ()))
NT = (((1,), (1,)), ((), ()))
TN = (((0,), (0,)), ((), ()))


def _mm(a, b, dims=NN):
    return lax.dot_general(a.astype(BF16), b.astype(BF16), dims, preferred_element_type=F32)


def _dotb(a, b, dims=NN):
    return lax.dot_general(a, b, dims, preferred_element_type=F32)


def _split3(x):
    hi = x.astype(BF16)
    rem = x - hi.astype(F32)
    mid = rem.astype(BF16)
    return hi, mid, (rem - mid.astype(F32)).astype(BF16)


def _cumsum_rows(tril01, x):
    hi, mid, lo = _split3(x)
    return _dotb(tril01, hi) + _dotb(tril01, mid) + _dotb(tril01, lo)


def _cumsum_lanes(x, triu01):
    hi, mid, lo = _split3(x)
    return _dotb(hi, triu01) + _dotb(mid, triu01) + _dotb(lo, triu01)


def _sigmoid(x):
    return 1.0 / (1.0 + jnp.exp(-x))


def _silu(x):
    return x * _sigmoid(x)


def _softplus(x):
    return jnp.maximum(x, 0.0) + jnp.log(1.0 + jnp.exp(-jnp.abs(x)))


def _shift_rows(cur, prev, k):
    rolled = pltpu.roll(cur, k, 0)
    fix = pltpu.roll(prev, k, 0)
    wrap = lax.broadcasted_iota(jnp.int32, fix.shape, 0) < k
    head = jnp.where(wrap, fix, rolled[0:SUBLANES])
    if cur.shape[0] == SUBLANES:
        return head
    return jnp.concatenate([head, rolled[SUBLANES:]], axis=0)


class _PairMasks:
    def __init__(self, c):
        row = lax.broadcasted_iota(jnp.int32, (c, 2 * c), 0)
        lane = lax.broadcasted_iota(jnp.int32, (c, 2 * c), 1)
        col = lane & (c - 1)
        self.tril = row >= col
        self.strict = row > col
        diff = row ^ col
        self.levels = [(diff >> 1) == 0]
        sh = 2
        while (1 << sh) <= c:
            self.levels.append((diff >> (sh - 1)) == 1)
            sh += 1
        self.head0 = lane < c
        self.tok0 = self.head0.astype(F32).astype(BF16)
        self.tok1 = (lane >= c).astype(F32).astype(BF16)

    def block_diag(self, xb):
        return jnp.concatenate([xb * self.tok0, xb * self.tok1], axis=0)


def _tri_inv_pairs(lows, pm):
    nn = [-jnp.where(pm.levels[0], x, 0.0) for x in lows]
    for m in pm.levels[1:]:
        off = [jnp.where(m, x, 0.0) for x in lows]
        nnb = [x.astype(BF16) for x in nn]
        xc = [o + _dotb(xb, pm.block_diag(o.astype(BF16))) for o, xb in zip(off, nnb)]
        nn = [x - (y + _dotb(y.astype(BF16), pm.block_diag(xb)))
              for x, y, xb in zip(nn, xc, nnb)]
    return nn


def _proj_kernel(x_ref, lnw_ref, w_ref, o_ref, h_ref, *, apply_norm):
    @pl.when(pl.program_id(1) == 0)
    def _():
        x = x_ref[...]
        if apply_norm:
            x = x * lax.rsqrt(jnp.mean(x * x, axis=-1, keepdims=True) + EPS)
            x = x * lnw_ref[...]
        h_ref[...] = x.astype(BF16)

    o_ref[...] = lax.dot_general(h_ref[...], w_ref[...], NT, preferred_element_type=F32)


def _proj(x, lnw, w, *, apply_norm):
    m, d = x.shape
    n = w.shape[0]
    tm = min(m, PROJ_TM)
    tn = PROJ_TN
    assert m % tm == 0 and n % tn == 0
    vmem = 2 * (tm * d * 4 + tn * d * 2 + tm * tn * 4) + tm * d * 2 + (4 << 20)
    return pl.pallas_call(
        functools.partial(_proj_kernel, apply_norm=apply_norm),
        out_shape=jax.ShapeDtypeStruct((m, n), F32),
        grid=(m // tm, n // tn),
        in_specs=[pl.BlockSpec((tm, d), lambda i, j: (i, 0)),
                  pl.BlockSpec((1, d), lambda i, j: (0, 0)),
                  pl.BlockSpec((tn, d), lambda i, j: (j, 0))],
        out_specs=pl.BlockSpec((tm, tn), lambda i, j: (i, j)),
        scratch_shapes=[pltpu.VMEM((tm, d), BF16)],
        compiler_params=pltpu.CompilerParams(
            dimension_semantics=("parallel", "arbitrary"), vmem_limit_bytes=vmem),
        name="proj",
    )(x, lnw, w)


def _rmsnorm_kernel(x_ref, w_ref, o_ref):
    x = x_ref[...]
    o_ref[...] = x * lax.rsqrt(jnp.mean(x * x, axis=-1, keepdims=True) + EPS) * w_ref[...]


def _rmsnorm_rows(x, w):
    return pl.pallas_call(
        _rmsnorm_kernel,
        out_shape=jax.ShapeDtypeStruct(x.shape, F32),
        name="rmsnorm_rows",
    )(x, w)


def _gdn_kernel(qkv_ref, z_ref, ab_ref, abt_ref, s0_ref, prev0_ref, convw_ref,
                alr_ref, dtr_ref, alc_ref, dtc_ref, nw_ref,
                o_ref, s_ref, prev_ref, *, c, ns, nh, dk, dv, t_valid, t_total):
    ci = pl.program_id(1)
    shared = s0_ref.shape[0] == 1

    @pl.when(ci == 0)
    def _():
        for j in range(ns):
            s_ref[j] = s0_ref[0 if shared else j]
            _init_carry(prev_ref, prev0_ref, j, 0 if shared else j)

    pm = _PairMasks(c)
    row = lax.broadcasted_iota(jnp.int32, (c, c), 0)
    col = lax.broadcasted_iota(jnp.int32, (c, c), 1)
    tril = row >= col
    kw = convw_ref.shape[0]
    seqs = range(ns)

    ab = [ab_ref[j] for j in seqs]
    g_c = [-jnp.exp(alr_ref[...]) * _softplus(x + dtr_ref[...]) for x in ab]
    beta_c = [_sigmoid(x) for x in ab]
    abt = [abt_ref[j, 0] for j in seqs]
    g_r = [-jnp.exp(alc_ref[...]) * _softplus(x + dtc_ref[...]) for x in abt]
    if t_valid < t_total:
        tok_c = ci * c + lax.broadcasted_iota(jnp.int32, ab[0].shape, 0)
        tok_r = ci * c + lax.broadcasted_iota(jnp.int32, abt[0].shape, 1)
        g_c = [jnp.where(tok_c < t_valid, x, 0.0) for x in g_c]
        beta_c = [jnp.where(tok_c < t_valid, x, 0.0) for x in beta_c]
        g_r = [jnp.where(tok_r < t_valid, x, 0.0) for x in g_r]
    gc_c = _cumsum_rows(tril.astype(BF16), jnp.concatenate(g_c, axis=1))
    gc_r = _cumsum_lanes(jnp.concatenate(g_r, axis=0), (row <= col).astype(BF16))

    def conv(j, col0, width):
        cur = qkv_ref[j, :, col0:col0 + width]
        prev = prev_ref[j, :, col0:col0 + width]
        acc = None
        for k in range(kw - 1, -1, -1):
            src = cur if k == 0 else _shift_rows(cur, prev, k)
            term = src * convw_ref[kw - 1 - k:kw - k, col0:col0 + width]
            acc = term if acc is None else acc + term
        return _silu(acc)

    units = [(j, h) for j in seqs for h in range(nh)]
    us = range(len(units))
    gcc = [gc_c[:, j * LANES + h:j * LANES + h + 1] for j, h in units]
    gcr = [gc_r[j * 2 * nh + h:j * 2 * nh + h + 1, :] for j, h in units]
    beta = [beta_c[j][:, nh + h:nh + h + 1] for j, h in units]
    glast = [x[c - 1:c, :] for x in gcc]
    q = [conv(j, h * dk, dk) for j, h in units]
    k = [conv(j, nh * dk + h * dk, dk) for j, h in units]
    v = [conv(j, 2 * nh * dk + h * dv, dv) for j, h in units]
    q = [x * lax.rsqrt(jnp.sum(x * x, axis=-1, keepdims=True) + L2_EPS) * (dk ** -0.5) for x in q]
    k = [x * lax.rsqrt(jnp.sum(x * x, axis=-1, keepdims=True) + L2_EPS) for x in k]
    kb = [k[u] * beta[u] for u in us]
    vb = [v[u] * beta[u] for u in us]
    kbg = [kb[u] * jnp.exp(gcc[u]) for u in us]
    qg = [q[u] * jnp.exp(gcc[u]) for u in us]
    k_tail = [k[u] * jnp.exp(glast[u] - gcc[u]) for u in us]

    pairs = [(u, u + 1) for u in us if u % 2 == 0]

    def side_by_side(xs):
        return [jnp.concatenate([xs[a].astype(BF16), xs[b].astype(BF16)], axis=1) for a, b in pairs]

    def per_head_rows(xs):
        out = []
        for a, b in pairs:
            xa, xb = xs[a].astype(BF16), xs[b].astype(BF16)
            out.append(jnp.concatenate([jnp.concatenate([xa, jnp.zeros_like(xb)], axis=1),
                                        jnp.concatenate([jnp.zeros_like(xa), xb], axis=1)], axis=0))
        return out

    def split(xs, width):
        return [x[:, i * width:(i + 1) * width] for x in xs for i in range(2)]

    gcc_s = [jnp.where(pm.head0, gcc[a], gcc[b]) for a, b in pairs]
    gcr_s = [jnp.concatenate([gcr[a], gcr[b]], axis=1) for a, b in pairs]
    decay = [jnp.where(pm.tril, jnp.exp(jnp.where(pm.tril, x - y, 0.0)), 0.0)
             for x, y in zip(gcc_s, gcr_s)]
    k_rows = per_head_rows(k)
    lhs = [jnp.concatenate([x, y], axis=0) for x, y in zip(side_by_side(kb), side_by_side(q))]
    both = [_dotb(x, y, NT) for x, y in zip(lhs, k_rows)]
    lower = [jnp.where(pm.strict, x[0:c] * d_, 0.0) for x, d_ in zip(both, decay)]
    qk = [jnp.where(pm.tril, x[c:2 * c] * d_, 0.0) for x, d_ in zip(both, decay)]
    s_old = [s_ref[j, h] for j, h in units]
    qs = [_mm(qg[u], s_old[u]) for u in us]
    nn = [x.astype(BF16) for x in _tri_inv_pairs(lower, pm)]
    u_ = [vb[u] + x for u, x in enumerate(split(
        [_dotb(x, y) for x, y in zip(nn, per_head_rows(vb))], dv))]
    w_ = [kbg[u] + x for u, x in enumerate(split(
        [_dotb(x, y) for x, y in zip(nn, per_head_rows(kbg))], dk))]
    v_new = [u_[u] - _mm(w_[u], s_old[u]) for u in us]
    kv = [_mm(k_tail[u], v_new[u], TN) for u in us]
    for u, (j, h) in enumerate(units):
        s_ref[j, h] = s_old[u] * jnp.exp(glast[u]) + kv[u]
    o = [qs[u] + x for u, x in enumerate(split(
        [_dotb(x.astype(BF16), y) for x, y in zip(qk, per_head_rows(v_new))], dv))]
    for u, (j, h) in enumerate(units):
        oh = o[u] * lax.rsqrt(jnp.mean(o[u] * o[u], axis=-1, keepdims=True) + EPS) * nw_ref[...]
        oh = oh * _silu(z_ref[j, :, h * dv:(h + 1) * dv])
        o_ref[j, :, h * dv:(h + 1) * dv] = oh.astype(BF16)

    for j in seqs:
        prev_ref[j] = qkv_ref[j, c - SUBLANES:c, :]


def _state_specs(s0, ns, tails):
    shared = s0.shape[0] == 1
    lead = 1 if shared else ns
    idx4 = (lambda b, i: (0, 0, 0, 0)) if shared else (lambda b, i: (b, 0, 0, 0))
    idx3 = (lambda b, i: (0, 0, 0)) if shared else (lambda b, i: (0, b, 0))
    return ([pl.BlockSpec((lead,) + s0.shape[1:], idx4)]
            + [pl.BlockSpec((tl.shape[0], lead, tl.shape[2]), idx3) for tl in tails])


def _init_carry(prev_ref, tail_ref, j, jj):
    k = tail_ref.shape[0]
    if k < SUBLANES:
        prev_ref[j] = jnp.zeros(prev_ref.shape[1:], prev_ref.dtype)
    for r in range(k):
        row = SUBLANES - k + r
        prev_ref[j, row:row + 1, :] = tail_ref[r, jj:jj + 1, :]


def _gdn_parts(p, abt, s0, prev0, convw, alr, dtr, alc, dtc, nw, *, c, ns, t_valid, lay):
    nb, t, _ = p.shape
    nh, dk, dv = s0.shape[1:]
    wq = lay["qkv"][1]
    const2 = lambda b, i: (0, 0)
    return dict(
        operands=[p, p, p, abt, s0, prev0, convw, alr, dtr, alc, dtc, nw],
        in_specs=[
            pl.BlockSpec((ns, c, wq), lambda b, i: (b, i, lay["qkv"][0] // wq)),
            pl.BlockSpec((ns, c, nh * dv), lambda b, i: (b, i, lay["za"][0] // (nh * dv))),
            pl.BlockSpec((ns, c, LANES), lambda b, i: (b, i, lay["ab"][0] // LANES)),
            pl.BlockSpec((ns, 1, 2 * nh, c), lambda b, i: (b, i, 0, 0)),
            *_state_specs(s0, ns, [prev0]),
            pl.BlockSpec(convw.shape, const2),
            pl.BlockSpec(alr.shape, const2),
            pl.BlockSpec(dtr.shape, const2),
            pl.BlockSpec(alc.shape, const2),
            pl.BlockSpec(dtc.shape, const2),
            pl.BlockSpec(nw.shape, const2),
        ],
        out_shape=[jax.ShapeDtypeStruct((nb, t, nh * dv), BF16),
                   jax.ShapeDtypeStruct((nb, nh, dk, dv), F32)],
        out_specs=[pl.BlockSpec((ns, c, nh * dv), lambda b, i: (b, i, 0)),
                   pl.BlockSpec((ns, nh, dk, dv), lambda b, i: (b, 0, 0, 0))],
        scratch=[pltpu.VMEM((ns, SUBLANES, wq), F32)],
        statics=dict(c=c, ns=ns, nh=nh, dk=dk, dv=dv, t_valid=t_valid, t_total=t))


def _token_shift(x_ref, prev_ref, mu_ref, j, lo, hi):
    cur = x_ref[j, :, lo:hi]
    prv = _shift_rows(cur, prev_ref[j, :, lo:hi], 1)
    return cur + mu_ref[:, lo:hi] * (prv - cur)


_RWKV_OPERANDS = ("kkg", "rg", "bh", "kh", "kt", "bt", "v")


def _rwkv_kernel(x_ref, wa_ref, s0_ref, prev0_ref, prevwa0_ref, mu_ref, muwa_ref,
                 w0_ref, w2_ref, a0_ref, a2_ref, kk_ref, ka_ref, rk_ref, gnw_ref, gnb_ref,
                 o_ref, s_ref, prev_ref, prevwa_ref, sbd_ref,
                 *, c, ns, nh, n, t_valid, t_total):
    step = pl.program_id(1)
    d = nh * n
    npair = nh // 2
    ngrp = d // LANES
    shared = s0_ref.shape[0] == 1
    seqs = range(ns)

    @pl.when(step == 0)
    def _():
        zero = jnp.zeros((n, n), F32)
        for j in seqs:
            jj = 0 if shared else j
            for p in range(npair):
                top = jnp.concatenate([s0_ref[jj, 2 * p], zero], axis=1)
                bot = jnp.concatenate([zero, s0_ref[jj, 2 * p + 1]], axis=1)
                sbd_ref[j, p] = jnp.concatenate([top, bot], axis=0)
            _init_carry(prev_ref, prev0_ref, j, jj)
            _init_carry(prevwa_ref, prevwa0_ref, j, jj)

    w2 = 2 * c
    pm = _PairMasks(c)
    tril, strict = pm.tril, pm.strict
    chl = lax.broadcasted_iota(jnp.int32, (c, LANES), 1)
    ch0 = (chl < n).astype(F32).astype(BF16)
    ch1 = (chl >= n).astype(F32).astype(BF16)
    row2 = lax.broadcasted_iota(jnp.int32, (LANES, LANES), 0)
    lane2 = lax.broadcasted_iota(jnp.int32, (LANES, LANES), 1)
    log2n = n.bit_length() - 1
    same_head = (row2 >> log2n) == (lane2 >> log2n)
    seg_ones = same_head.astype(F32).astype(BF16)

    def bd_ch(xb):
        return jnp.concatenate([xb * ch0, xb * ch1], axis=0)

    dot = _dotb

    def seg_sum(x):
        stk = jnp.concatenate([x[:, i * LANES:(i + 1) * LANES] for i in range(ngrp)], axis=0)
        acc = dot(stk.astype(BF16), seg_ones)
        return jnp.concatenate([acc[i * c:(i + 1) * c] for i in range(ngrp)], axis=1)

    lora = w2_ref.shape[0]
    row_c = lax.broadcasted_iota(jnp.int32, (c, c), 0)
    col_c = lax.broadcasted_iota(jnp.int32, (c, c), 1)
    tril_b = (row_c >= col_c).astype(BF16)

    def pre_steps(out):
        r = [_token_shift(x_ref, prev_ref, mu_ref, j, 0, d) for j in seqs]
        kb = [_token_shift(x_ref, prev_ref, mu_ref, j, d, 2 * d) for j in seqs]
        vb = [_token_shift(x_ref, prev_ref, mu_ref, j, 2 * d, 3 * d) for j in seqs]
        sz = [_silu(_token_shift(x_ref, prev_ref, mu_ref, j, 3 * d, 4 * d)) for j in seqs]
        wa = [wa_ref[j] for j in seqs]
        wa = [wa[j] + muwa_ref[...] * (_shift_rows(wa[j], prevwa_ref[j], 1) - wa[j])
              for j in seqs]
        tw = [jnp.tanh(x[:, 0:lora]) for x in wa]
        w_raw = [w0_ref[...] + _mm(tw[j], w2_ref[...]) for j in seqs]
        a = [_sigmoid(a0_ref[...] + _mm(wa[j][:, lora:], a2_ref[...])) for j in seqs]
        w_log = [-jnp.exp(-_softplus(-x) - 0.5) for x in w_raw]
        kku = [kb[j] * kk_ref[...] for j in seqs]
        k2 = [kb[j] * (1.0 + (a[j] - 1.0) * ka_ref[...]) for j in seqs]
        if t_valid < t_total:
            ok = (step * c + lax.broadcasted_iota(jnp.int32, (c, d), 0)) < t_valid
            w_log = [jnp.where(ok, x, 0.0) for x in w_log]
            kku = [jnp.where(ok, x, 0.0) for x in kku]
            vb = [jnp.where(ok, x, 0.0) for x in vb]
        gcum = [_cumsum_rows(tril_b, x) for x in w_log]
        glast = [x[c - 1:c, :] for x in gcum]
        g_in = [jnp.exp(x) for x in gcum]
        g_ex = [jnp.exp(gcum[j] - w_log[j]) for j in seqs]
        g_inv = [jnp.exp(-x) for x in gcum]
        g_tail = [jnp.exp(glast[j] - gcum[j]) for j in seqs]
        ssq = [x * x for x in kku]
        bonus_in = [r[j] * k2[j] * rk_ref[...] for j in seqs]
        kk = [kku[j] * lax.rsqrt(seg_sum(ssq[j]) + L2_EPS) for j in seqs]
        bonus = [seg_sum(bonus_in[j]) * vb[j] for j in seqs]
        for j in seqs:
            b = kk[j] * a[j]
            out[j].update(
                kkg=(kk[j] * g_ex[j]).astype(BF16), rg=(r[j] * g_in[j]).astype(BF16),
                bh=(b * g_inv[j]).astype(BF16), kh=(k2[j] * g_inv[j]).astype(BF16),
                kt=(k2[j] * g_tail[j]).astype(BF16), bt=(b * g_tail[j]).astype(BF16),
                v=vb[j].astype(BF16), bonus=bonus[j], sz=sz[j], g_all=jnp.exp(glast[j]))

    units = [(j, p) for j in seqs for p in range(npair)]
    us = range(len(units))

    def stage_steps(ops):
        kkg, rg, bh, kh, kt, bt, v = (
            [ops[j][k][:, p * LANES:(p + 1) * LANES] for j, p in units] for k in _RWKV_OPERANDS)
        g_all = [ops[j]["g_all"][:, p * LANES:(p + 1) * LANES] for j, p in units]
        lhs = [jnp.concatenate([kkg[u], rg[u]], axis=0) for u in us]
        rhs = [jnp.concatenate([bd_ch(bh[u]), bd_ch(kh[u])], axis=0) for u in us]
        big = [dot(lhs[u], rhs[u], NT) for u in us]
        s_old = [sbd_ref[j, p] for j, p in units]
        from_s = [dot(lhs[u], s_old[u].astype(BF16), NT) for u in us]
        low = [jnp.where(strict, x[0:c, 0:w2], 0.0) for x in big]
        m_kv = [jnp.where(strict, x[0:c, w2:2 * w2], 0.0) for x in big]
        q_b = [jnp.where(tril, x[c:2 * c, 0:w2], 0.0) for x in big]
        p_kv = [jnp.where(tril, x[c:2 * c, w2:2 * w2], 0.0) for x in big]
        mp = [dot(jnp.concatenate([m_kv[u], p_kv[u]], axis=0).astype(BF16), bd_ch(v[u]))
              for u in us]
        nn = _tri_inv_pairs(low, pm)
        rhs_sa = [from_s[u][0:c] + mp[u][0:c] for u in us]
        sa = [rhs_sa[u] + dot(nn[u].astype(BF16), bd_ch(rhs_sa[u].astype(BF16))) for u in us]
        sab = [x.astype(BF16) for x in sa]
        qs = [dot(q_b[u].astype(BF16), bd_ch(sab[u])) for u in us]
        upd = [dot(jnp.concatenate([v[u], -sab[u]], axis=0),
                   jnp.concatenate([kt[u], bt[u]], axis=0), TN) for u in us]
        s_new = [s_old[u] * g_all[u] + jnp.where(same_head, upd[u], 0.0) for u in us]
        for u, (j, p) in enumerate(units):
            sbd_ref[j, p] = s_new[u]

        @pl.when(step == pl.num_programs(1) - 1)
        def _():
            for u, (j, p) in enumerate(units):
                s_ref[j, 2 * p] = s_new[u][0:n, 0:n]
                s_ref[j, 2 * p + 1] = s_new[u][n:2 * n, n:2 * n]

        y = [from_s[u][c:2 * c] + mp[u][c:2 * c] - qs[u] for u in us]
        yj = [jnp.concatenate(y[j * npair:(j + 1) * npair], axis=1) for j in seqs]
        yc = [yj[j] - seg_sum(yj[j]) * (1.0 / n) for j in seqs]
        var = [seg_sum(yc[j] * yc[j]) * (1.0 / n) for j in seqs]
        for j in seqs:
            yn = yc[j] * lax.rsqrt(var[j] + n * GN_EPS_PER_CH) * gnw_ref[...] + gnb_ref[...]
            o_ref[j] = ((yn + ops[j]["bonus"]) * ops[j]["sz"]).astype(BF16)

    ops = [dict() for _ in seqs]
    pre_steps(ops)
    stage_steps(ops)

    for j in seqs:
        prev_ref[j] = x_ref[j, c - SUBLANES:c, :]
        prevwa_ref[j] = wa_ref[j, c - SUBLANES:c, :]


def _rwkv_parts(p, s0, prev0, prevwa0, mu, muwa, w0, w2, a0, a2, kk, ka, rk, gnw, gnb,
                *, c, ns, t_valid, lay):
    nb, t, _ = p.shape
    nh, n = s0.shape[1:3]
    d = nh * n
    wx = lay["rkvz"][1]
    const2 = lambda b, i: (0, 0)
    vec = pl.BlockSpec((1, d), const2)
    assert 2 * n == LANES and c <= n and c & (c - 1) == 0, "two heads per lane group"
    return dict(
        operands=[p, p, s0, prev0, prevwa0, mu, muwa, w0, w2, a0, a2, kk, ka, rk, gnw, gnb],
        in_specs=[
            pl.BlockSpec((ns, c, wx), lambda b, i: (b, i, lay["rkvz"][0] // wx)),
            pl.BlockSpec((ns, c, LANES), lambda b, i: (b, i, lay["wa"][0] // LANES)),
            *_state_specs(s0, ns, [prev0, prevwa0]),
            pl.BlockSpec((1, wx), const2),
            pl.BlockSpec((1, LANES), const2),
            vec,
            pl.BlockSpec(w2.shape, const2),
            vec,
            pl.BlockSpec(a2.shape, const2),
            vec, vec, vec, vec, vec,
        ],
        out_shape=[jax.ShapeDtypeStruct((nb, t, d), BF16),
                   jax.ShapeDtypeStruct((nb, nh, n, n), F32)],
        out_specs=[pl.BlockSpec((ns, c, d), lambda b, i: (b, i, 0)),
                   pl.BlockSpec((ns, nh, n, n), lambda b, i: (b, 0, 0, 0))],
        scratch=[pltpu.VMEM((ns, SUBLANES, wx), F32), pltpu.VMEM((ns, SUBLANES, LANES), F32),
                 pltpu.VMEM((ns, nh // 2, LANES, LANES), F32)],
        statics=dict(c=c, ns=ns, nh=nh, n=n, t_valid=t_valid, t_total=t))


def _recurrence(body, parts, *, name):
    statics = parts["statics"]
    nb, t = parts["out_shape"][0].shape[:2]
    return pl.pallas_call(
        functools.partial(body, **statics),
        out_shape=tuple(parts["out_shape"]),
        grid=(nb // statics["ns"], t // statics["c"]),
        in_specs=parts["in_specs"],
        out_specs=tuple(parts["out_specs"]),
        scratch_shapes=parts["scratch"],
        compiler_params=pltpu.CompilerParams(
            dimension_semantics=("parallel", "arbitrary"), vmem_limit_bytes=VMEM_LIMIT),
        name=name,
    )(*parts["operands"])


def _merge_kernel(oa_ref, ob_ref, gate_ref, x_ref, woa_ref, wob_ref, wo_ref, lnf_ref, y_ref):
    d = x_ref.shape[1]
    ba = jnp.dot(oa_ref[...], woa_ref[...], preferred_element_type=F32)
    bb = jnp.dot(ob_ref[...], wob_ref[...], preferred_element_type=F32)
    gates = _sigmoid(gate_ref[...])
    merged = gates[:, :d] * ba + gates[:, d:] * bb
    xn = x_ref[...] + jnp.dot(merged.astype(BF16), wo_ref[...], preferred_element_type=F32)
    y_ref[...] = xn * lax.rsqrt(jnp.mean(xn * xn, axis=-1, keepdims=True) + EPS) * lnf_ref[...]


def _merge(oa, ob, p, x, woa, wob, wo, lnf, *, lay):
    m, d = x.shape
    tm = min(m, 512)
    wg = lay["gate"][1]
    const2 = lambda i: (0, 0)
    return pl.pallas_call(
        _merge_kernel,
        out_shape=jax.ShapeDtypeStruct((m, d), F32),
        grid=(m // tm,),
        in_specs=[
            pl.BlockSpec((tm, oa.shape[1]), lambda i: (i, 0)),
            pl.BlockSpec((tm, ob.shape[1]), lambda i: (i, 0)),
            pl.BlockSpec((tm, wg), lambda i: (i, lay["gate"][0] // wg)),
            pl.BlockSpec((tm, d), lambda i: (i, 0)),
            pl.BlockSpec(woa.shape, const2),
            pl.BlockSpec(wob.shape, const2),
            pl.BlockSpec(wo.shape, const2),
            pl.BlockSpec((1, d), const2),
        ],
        out_specs=pl.BlockSpec((tm, d), lambda i: (i, 0)),
        compiler_params=pltpu.CompilerParams(
            dimension_semantics=("parallel",), vmem_limit_bytes=VMEM_LIMIT),
        name="merge",
    )(oa, ob, p, x, woa, wob, wo, lnf)


def _pad_lanes(v, width):
    return jnp.pad(v, ((0, 0), (0, width - v.shape[1])))


def _token_major_t(ab, c):
    nb, t, k = ab.shape
    return jnp.transpose(ab.reshape(nb, t // c, c, k), (0, 1, 3, 2))


def kernel(x_prompt, x_sample, state_gdn, state_gdn_conv, state_rwkv, state_shift, meta_tokens,
           ln1_w, w_in, gdn_conv_w, gdn_a_log, gdn_dt_bias, gdn_norm_w, w_out_a, rwkv_mu, rwkv_w0,
           rwkv_w2, rwkv_a0, rwkv_a2, rwkv_k_k, rwkv_k_a, rwkv_r_k, rwkv_gn_w, rwkv_gn_b, w_out_b,
           w_out, lnf_w):
    assert ln1_w.shape[0] == 1, "single-layer trunk"
    bp, seq, d = x_prompt.shape
    bs, tseq, _ = x_sample.shape
    n_meta = meta_tokens.shape[0]
    _, _, nh_a, dk, dv = state_gdn.shape
    _, _, nh_b, n_b, _ = state_rwkv.shape
    kw, w_qkv = gdn_conv_w.shape[1:]
    lora_w = rwkv_w2.shape[1]
    lora_a = rwkv_a2.shape[1]
    d_a = nh_a * dv
    d_b = nh_b * n_b
    assert w_qkv == 2 * nh_a * dk + d_a and d_a == d and d_b == d and lora_w + lora_a == LANES
    assert n_meta & (n_meta - 1) == 0 and n_meta >= SUBLANES and kw - 1 <= tseq
    assert seq % PROMPT_CHUNK == 0 and all(bp % k == 0 for k in PROMPT_SEQS)
    assert all(bs % k == 0 for k in SAMPLE_SEQS)

    o_a = w_qkv
    o_b = o_a + nh_a
    o_z = o_b + nh_a
    o_r = o_z + d_a
    o_g = o_r + 3 * d_b + lora_w + lora_a + d_b
    w = w_in[0].T
    wr = w[o_r:o_g]
    mu = rwkv_mu
    lora0 = 3 * d_b
    rows = [w[:w_qkv], w[o_z:o_r], wr[:lora0], wr[lora0 + LANES:], w[o_g:],
            wr[lora0:lora0 + LANES], w[o_a:o_z]]
    used = sum(rw.shape[0] for rw in rows)
    n_pad = -(-(used + LANES - 2 * nh_a) // PROJ_TN) * PROJ_TN
    rows.append(jnp.zeros((n_pad - used, d), F32))
    w_all = jnp.concatenate([rw.astype(BF16) for rw in rows], axis=0)
    lay = {"qkv": (0, w_qkv), "za": (w_qkv, d_a), "rkvz": (w_qkv + d_a, 4 * d_b),
           "gate": (w_qkv + d_a + 4 * d_b, 2 * d)}
    lay["wa"] = (lay["gate"][0] + 2 * d, LANES)
    lay["ab"] = (lay["wa"][0] + LANES, LANES)
    for off, width in lay.values():
        assert off % width == 0
    mu_x = jnp.concatenate([mu[:, :lora0], mu[:, lora0 + LANES:]], axis=1)
    mu_wa = mu[:, lora0:lora0 + LANES]

    alr = _pad_lanes(gdn_a_log, LANES)
    dtr = _pad_lanes(gdn_dt_bias, LANES)
    alc = jnp.pad(gdn_a_log.reshape(nh_a, 1), ((0, nh_a), (0, 0)))
    dtc = jnp.pad(gdn_dt_bias.reshape(nh_a, 1), ((0, nh_a), (0, 0)))
    convw = gdn_conv_w[0]
    rk = rwkv_r_k.reshape(1, d_b)
    woa = w_out_a[0].astype(BF16)
    wob = w_out_b[0].astype(BF16)
    wo = w_out[0].astype(BF16)
    lnf = lnf_w.reshape(1, d)

    def branches(p, c, ns_a, ns_b, t_valid, s_gdn, conv_tail, s_rwkv, x_tail, wa_tail):
        abt = _token_major_t(p[:, :, lay["ab"][0]:lay["ab"][0] + 2 * nh_a], c)
        oa, sg = _recurrence(_gdn_kernel, _gdn_parts(
            p, abt, s_gdn, conv_tail, convw, alr, dtr, alc, dtc, gdn_norm_w,
            c=c, ns=ns_a, t_valid=t_valid, lay=lay), name="gdn")
        ob, sr = _recurrence(_rwkv_kernel, _rwkv_parts(
            p, s_rwkv, x_tail, wa_tail, mu_x, mu_wa, rwkv_w0, rwkv_w2[0], rwkv_a0, rwkv_a2[0],
            rwkv_k_k, rwkv_k_a, rk, rwkv_gn_w, rwkv_gn_b,
            c=c, ns=ns_b, t_valid=t_valid, lay=lay), name="rwkv")
        return oa, sg, ob, sr

    x0, wx = lay["rkvz"]
    a0_, _ = lay["wa"]

    tpad = SUBLANES
    assert n_meta <= bs and tseq < tpad
    rider = jnp.zeros((bs, tpad - tseq, d), F32).at[:n_meta, 0].set(meta_tokens)
    xs = jnp.concatenate([x_sample, rider], axis=1).reshape(bs * tpad, d)
    p_s = _proj(xs, ln1_w, w_all, apply_norm=True)
    p_m = p_s.reshape(bs, tpad, n_pad)[:n_meta, tseq]

    _, sg_m, _, sr_m = branches(
        p_m[None], n_meta, 1, 1, n_meta,
        jnp.zeros((1, nh_a, dk, dv), F32), jnp.zeros((1, 1, w_qkv), F32),
        jnp.zeros((1, nh_b, n_b, n_b), F32), jnp.zeros((1, 1, wx), F32),
        jnp.zeros((1, 1, LANES), F32))

    xp = x_prompt.reshape(bp * seq, d)
    p_p = _proj(xp, ln1_w, w_all, apply_norm=True)
    last_m = p_m[n_meta - 1:, None]
    oa_p, sg_p, ob_p, sr_p = branches(
        p_p.reshape(bp, seq, n_pad), PROMPT_CHUNK, *PROMPT_SEQS, seq, sg_m,
        p_m[n_meta - (kw - 1):, None, :w_qkv], sr_m,
        last_m[:, :, x0:x0 + wx], last_m[:, :, a0_:a0_ + LANES])
    y_p = _merge(oa_p.reshape(bp * seq, d), ob_p.reshape(bp * seq, d), p_p, xp, woa, wob, wo, lnf,
                 lay=lay)

    p_first =_proj(state_shift[0], ln1_w, w_all, apply_norm=False)[None]
    oa_s, sg_s, ob_s, sr_s = branches(
        p_s.reshape(bs, tpad, n_pad), tpad, *SAMPLE_SEQS, tseq, state_gdn[0],
        jnp.transpose(state_gdn_conv[0], (1, 0, 2)), state_rwkv[0],
        p_first[:, :, x0:x0 + wx], p_first[:, :, a0_:a0_ + LANES])
    y_s = _merge(oa_s.reshape(bs * tpad, d), ob_s.reshape(bs * tpad, d), p_s, xs, woa, wob, wo,
                 lnf, lay=lay)

    shift_p = _rmsnorm_rows(x_prompt[:, -1], ln1_w)
    shift_s = _rmsnorm_rows(x_sample[:, -1], ln1_w)
    conv_p = p_p.reshape(bp, seq, n_pad)[:, seq - (kw - 1):, :w_qkv]
    conv_s = p_s.reshape(bs, tpad, n_pad)[:, tseq - (kw - 1):tseq, :w_qkv]
    return (y_p.reshape(bp, seq, d), y_s.reshape(bs, tpad, d)[:, :tseq],
            sg_p[None], conv_p[None], sr_p[None], shift_p[None],
            sg_s[None], conv_s[None], sr_s[None], shift_s[None])
```

```python
import functools

import jax
import jax.numpy as jnp
from jax import lax
from jax.experimental import pallas as pl
from jax.experimental.pallas import tpu as pltpu

F32 = jnp.float32
BF16 = jnp.bfloat16

EPS = 1e-6
L2_EPS = 1e-6
GN_EPS_PER_CH = 1e-5

LANES = 128
SUBLANES = 8
VMEM_LIMIT = 32 * 1024 * 1024
PROJ_TM = 1024
PROJ_TN = 1536
PROMPT_CHUNK = 64
PROMPT_SEQS = (4, 4)
SAMPLE_SEQS = (8, 8)

NN = (((1,), (0,)), ((), ()))
NT = (((1,), (1,)), ((), ()))
TN = (((0,), (0,)), ((), ()))


def _mm(a, b, dims=NN):
    return lax.dot_general(a.astype(BF16), b.astype(BF16), dims, preferred_element_type=F32)


def _dotb(a, b, dims=NN):
    return lax.dot_general(a, b, dims, preferred_element_type=F32)


def _split3(x):
    hi = x.astype(BF16)
    rem = x - hi.astype(F32)
    mid = rem.astype(BF16)
    return hi, mid, (rem - mid.astype(F32)).astype(BF16)


def _cumsum_rows(tril01, x):
    hi, mid, lo = _split3(x)
    return _dotb(tril01, hi) + _dotb(tril01, mid) + _dotb(tril01, lo)


def _cumsum_lanes(x, triu01):
    hi, mid, lo = _split3(x)
    return _dotb(hi, triu01) + _dotb(mid, triu01) + _dotb(lo, triu01)


def _sigmoid(x):
    return 0.5 + 0.5 * jnp.tanh(0.5 * x)


def _silu(x):
    h = 0.5 * x
    return h + h * jnp.tanh(h)


def _softplus(x):
    return jnp.maximum(x, 0.0) + jnp.log(1.0 + jnp.exp(-jnp.abs(x)))


def _shift_rows(cur, prev, k):
    rolled = pltpu.roll(cur, k, 0)
    fix = pltpu.roll(prev, k, 0)
    wrap = lax.broadcasted_iota(jnp.int32, fix.shape, 0) < k
    head = jnp.where(wrap, fix, rolled[0:SUBLANES])
    if cur.shape[0] == SUBLANES:
        return head
    return jnp.concatenate([head, rolled[SUBLANES:]], axis=0)


class _PairMasks:
    def __init__(self, c):
        row = lax.broadcasted_iota(jnp.int32, (c, 2 * c), 0)
        lane = lax.broadcasted_iota(jnp.int32, (c, 2 * c), 1)
        col = lane & (c - 1)
        self.tril = row >= col
        self.strict = row > col
        diff = row ^ col
        self.levels = [(diff >> 1) == 0]
        sh = 2
        while (1 << sh) <= c:
            self.levels.append((diff >> (sh - 1)) == 1)
            sh += 1
        self.head0 = lane < c
        self.tok0 = self.head0.astype(F32).astype(BF16)
        self.tok1 = (lane >= c).astype(F32).astype(BF16)

    def block_diag(self, xb):
        return jnp.concatenate([xb * self.tok0, xb * self.tok1], axis=0)


def _tri_inv_pairs(lows, pm):
    nn = [-jnp.where(pm.levels[0], x, 0.0) for x in lows]
    for m in pm.levels[1:]:
        off = [jnp.where(m, x, 0.0) for x in lows]
        nnb = [x.astype(BF16) for x in nn]
        xc = [o + _dotb(xb, pm.block_diag(o.astype(BF16))) for o, xb in zip(off, nnb)]
        nn = [x - (y + _dotb(y.astype(BF16), pm.block_diag(xb)))
              for x, y, xb in zip(nn, xc, nnb)]
    return nn


def _proj_kernel(x_ref, lnw_ref, w_ref, o_ref, h_ref, *, apply_norm):
    @pl.when(pl.program_id(1) == 0)
    def _():
        x = x_ref[...]
        if apply_norm:
            x = x * lax.rsqrt(jnp.mean(x * x, axis=-1, keepdims=True) + EPS)
            x = x * lnw_ref[...]
        h_ref[...] = x.astype(BF16)

    o_ref[...] = lax.dot_general(h_ref[...], w_ref[...], NT, preferred_element_type=F32)


def _proj(x, lnw, w, *, apply_norm):
    m, d = x.shape
    n = w.shape[0]
    tm = min(m, PROJ_TM)
    tn = PROJ_TN
    assert m % tm == 0 and n % tn == 0
    vmem = 2 * (tm * d * 4 + tn * d * 2 + tm * tn * 4) + tm * d * 2 + (4 << 20)
    return pl.pallas_call(
        functools.partial(_proj_kernel, apply_norm=apply_norm),
        out_shape=jax.ShapeDtypeStruct((m, n), F32),
        grid=(m // tm, n // tn),
        in_specs=[pl.BlockSpec((tm, d), lambda i, j: (i, 0)),
                  pl.BlockSpec((1, d), lambda i, j: (0, 0)),
                  pl.BlockSpec((tn, d), lambda i, j: (j, 0))],
        out_specs=pl.BlockSpec((tm, tn), lambda i, j: (i, j)),
        scratch_shapes=[pltpu.VMEM((tm, d), BF16)],
        compiler_params=pltpu.CompilerParams(
            dimension_semantics=("parallel", "arbitrary"), vmem_limit_bytes=vmem),
        name="proj",
    )(x, lnw, w)


def _rmsnorm_kernel(x_ref, w_ref, o_ref):
    x = x_ref[...]
    o_ref[...] = x * lax.rsqrt(jnp.mean(x * x, axis=-1, keepdims=True) + EPS) * w_ref[...]


def _rmsnorm_rows(x, w):
    return pl.pallas_call(
        _rmsnorm_kernel,
        out_shape=jax.ShapeDtypeStruct(x.shape, F32),
        name="rmsnorm_rows",
    )(x, w)


def _gdn_kernel(qkv_ref, z_ref, ab_ref, abt_ref, s0_ref, prev0_ref, convw_ref,
                alr_ref, dtr_ref, alc_ref, dtc_ref, nw_ref,
                o_ref, s_ref, prev_ref, *, c, ns, nh, dk, dv, t_valid, t_total):
    ci = pl.program_id(1)
    shared = s0_ref.shape[0] == 1

    @pl.when(ci == 0)
    def _():
        for j in range(ns):
            s_ref[j] = s0_ref[0 if shared else j]
            _init_carry(prev_ref, prev0_ref, j, 0 if shared else j)

    pm = _PairMasks(c)
    row = lax.broadcasted_iota(jnp.int32, (c, c), 0)
    col = lax.broadcasted_iota(jnp.int32, (c, c), 1)
    tril = row >= col
    kw = convw_ref.shape[0]
    seqs = range(ns)

    ab = [ab_ref[j] for j in seqs]
    g_c = [-jnp.exp(alr_ref[...]) * _softplus(x + dtr_ref[...]) for x in ab]
    beta_c = [_sigmoid(x) for x in ab]
    abt = [abt_ref[j, 0] for j in seqs]
    g_r = [-jnp.exp(alc_ref[...]) * _softplus(x + dtc_ref[...]) for x in abt]
    if t_valid < t_total:
        tok_c = ci * c + lax.broadcasted_iota(jnp.int32, ab[0].shape, 0)
        tok_r = ci * c + lax.broadcasted_iota(jnp.int32, abt[0].shape, 1)
        g_c = [jnp.where(tok_c < t_valid, x, 0.0) for x in g_c]
        beta_c = [jnp.where(tok_c < t_valid, x, 0.0) for x in beta_c]
        g_r = [jnp.where(tok_r < t_valid, x, 0.0) for x in g_r]
    gc_c = _cumsum_rows(tril.astype(BF16), jnp.concatenate(g_c, axis=1))
    gc_r = _cumsum_lanes(jnp.concatenate(g_r, axis=0), (row <= col).astype(BF16))

    def conv(j, col0, width):
        cur = qkv_ref[j, :, col0:col0 + width]
        prev = prev_ref[j, :, col0:col0 + width]
        acc = None
        for k in range(kw - 1, -1, -1):
            src = cur if k == 0 else _shift_rows(cur, prev, k)
            term = src * convw_ref[kw - 1 - k:kw - k, col0:col0 + width]
            acc = term if acc is None else acc + term
        return _silu(acc)

    units = [(j, h) for j in seqs for h in range(nh)]
    us = range(len(units))
    gcc = [gc_c[:, j * LANES + h:j * LANES + h + 1] for j, h in units]
    gcr = [gc_r[j * 2 * nh + h:j * 2 * nh + h + 1, :] for j, h in units]
    beta = [beta_c[j][:, nh + h:nh + h + 1] for j, h in units]
    glast = [x[c - 1:c, :] for x in gcc]
    q = [conv(j, h * dk, dk) for j, h in units]
    k = [conv(j, nh * dk + h * dk, dk) for j, h in units]
    v = [conv(j, 2 * nh * dk + h * dv, dv) for j, h in units]
    q = [x * lax.rsqrt(jnp.sum(x * x, axis=-1, keepdims=True) + L2_EPS) * (dk ** -0.5) for x in q]
    k = [x * lax.rsqrt(jnp.sum(x * x, axis=-1, keepdims=True) + L2_EPS) for x in k]
    kb = [k[u] * beta[u] for u in us]
    vb = [v[u] * beta[u] for u in us]
    kbg = [kb[u] * jnp.exp(gcc[u]) for u in us]
    qg = [q[u] * jnp.exp(gcc[u]) for u in us]
    k_tail = [k[u] * jnp.exp(glast[u] - gcc[u]) for u in us]

    pairs = [(u, u + 1) for u in us if u % 2 == 0]

    def side_by_side(xs):
        return [jnp.concatenate([xs[a].astype(BF16), xs[b].astype(BF16)], axis=1) for a, b in pairs]

    def per_head_rows(xs):
        out = []
        for a, b in pairs:
            xa, xb = xs[a].astype(BF16), xs[b].astype(BF16)
            out.append(jnp.concatenate([jnp.concatenate([xa, jnp.zeros_like(xb)], axis=1),
                                        jnp.concatenate([jnp.zeros_like(xa), xb], axis=1)], axis=0))
        return out

    def split(xs, width):
        return [x[:, i * width:(i + 1) * width] for x in xs for i in range(2)]

    gcc_s = [jnp.where(pm.head0, gcc[a], gcc[b]) for a, b in pairs]
    gcr_s = [jnp.concatenate([gcr[a], gcr[b]], axis=1) for a, b in pairs]
    decay = [jnp.where(pm.tril, jnp.exp(jnp.where(pm.tril, x - y, 0.0)), 0.0)
             for x, y in zip(gcc_s, gcr_s)]
    k_rows = per_head_rows(k)
    lhs = [jnp.concatenate([x, y], axis=0) for x, y in zip(side_by_side(kb), side_by_side(q))]
    both = [_dotb(x, y, NT) for x, y in zip(lhs, k_rows)]
    lower = [jnp.where(pm.strict, x[0:c] * d_, 0.0) for x, d_ in zip(both, decay)]
    qk = [jnp.where(pm.tril, x[c:2 * c] * d_, 0.0) for x, d_ in zip(both, decay)]
    s_old = [s_ref[j, h] for j, h in units]
    qs = [_mm(qg[u], s_old[u]) for u in us]
    nn = [x.astype(BF16) for x in _tri_inv_pairs(lower, pm)]
    u_ = [vb[u] + x for u, x in enumerate(split(
        [_dotb(x, y) for x, y in zip(nn, per_head_rows(vb))], dv))]
    w_ = [kbg[u] + x for u, x in enumerate(split(
        [_dotb(x, y) for x, y in zip(nn, per_head_rows(kbg))], dk))]
    v_new = [u_[u] - _mm(w_[u], s_old[u]) for u in us]
    kv = [_mm(k_tail[u], v_new[u], TN) for u in us]
    for u, (j, h) in enumerate(units):
        s_ref[j, h] = s_old[u] * jnp.exp(glast[u]) + kv[u]
    o = [qs[u] + x for u, x in enumerate(split(
        [_dotb(x.astype(BF16), y) for x, y in zip(qk, per_head_rows(v_new))], dv))]
    for u, (j, h) in enumerate(units):
        oh = o[u] * lax.rsqrt(jnp.mean(o[u] * o[u], axis=-1, keepdims=True) + EPS) * nw_ref[...]
        oh = oh * _silu(z_ref[j, :, h * dv:(h + 1) * dv])
        o_ref[j, :, h * dv:(h + 1) * dv] = oh.astype(BF16)

    for j in seqs:
        prev_ref[j] = qkv_ref[j, c - SUBLANES:c, :]


def _state_specs(s0, ns, tails):
    shared = s0.shape[0] == 1
    lead = 1 if shared else ns
    idx4 = (lambda b, i: (0, 0, 0, 0)) if shared else (lambda b, i: (b, 0, 0, 0))
    idx3 = (lambda b, i: (0, 0, 0)) if shared else (lambda b, i: (0, b, 0))
    return ([pl.BlockSpec((lead,) + s0.shape[1:], idx4)]
            + [pl.BlockSpec((tl.shape[0], lead, tl.shape[2]), idx3) for tl in tails])


def _init_carry(prev_ref, tail_ref, j, jj):
    k = tail_ref.shape[0]
    if k < SUBLANES:
        prev_ref[j] = jnp.zeros(prev_ref.shape[1:], prev_ref.dtype)
    for r in range(k):
        row = SUBLANES - k + r
        prev_ref[j, row:row + 1, :] = tail_ref[r, jj:jj + 1, :]


def _gdn_parts(p, abt, s0, prev0, convw, alr, dtr, alc, dtc, nw, *, c, ns, t_valid, lay):
    nb, t, _ = p.shape
    nh, dk, dv = s0.shape[1:]
    wq = lay["qkv"][1]
    const2 = lambda b, i: (0, 0)
    return dict(
        operands=[p, p, p, abt, s0, prev0, convw, alr, dtr, alc, dtc, nw],
        in_specs=[
            pl.BlockSpec((ns, c, wq), lambda b, i: (b, i, lay["qkv"][0] // wq)),
            pl.BlockSpec((ns, c, nh * dv), lambda b, i: (b, i, lay["za"][0] // (nh * dv))),
            pl.BlockSpec((ns, c, LANES), lambda b, i: (b, i, lay["ab"][0] // LANES)),
            pl.BlockSpec((ns, 1, 2 * nh, c), lambda b, i: (b, i, 0, 0)),
            *_state_specs(s0, ns, [prev0]),
            pl.BlockSpec(convw.shape, const2),
            pl.BlockSpec(alr.shape, const2),
            pl.BlockSpec(dtr.shape, const2),
            pl.BlockSpec(alc.shape, const2),
            pl.BlockSpec(dtc.shape, const2),
            pl.BlockSpec(nw.shape, const2),
        ],
        out_shape=[jax.ShapeDtypeStruct((nb, t, nh * dv), BF16),
                   jax.ShapeDtypeStruct((nb, nh, dk, dv), F32)],
        out_specs=[pl.BlockSpec((ns, c, nh * dv), lambda b, i: (b, i, 0)),
                   pl.BlockSpec((ns, nh, dk, dv), lambda b, i: (b, 0, 0, 0))],
        scratch=[pltpu.VMEM((ns, SUBLANES, wq), F32)],
        statics=dict(c=c, ns=ns, nh=nh, dk=dk, dv=dv, t_valid=t_valid, t_total=t))


def _token_shift(x_ref, prev_ref, mu_ref, j, lo, hi):
    cur = x_ref[j, :, lo:hi]
    prv = _shift_rows(cur, prev_ref[j, :, lo:hi], 1)
    return cur + mu_ref[:, lo:hi] * (prv - cur)


_RWKV_OPERANDS = ("kkg", "rg", "bh", "kh", "kt", "bt", "v")


def _rwkv_kernel(x_ref, wa_ref, s0_ref, prev0_ref, prevwa0_ref, mu_ref, muwa_ref,
                 w0_ref, w2_ref, a0_ref, a2_ref, kk_ref, ka_ref, rk_ref, gnw_ref, gnb_ref,
                 o_ref, s_ref, prev_ref, prevwa_ref, sbd_ref,
                 *, c, ns, nh, n, t_valid, t_total):
    step = pl.program_id(1)
    d = nh * n
    npair = nh // 2
    ngrp = d // LANES
    shared = s0_ref.shape[0] == 1
    seqs = range(ns)

    @pl.when(step == 0)
    def _():
        zero = jnp.zeros((n, n), F32)
        for j in seqs:
            jj = 0 if shared else j
            for p in range(npair):
                top = jnp.concatenate([s0_ref[jj, 2 * p], zero], axis=1)
                bot = jnp.concatenate([zero, s0_ref[jj, 2 * p + 1]], axis=1)
                sbd_ref[j, p] = jnp.concatenate([top, bot], axis=0)
            _init_carry(prev_ref, prev0_ref, j, jj)
            _init_carry(prevwa_ref, prevwa0_ref, j, jj)

    w2 = 2 * c
    pm = _PairMasks(c)
    tril, strict = pm.tril, pm.strict
    chl = lax.broadcasted_iota(jnp.int32, (c, LANES), 1)
    ch0 = (chl < n).astype(F32).astype(BF16)
    ch1 = (chl >= n).astype(F32).astype(BF16)
    row2 = lax.broadcasted_iota(jnp.int32, (LANES, LANES), 0)
    lane2 = lax.broadcasted_iota(jnp.int32, (LANES, LANES), 1)
    log2n = n.bit_length() - 1
    same_head = (row2 >> log2n) == (lane2 >> log2n)
    seg_ones = same_head.astype(F32).astype(BF16)

    def bd_ch(xb):
        return jnp.concatenate([xb * ch0, xb * ch1], axis=0)

    dot = _dotb

    def seg_sum(x):
        stk = jnp.concatenate([x[:, i * LANES:(i + 1) * LANES] for i in range(ngrp)], axis=0)
        acc = dot(stk.astype(BF16), seg_ones)
        return jnp.concatenate([acc[i * c:(i + 1) * c] for i in range(ngrp)], axis=1)

    lora = w2_ref.shape[0]
    row_c = lax.broadcasted_iota(jnp.int32, (c, c), 0)
    col_c = lax.broadcasted_iota(jnp.int32, (c, c), 1)
    tril_b = (row_c >= col_c).astype(BF16)

    def pre_steps(out):
        r = [_token_shift(x_ref, prev_ref, mu_ref, j, 0, d) for j in seqs]
        kb = [_token_shift(x_ref, prev_ref, mu_ref, j, d, 2 * d) for j in seqs]
        vb = [_token_shift(x_ref, prev_ref, mu_ref, j, 2 * d, 3 * d) for j in seqs]
        sz = [_silu(_token_shift(x_ref, prev_ref, mu_ref, j, 3 * d, 4 * d)) for j in seqs]
        wa = [wa_ref[j] for j in seqs]
        wa = [wa[j] + muwa_ref[...] * (_shift_rows(wa[j], prevwa_ref[j], 1) - wa[j])
              for j in seqs]
        tw = [jnp.tanh(x[:, 0:lora]) for x in wa]
        w_raw = [w0_ref[...] + _mm(tw[j], w2_ref[...]) for j in seqs]
        a = [_sigmoid(a0_ref[...] + _mm(wa[j][:, lora:], a2_ref[...])) for j in seqs]
        w_log = [-jnp.exp(-_softplus(-x) - 0.5) for x in w_raw]
        kku = [kb[j] * kk_ref[...] for j in seqs]
        k2 = [kb[j] * (1.0 + (a[j] - 1.0) * ka_ref[...]) for j in seqs]
        if t_valid < t_total:
            ok = (step * c + lax.broadcasted_iota(jnp.int32, (c, d), 0)) < t_valid
            w_log = [jnp.where(ok, x, 0.0) for x in w_log]
            kku = [jnp.where(ok, x, 0.0) for x in kku]
            vb = [jnp.where(ok, x, 0.0) for x in vb]
        gcum = [_cumsum_rows(tril_b, x) for x in w_log]
        glast = [x[c - 1:c, :] for x in gcum]
        g_in = [jnp.exp(x) for x in gcum]
        g_ex = [jnp.exp(gcum[j] - w_log[j]) for j in seqs]
        g_inv = [jnp.exp(-x) for x in gcum]
        g_tail = [jnp.exp(glast[j] - gcum[j]) for j in seqs]
        ssq = [x * x for x in kku]
        bonus_in = [r[j] * k2[j] * rk_ref[...] for j in seqs]
        kk = [kku[j] * lax.rsqrt(seg_sum(ssq[j]) + L2_EPS) for j in seqs]
        bonus = [seg_sum(bonus_in[j]) * vb[j] for j in seqs]
        for j in seqs:
            b = kk[j] * a[j]
            out[j].update(
                kkg=(kk[j] * g_ex[j]).astype(BF16), rg=(r[j] * g_in[j]).astype(BF16),
                bh=(b * g_inv[j]).astype(BF16), kh=(k2[j] * g_inv[j]).astype(BF16),
                kt=(k2[j] * g_tail[j]).astype(BF16), bt=(b * g_tail[j]).astype(BF16),
                v=vb[j].astype(BF16), bonus=bonus[j], sz=sz[j], g_all=jnp.exp(glast[j]))

    units = [(j, p) for j in seqs for p in range(npair)]
    us = range(len(units))

    def stage_steps(ops):
        kkg, rg, bh, kh, kt, bt, v = (
            [ops[j][k][:, p * LANES:(p + 1) * LANES] for j, p in units] for k in _RWKV_OPERANDS)
        g_all = [ops[j]["g_all"][:, p * LANES:(p + 1) * LANES] for j, p in units]
        lhs = [jnp.concatenate([kkg[u], rg[u]], axis=0) for u in us]
        rhs = [jnp.concatenate([bd_ch(bh[u]), bd_ch(kh[u])], axis=0) for u in us]
        big = [dot(lhs[u], rhs[u], NT) for u in us]
        s_old = [sbd_ref[j, p] for j, p in units]
        from_s = [dot(lhs[u], s_old[u].astype(BF16), NT) for u in us]
        low = [jnp.where(strict, x[0:c, 0:w2], 0.0) for x in big]
        m_kv = [jnp.where(strict, x[0:c, w2:2 * w2], 0.0) for x in big]
        q_b = [jnp.where(tril, x[c:2 * c, 0:w2], 0.0) for x in big]
        p_kv = [jnp.where(tril, x[c:2 * c, w2:2 * w2], 0.0) for x in big]
        mp = [dot(jnp.concatenate([m_kv[u], p_kv[u]], axis=0).astype(BF16), bd_ch(v[u]))
              for u in us]
        nn = _tri_inv_pairs(low, pm)
        rhs_sa = [from_s[u][0:c] + mp[u][0:c] for u in us]
        sa = [rhs_sa[u] + dot(nn[u].astype(BF16), bd_ch(rhs_sa[u].astype(BF16))) for u in us]
        sab = [x.astype(BF16) for x in sa]
        qs = [dot(q_b[u].astype(BF16), bd_ch(sab[u])) for u in us]
        upd = [dot(jnp.concatenate([v[u], -sab[u]], axis=0),
                   jnp.concatenate([kt[u], bt[u]], axis=0), TN) for u in us]
        s_new = [s_old[u] * g_all[u] + jnp.where(same_head, upd[u], 0.0) for u in us]
        for u, (j, p) in enumerate(units):
            sbd_ref[j, p] = s_new[u]

        @pl.when(step == pl.num_programs(1) - 1)
        def _():
            for u, (j, p) in enumerate(units):
                s_ref[j, 2 * p] = s_new[u][0:n, 0:n]
                s_ref[j, 2 * p + 1] = s_new[u][n:2 * n, n:2 * n]

        y = [from_s[u][c:2 * c] + mp[u][c:2 * c] - qs[u] for u in us]
        yj = [jnp.concatenate(y[j * npair:(j + 1) * npair], axis=1) for j in seqs]
        yc = [yj[j] - seg_sum(yj[j]) * (1.0 / n) for j in seqs]
        var = [seg_sum(yc[j] * yc[j]) * (1.0 / n) for j in seqs]
        for j in seqs:
            yn = yc[j] * lax.rsqrt(var[j] + n * GN_EPS_PER_CH) * gnw_ref[...] + gnb_ref[...]
            o_ref[j] = ((yn + ops[j]["bonus"]) * ops[j]["sz"]).astype(BF16)

    ops = [dict() for _ in seqs]
    pre_steps(ops)
    stage_steps(ops)

    for j in seqs:
        prev_ref[j] = x_ref[j, c - SUBLANES:c, :]
        prevwa_ref[j] = wa_ref[j, c - SUBLANES:c, :]


def _rwkv_parts(p, s0, prev0, prevwa0, mu, muwa, w0, w2, a0, a2, kk, ka, rk, gnw, gnb,
                *, c, ns, t_valid, lay):
    nb, t, _ = p.shape
    nh, n = s0.shape[1:3]
    d = nh * n
    wx = lay["rkvz"][1]
    const2 = lambda b, i: (0, 0)
    vec = pl.BlockSpec((1, d), const2)
    assert 2 * n == LANES and c <= n and c & (c - 1) == 0, "two heads per lane group"
    return dict(
        operands=[p, p, s0, prev0, prevwa0, mu, muwa, w0, w2, a0, a2, kk, ka, rk, gnw, gnb],
        in_specs=[
            pl.BlockSpec((ns, c, wx), lambda b, i: (b, i, lay["rkvz"][0] // wx)),
            pl.BlockSpec((ns, c, LANES), lambda b, i: (b, i, lay["wa"][0] // LANES)),
            *_state_specs(s0, ns, [prev0, prevwa0]),
            pl.BlockSpec((1, wx), const2),
            pl.BlockSpec((1, LANES), const2),
            vec,
            pl.BlockSpec(w2.shape, const2),
            vec,
            pl.BlockSpec(a2.shape, const2),
            vec, vec, vec, vec, vec,
        ],
        out_shape=[jax.ShapeDtypeStruct((nb, t, d), BF16),
                   jax.ShapeDtypeStruct((nb, nh, n, n), F32)],
        out_specs=[pl.BlockSpec((ns, c, d), lambda b, i: (b, i, 0)),
                   pl.BlockSpec((ns, nh, n, n), lambda b, i: (b, 0, 0, 0))],
        scratch=[pltpu.VMEM((ns, SUBLANES, wx), F32), pltpu.VMEM((ns, SUBLANES, LANES), F32),
                 pltpu.VMEM((ns, nh // 2, LANES, LANES), F32)],
        statics=dict(c=c, ns=ns, nh=nh, n=n, t_valid=t_valid, t_total=t))


def _recurrence(body, parts, *, name):
    statics = parts["statics"]
    nb, t = parts["out_shape"][0].shape[:2]
    return pl.pallas_call(
        functools.partial(body, **statics),
        out_shape=tuple(parts["out_shape"]),
        grid=(nb // statics["ns"], t // statics["c"]),
        in_specs=parts["in_specs"],
        out_specs=tuple(parts["out_specs"]),
        scratch_shapes=parts["scratch"],
        compiler_params=pltpu.CompilerParams(
            dimension_semantics=("parallel", "arbitrary"), vmem_limit_bytes=VMEM_LIMIT),
        name=name,
    )(*parts["operands"])


def _merge_kernel(oa_ref, ob_ref, gate_ref, x_ref, woa_ref, wob_ref, wo_ref, lnf_ref, y_ref):
    d = x_ref.shape[1]
    ba = jnp.dot(oa_ref[...], woa_ref[...], preferred_element_type=F32)
    bb = jnp.dot(ob_ref[...], wob_ref[...], preferred_element_type=F32)
    gates = _sigmoid(gate_ref[...])
    merged = gates[:, :d] * ba + gates[:, d:] * bb
    xn = x_ref[...] + jnp.dot(merged.astype(BF16), wo_ref[...], preferred_element_type=F32)
    y_ref[...] = xn * lax.rsqrt(jnp.mean(xn * xn, axis=-1, keepdims=True) + EPS) * lnf_ref[...]


def _merge(oa, ob, p, x, woa, wob, wo, lnf, *, lay):
    m, d = x.shape
    tm = min(m, 512)
    wg = lay["gate"][1]
    const2 = lambda i: (0, 0)
    return pl.pallas_call(
        _merge_kernel,
        out_shape=jax.ShapeDtypeStruct((m, d), F32),
        grid=(m // tm,),
        in_specs=[
            pl.BlockSpec((tm, oa.shape[1]), lambda i: (i, 0)),
            pl.BlockSpec((tm, ob.shape[1]), lambda i: (i, 0)),
            pl.BlockSpec((tm, wg), lambda i: (i, lay["gate"][0] // wg)),
            pl.BlockSpec((tm, d), lambda i: (i, 0)),
            pl.BlockSpec(woa.shape, const2),
            pl.BlockSpec(wob.shape, const2),
            pl.BlockSpec(wo.shape, const2),
            pl.BlockSpec((1, d), const2),
        ],
        out_specs=pl.BlockSpec((tm, d), lambda i: (i, 0)),
        compiler_params=pltpu.CompilerParams(
            dimension_semantics=("parallel",), vmem_limit_bytes=VMEM_LIMIT),
        name="merge",
    )(oa, ob, p, x, woa, wob, wo, lnf)


def _pad_lanes(v, width):
    return jnp.pad(v, ((0, 0), (0, width - v.shape[1])))


def _token_major_t(ab, c):
    nb, t, k = ab.shape
    return jnp.transpose(ab.reshape(nb, t // c, c, k), (0, 1, 3, 2))


def kernel(x_prompt, x_sample, state_gdn, state_gdn_conv, state_rwkv, state_shift, meta_tokens,
           ln1_w, w_in, gdn_conv_w, gdn_a_log, gdn_dt_bias, gdn_norm_w, w_out_a, rwkv_mu, rwkv_w0,
           rwkv_w2, rwkv_a0, rwkv_a2, rwkv_k_k, rwkv_k_a, rwkv_r_k, rwkv_gn_w, rwkv_gn_b, w_out_b,
           w_out, lnf_w):
    assert ln1_w.shape[0] == 1, "single-layer trunk"
    bp, seq, d = x_prompt.shape
    bs, tseq, _ = x_sample.shape
    n_meta = meta_tokens.shape[0]
    _, _, nh_a, dk, dv = state_gdn.shape
    _, _, nh_b, n_b, _ = state_rwkv.shape
    kw, w_qkv = gdn_conv_w.shape[1:]
    lora_w = rwkv_w2.shape[1]
    lora_a = rwkv_a2.shape[1]
    d_a = nh_a * dv
    d_b = nh_b * n_b
    assert w_qkv == 2 * nh_a * dk + d_a and d_a == d and d_b == d and lora_w + lora_a == LANES
    assert n_meta & (n_meta - 1) == 0 and n_meta >= SUBLANES and kw - 1 <= tseq
    assert seq % PROMPT_CHUNK == 0 and all(bp % k == 0 for k in PROMPT_SEQS)
    assert all(bs % k == 0 for k in SAMPLE_SEQS)

    o_a = w_qkv
    o_b = o_a + nh_a
    o_z = o_b + nh_a
    o_r = o_z + d_a
    o_g = o_r + 3 * d_b + lora_w + lora_a + d_b
    w = w_in[0].T
    wr = w[o_r:o_g]
    mu = rwkv_mu
    lora0 = 3 * d_b
    rows = [w[:w_qkv], w[o_z:o_r], wr[:lora0], wr[lora0 + LANES:], w[o_g:],
            wr[lora0:lora0 + LANES], w[o_a:o_z]]
    used = sum(rw.shape[0] for rw in rows)
    n_pad = -(-(used + LANES - 2 * nh_a) // PROJ_TN) * PROJ_TN
    rows.append(jnp.zeros((n_pad - used, d), F32))
    w_all = jnp.concatenate([rw.astype(BF16) for rw in rows], axis=0)
    lay = {"qkv": (0, w_qkv), "za": (w_qkv, d_a), "rkvz": (w_qkv + d_a, 4 * d_b),
           "gate": (w_qkv + d_a + 4 * d_b, 2 * d)}
    lay["wa"] = (lay["gate"][0] + 2 * d, LANES)
    lay["ab"] = (lay["wa"][0] + LANES, LANES)
    for off, width in lay.values():
        assert off % width == 0
    mu_x = jnp.concatenate([mu[:, :lora0], mu[:, lora0 + LANES:]], axis=1)
    mu_wa = mu[:, lora0:lora0 + LANES]

    alr = _pad_lanes(gdn_a_log, LANES)
    dtr = _pad_lanes(gdn_dt_bias, LANES)
    alc = jnp.pad(gdn_a_log.reshape(nh_a, 1), ((0, nh_a), (0, 0)))
    dtc = jnp.pad(gdn_dt_bias.reshape(nh_a, 1), ((0, nh_a), (0, 0)))
    convw = gdn_conv_w[0]
    rk = rwkv_r_k.reshape(1, d_b)
    woa = w_out_a[0].astype(BF16)
    wob = w_out_b[0].astype(BF16)
    wo = w_out[0].astype(BF16)
    lnf = lnf_w.reshape(1, d)

    def branches(p, c, ns_a, ns_b, t_valid, s_gdn, conv_tail, s_rwkv, x_tail, wa_tail):
        abt = _token_major_t(p[:, :, lay["ab"][0]:lay["ab"][0] + 2 * nh_a], c)
        oa, sg = _recurrence(_gdn_kernel, _gdn_parts(
            p, abt, s_gdn, conv_tail, convw, alr, dtr, alc, dtc, gdn_norm_w,
            c=c, ns=ns_a, t_valid=t_valid, lay=lay), name="gdn")
        ob, sr = _recurrence(_rwkv_kernel, _rwkv_parts(
            p, s_rwkv, x_tail, wa_tail, mu_x, mu_wa, rwkv_w0, rwkv_w2[0], rwkv_a0, rwkv_a2[0],
            rwkv_k_k, rwkv_k_a, rk, rwkv_gn_w, rwkv_gn_b,
            c=c, ns=ns_b, t_valid=t_valid, lay=lay), name="rwkv")
        return oa, sg, ob, sr

    x0, wx = lay["rkvz"]
    a0_, _ = lay["wa"]

    tpad = SUBLANES
    assert n_meta <= bs and tseq < tpad
    rider = jnp.zeros((bs, tpad - tseq, d), F32).at[:n_meta, 0].set(meta_tokens)
    xs = jnp.concatenate([x_sample, rider], axis=1).reshape(bs * tpad, d)
    p_s = _proj(xs, ln1_w, w_all, apply_norm=True)
    p_m = p_s.reshape(bs, tpad, n_pad)[:n_meta, tseq]

    _, sg_m, _, sr_m = branches(
        p_m[None], n_meta, 1, 1, n_meta,
        jnp.zeros((1, nh_a, dk, dv), F32), jnp.zeros((1, 1, w_qkv), F32),
        jnp.zeros((1, nh_b, n_b, n_b), F32), jnp.zeros((1, 1, wx), F32),
        jnp.zeros((1, 1, LANES), F32))

    xp = x_prompt.reshape(bp * seq, d)
    p_p = _proj(xp, ln1_w, w_all, apply_norm=True)
    last_m = p_m[n_meta - 1:, None]
    oa_p, sg_p, ob_p, sr_p = branches(
        p_p.reshape(bp, seq, n_pad), PROMPT_CHUNK, *PROMPT_SEQS, seq, sg_m,
        p_m[n_meta - (kw - 1):, None, :w_qkv], sr_m,
        last_m[:, :, x0:x0 + wx], last_m[:, :, a0_:a0_ + LANES])
    y_p = _merge(oa_p.reshape(bp * seq, d), ob_p.reshape(bp * seq, d), p_p, xp, woa, wob, wo, lnf,
                 lay=lay)

    p_first =_proj(state_shift[0], ln1_w, w_all, apply_norm=False)[None]
    oa_s, sg_s, ob_s, sr_s = branches(
        p_s.reshape(bs, tpad, n_pad), tpad, *SAMPLE_SEQS, tseq, state_gdn[0],
        jnp.transpose(state_gdn_conv[0], (1, 0, 2)), state_rwkv[0],
        p_first[:, :, x0:x0 + wx], p_first[:, :, a0_:a0_ + LANES])
    y_s = _merge(oa_s.reshape(bs * tpad, d), ob_s.reshape(bs * tpad, d), p_s, xs, woa, wob, wo,
                 lnf, lay=lay)

    shift_p = _rmsnorm_rows(x_prompt[:, -1], ln1_w)
    shift_s = _rmsnorm_rows(x_sample[:, -1], ln1_w)
    conv_p = p_p.reshape(bp, seq, n_pad)[:, seq - (kw - 1):, :w_qkv]
    conv_s = p_s.reshape(bs, tpad, n_pad)[:, tseq - (kw - 1):tseq, :w_qkv]
    return (y_p.reshape(bp, seq, d), y_s.reshape(bs, tpad, d)[:, :tseq],
            sg_p[None], conv_p[None], sr_p[None], shift_p[None],
            sg_s[None], conv_s[None], sr_s[None], shift_s[None])
```
